```python
import jax, jax.numpy as jnp
from jax import lax
import numpy as np


D_MODEL = 2048
BATCH = 8
SEQ = 4096
DEPTH = 2
DEC_BATCH = 4
DEC_SEQ = 4096
PAST_LEN = 128

ATT_GROUPS = ((128, 1), (512, 4), (2048, 16))
N_GROUPS = len(ATT_GROUPS)
ATT_HEADS = 16
ATT_HEAD_DIM = 128
ATT_WIDTH = ATT_HEADS * ATT_HEAD_DIM
ATT_IN_COLS = 3 * N_GROUPS * ATT_WIDTH + ATT_WIDTH
ROPE_THETA = 500000.0
ROPE_DIMS = ATT_HEAD_DIM // 4

M_HEADS = 8
M_QK_DIM = 256
M_V_DIM = 512
M_QK_WIDTH = M_HEADS * M_QK_DIM
M_V_WIDTH = M_HEADS * M_V_DIM
M_IN_COLS = 2 * M_QK_WIDTH + 3 * M_V_WIDTH + 4 * M_HEADS
M_CHUNK = 128
F_BIAS_LO = 3.0
F_BIAS_HI = 6.0

NORM_EPS = 1e-6
NEG_INF = -1e30

kernel_name = 'hybrid_dilated_attn_mlstm_encoder'


def rms_norm(x, g):
    xf = x.astype(jnp.float32)
    y = xf * lax.rsqrt(jnp.mean(xf * xf, axis=-1, keepdims=True) + NORM_EPS) * g.astype(jnp.float32)
    return y.astype(x.dtype)


def partial_rope(t):
    S = t.shape[1]
    half = ROPE_DIMS // 2
    inv = jnp.power(ROPE_THETA, -jnp.arange(half, dtype=jnp.float32) / half)
    ang = jnp.arange(S, dtype=jnp.float32)[:, None] * inv[None, :]
    cos = jnp.cos(ang)[None, :, None, :]
    sin = jnp.sin(ang)[None, :, None, :]
    t1 = t[..., :half]
    t2 = t[..., half:ROPE_DIMS]
    return jnp.concatenate([t1 * cos - t2 * sin, t2 * cos + t1 * sin, t[..., ROPE_DIMS:]], axis=-1)


def neighbours(t, axis):
    n = t.shape[axis]
    pads = [(0, 0)] * t.ndim
    pads[axis] = (1, 1)
    tp = jnp.pad(t, pads)
    return jnp.concatenate([lax.slice_in_dim(tp, o, o + n, axis=axis) for o in range(3)], axis=axis + 1)


def dilated_window_attention(q, k, v, window, dilation):
    B, S, H, hd = q.shape
    half = window // (2 * dilation)
    span = dilation * half
    S_pad = -(-S // span) * span
    pad = S_pad - S
    Ls = S_pad // dilation
    nb = Ls // half

    def to_blocks(t):
        t = jnp.pad(t, ((0, 0), (0, pad), (0, 0), (0, 0)))
        t = t.reshape(B, Ls, dilation, H, hd).transpose(0, 2, 1, 3, 4)
        return t.reshape(B, dilation, nb, half, H, hd)

    qb = to_blocks(q)
    kc = neighbours(to_blocks(k), 2)
    vc = neighbours(to_blocks(v), 2)
    valid = (jnp.arange(S_pad) < S).reshape(Ls, dilation).T.reshape(dilation, nb, half)
    kvalid = neighbours(valid, 1)
    rel = jnp.arange(3 * half)[None, :] - half - jnp.arange(half)[:, None]
    band = jnp.abs(rel) <= half
    mask = band[None, None] & kvalid[:, :, None, :]

    s = jnp.einsum('brnqhd,brnkhd->brnhqk', qb, kc) * (hd ** -0.5)
    s = jnp.where(mask[None, :, :, None], s, NEG_INF)
    m = jnp.max(s, axis=-1, keepdims=True)
    p = jnp.exp(s - m)
    den = jnp.sum(p, axis=-1)
    den_t = den.transpose(0, 1, 2, 4, 3)
    o = jnp.einsum('brnhqk,brnkhd->brnqhd', p, vc) / den_t[..., None]
    lse = m[..., 0].transpose(0, 1, 2, 4, 3) + jnp.log(den_t)

    def from_blocks(t):
        tail = t.shape[4:]
        t = t.reshape((B, dilation, Ls) + tail)
        t = jnp.moveaxis(t, 1, 2).reshape((B, S_pad) + tail)
        return t[:, :S]

    return from_blocks(o), from_blocks(lse)


def dilated_attention_branch(h, w_in, w_out):
    B, S, _ = h.shape
    proj = h @ w_in
    outs = []
    lses = []
    for g, (window, dil) in enumerate(ATT_GROUPS):
        base = 3 * ATT_WIDTH * g
        def part(j):
            sl = proj[..., base + j * ATT_WIDTH: base + (j + 1) * ATT_WIDTH]
            return sl.reshape(B, S, ATT_HEADS, ATT_HEAD_DIM).astype(jnp.float32)
        q = partial_rope(part(0))
        k = partial_rope(part(1))
        o, l = dilated_window_attention(q, k, part(2), window, dil)
        outs.append(o)
        lses.append(l)
    z = proj[..., 3 * ATT_WIDTH * N_GROUPS:]
    wts = jax.nn.softmax(jnp.stack(lses, axis=0), axis=0)
    o = jnp.einsum('gbsh,gbshd->bshd', wts, jnp.stack(outs, axis=0)).reshape(B, S, ATT_WIDTH)
    y = o.astype(h.dtype) * jax.nn.silu(z)
    return y @ w_out


def mlstm_chunkwise(q, k, v, i_pre, log_f):
    B, H, S, dqk = q.shape
    dv = v.shape[-1]
    L = M_CHUNK
    nc = S // L

    def chunks(t):
        return jnp.moveaxis(t.reshape((B, H, nc, L) + t.shape[3:]), 2, 0)

    causal = jnp.tril(jnp.ones((L, L), dtype=bool))

    def step(carry, xs):
        C, n, m = carry
        qc, kc, vc, ic, fc = xs
        b = jnp.cumsum(fc, axis=-1)
        dmat = jnp.where(causal, b[..., :, None] - b[..., None, :] + ic[..., None, :], NEG_INF)
        inter = b + m[..., None]
        m_t = jnp.maximum(jnp.max(dmat, axis=-1), inter)
        a = jnp.exp(dmat - m_t[..., None]) * jnp.einsum('bhtd,bhsd->bhts', qc, kc)
        ginter = jnp.exp(inter - m_t)
        num = jnp.einsum('bhts,bhse->bhte', a, vc) + ginter[..., None] * jnp.einsum('bhtd,bhde->bhte', qc, C)
        den = jnp.sum(a, axis=-1) + ginter * jnp.einsum('bhtd,bhd->bht', qc, n)
        hc = num / jnp.maximum(jnp.abs(den), jnp.exp(-m_t))[..., None]
        bL = b[..., -1]
        gk = bL[..., None] - b + ic
        m_new = jnp.maximum(bL + m, jnp.max(gk, axis=-1))
        decay = jnp.exp(bL + m - m_new)
        wk = jnp.exp(gk - m_new[..., None])[..., None] * kc
        C_new = decay[..., None, None] * C + jnp.einsum('bhsd,bhse->bhde', wk, vc)
        n_new = decay[..., None] * n + jnp.sum(wk, axis=2)
        return (C_new, n_new, m_new), hc

    init = (jnp.zeros((B, H, dqk, dv), jnp.float32), jnp.zeros((B, H, dqk), jnp.float32),
            jnp.zeros((B, H), jnp.float32))
    _, hs = lax.scan(step, init, (chunks(q), chunks(k), chunks(v), chunks(i_pre), chunks(log_f)))
    return jnp.moveaxis(hs, 0, 2).reshape(B, H, S, dv)


def mlstm_branch(h, w_in, b_gate, head_gain, w_out):
    B, S, _ = h.shape
    proj = h @ w_in
    o1 = M_QK_WIDTH
    o2 = o1 + M_QK_WIDTH
    o3 = o2 + M_V_WIDTH
    o4 = o3 + M_V_WIDTH
    o5 = o4 + M_V_WIDTH
    q = proj[..., :o1].reshape(B, S, M_HEADS, M_QK_DIM).transpose(0, 2, 1, 3).astype(jnp.float32) * (M_QK_DIM ** -0.5)
    k = proj[..., o1:o2].reshape(B, S, M_HEADS, M_QK_DIM).transpose(0, 2, 1, 3).astype(jnp.float32)
    v = proj[..., o2:o3].reshape(B, S, M_HEADS, M_V_DIM).transpose(0, 2, 1, 3).astype(jnp.float32)
    o_pre = proj[..., o3:o4]
    z = proj[..., o4:o5]
    gates = proj[..., o5:].astype(jnp.float32) + b_gate.astype(jnp.float32)
    gates = gates.reshape(B, S, 4, M_HEADS).transpose(2, 0, 3, 1)
    i_f, f_f, i_b, f_b = gates[0], gates[1], gates[2], gates[3]
    h_fwd = mlstm_chunkwise(q, k, v, i_f, jax.nn.log_sigmoid(f_f))
    flip = lambda t: jnp.flip(t, axis=2)
    h_bwd = flip(mlstm_chunkwise(flip(q), flip(k), flip(v), flip(i_b), flip(jax.nn.log_sigmoid(f_b))))
    hs = h_fwd + h_bwd
    hs = hs * lax.rsqrt(jnp.mean(hs * hs, axis=-1, keepdims=True) + NORM_EPS) \
        * head_gain.astype(jnp.float32).reshape(M_HEADS, M_V_DIM)[None, :, None, :]
    hs = hs.transpose(0, 2, 1, 3).reshape(B, S, M_V_WIDTH)
    y = (jax.nn.sigmoid(o_pre.astype(jnp.float32)) * hs).astype(h.dtype) * jax.nn.silu(z)
    return y @ w_out


def setup_inputs(seed: int = 0) -> dict:
    key = jax.random.key(seed)
    ks = jax.random.split(key, 14)
    f32 = jnp.float32
    nrm = lambda k_, shape: jax.random.normal(k_, shape, f32)
    f_bias = jnp.linspace(F_BIAS_LO, F_BIAS_HI, M_HEADS, dtype=f32)
    b_gate = jnp.concatenate([
        0.1 * nrm(ks[7], (M_HEADS,)),
        f_bias + 0.1 * nrm(ks[8], (M_HEADS,)),
        0.1 * nrm(ks[9], (M_HEADS,)),
        f_bias + 0.1 * nrm(ks[10], (M_HEADS,)),
    ])
    return {
        'x_prompt': nrm(ks[0], (BATCH, SEQ, D_MODEL)),
        'x_sample': nrm(ks[1], (DEC_BATCH, DEC_SEQ, D_MODEL)),
        'l0_norm_pre': 1.0 + 0.1 * nrm(ks[2], (D_MODEL,)),
        'l0_w_in': nrm(ks[3], (D_MODEL, ATT_IN_COLS)) * D_MODEL ** -0.5,
        'l0_w_out': nrm(ks[4], (ATT_WIDTH, D_MODEL)) * ATT_WIDTH ** -0.5,
        'l0_norm_post': 1.0 + 0.1 * nrm(ks[5], (D_MODEL,)),
        'l1_norm_pre': 1.0 + 0.1 * nrm(ks[6], (D_MODEL,)),
        'l1_w_in': nrm(ks[11], (D_MODEL, M_IN_COLS)) * D_MODEL ** -0.5,
        'l1_b_gate': b_gate,
        'l1_head_norm': 1.0 + 0.1 * nrm(ks[12], (M_V_WIDTH,)),
        'l1_w_out': nrm(ks[13], (M_V_WIDTH, D_MODEL)) * M_V_WIDTH ** -0.5,
        'l1_norm_post': 1.0 + 0.1 * nrm(jax.random.fold_in(key, 99), (D_MODEL,)),
    }


def reference(x_prompt, x_sample, l0_norm_pre, l0_w_in, l0_w_out, l0_norm_post,
              l1_norm_pre, l1_w_in, l1_b_gate, l1_head_norm, l1_w_out, l1_norm_post):
    layer_params = [
        (l0_norm_pre, l0_w_in, l0_w_out, l0_norm_post),
        (l1_norm_pre, l1_w_in, l1_b_gate, l1_head_norm, l1_w_out, l1_norm_post),
    ]

    def trunk(x):
        for i in range(DEPTH):
            p = layer_params[i]
            if i % 2 == 0:
                norm_pre, w_in, w_out, norm_post = p
                out = dilated_attention_branch(rms_norm(x, norm_pre), w_in, w_out)
            else:
                norm_pre, w_in, b_gate, head_gain, w_out, norm_post = p
                out = mlstm_branch(rms_norm(x, norm_pre), w_in, b_gate, head_gain, w_out)
            x = x + rms_norm(out, norm_post)
        return x

    y_prompt = trunk(x_prompt)
    y_sample = trunk(x_sample)
    return (y_prompt, y_sample)
```

```python
import functools

import jax
import jax.numpy as jnp
from jax import lax
from jax.experimental import pallas as pl
from jax.experimental.pallas import tpu as pltpu

F32 = jnp.float32
BF16 = jnp.bfloat16

D_MODEL = 2048
ATT_GROUPS = ((128, 1), (512, 4), (2048, 16))
N_GROUPS = len(ATT_GROUPS)
ATT_HEADS = 16
ATT_HEAD_DIM = 128
ATT_WIDTH = ATT_HEADS * ATT_HEAD_DIM
ATT_IN_COLS = 3 * N_GROUPS * ATT_WIDTH + ATT_WIDTH
ATT_HALF = 64
ROPE_THETA = 500000.0
ROPE_DIMS = ATT_HEAD_DIM // 4
ROPE_HALF = ROPE_DIMS // 2

M_HEADS = 8
M_QK_DIM = 256
M_V_DIM = 512
M_QK_WIDTH = M_HEADS * M_QK_DIM
M_V_WIDTH = M_HEADS * M_V_DIM
M_MAIN_COLS = 2 * M_QK_WIDTH + 3 * M_V_WIDTH
M_CHUNK = 128
GATE_LANES = 128

NORM_EPS = 1e-6
NEG_INF = -1e30

LANES = 128
VMEM_CAP_BYTES = 56 * 1024 * 1024


def _vmem_limit(block_bytes, scratch_bytes=0, temp_bytes=0):
    need = 2 * block_bytes + scratch_bytes + temp_bytes
    return int(min(max(need, 16 * 1024 * 1024), VMEM_CAP_BYTES))


def _nbytes(shape, dtype):
    n = 1
    for s in shape:
        n *= s
    return n * jnp.dtype(dtype).itemsize


def _rmsnorm_kernel(x_ref, g_ref, o_ref):
    x = x_ref[...]
    ms = jnp.mean(x * x, axis=-1, keepdims=True)
    o_ref[...] = (x * lax.rsqrt(ms + NORM_EPS) * g_ref[...]).astype(o_ref.dtype)


def _rmsnorm(x2d, gain, tm=512):
    t, d = x2d.shape
    return pl.pallas_call(
        _rmsnorm_kernel,
        grid=(t // tm,),
        in_specs=[pl.BlockSpec((tm, d), lambda i: (i, 0)), pl.BlockSpec((1, d), lambda i: (0, 0))],
        out_specs=pl.BlockSpec((tm, d), lambda i: (i, 0)),
        out_shape=jax.ShapeDtypeStruct((t, d), BF16),
        compiler_params=pltpu.CompilerParams(
            dimension_semantics=("arbitrary",),
            vmem_limit_bytes=_vmem_limit(_nbytes((tm, d), F32) + _nbytes((tm, d), BF16), 0, 2 * _nbytes((tm, d), F32)),
        ),
        name="rmsnorm",
    )(x2d, gain.reshape(1, d).astype(F32))


def _inproj0_kernel(h_ref, w_ref, c_ref, s1_ref, s2_ref, o_ref, *, tn):
    j = pl.program_id(1)
    acc = jnp.dot(h_ref[...], w_ref[...], preferred_element_type=F32)
    seg = (j * tn) // ATT_WIDTH
    kind = seg % 3
    is_rope = jnp.logical_and(seg < 3 * N_GROUPS, kind < 2)

    @pl.when(is_rope)
    def _():
        scale = jnp.where(kind == 0, ATT_HEAD_DIM ** -0.5, 1.0).astype(F32)
        c = c_ref[...] * scale
        s1 = s1_ref[...] * scale
        s2 = s2_ref[...] * scale
        for t in range(tn // LANES):
            a = acc[:, t * LANES:(t + 1) * LANES]
            r = a * c + pltpu.roll(a, LANES - ROPE_HALF, 1) * s1 + pltpu.roll(a, ROPE_HALF, 1) * s2
            o_ref[:, t * LANES:(t + 1) * LANES] = r.astype(o_ref.dtype)

    @pl.when(jnp.logical_not(is_rope))
    def _():
        o_ref[...] = acc.astype(o_ref.dtype)


def _rope_tables(seq):
    inv = jnp.power(ROPE_THETA, -jnp.arange(ROPE_HALF, dtype=F32) / ROPE_HALF)
    ang = jnp.arange(seq, dtype=F32)[:, None] * inv[None, :]
    cos, sin = jnp.cos(ang), jnp.sin(ang)
    zeros = jnp.zeros((seq, LANES - ROPE_DIMS), F32)
    zh = jnp.zeros((seq, ROPE_HALF), F32)
    c = jnp.concatenate([cos, cos, jnp.ones((seq, LANES - ROPE_DIMS), F32)], axis=1)
    s1 = jnp.concatenate([-sin, zh, zeros], axis=1)
    s2 = jnp.concatenate([zh, sin, zeros], axis=1)
    return c, s1, s2


def _inproj0(h, w, tables, seq, tm=1024, tn=1024):
    t, k = h.shape
    n = w.shape[1]
    tm = min(tm, seq)
    pos_blocks = seq // tm
    tab_spec = pl.BlockSpec((tm, LANES), lambda i, j: (i % pos_blocks, 0))
    blocks = _nbytes((tm, k), BF16) + _nbytes((k, tn), BF16) + _nbytes((tm, tn), BF16) + 3 * _nbytes((tm, LANES), F32)
    return pl.pallas_call(
        functools.partial(_inproj0_kernel, tn=tn),
        grid=(t // tm, n // tn),
        in_specs=[
            pl.BlockSpec((tm, k), lambda i, j: (i, 0)),
            pl.BlockSpec((k, tn), lambda i, j: (0, j)),
            tab_spec, tab_spec, tab_spec,
        ],
        out_specs=pl.BlockSpec((tm, tn), lambda i, j: (i, j)),
        out_shape=jax.ShapeDtypeStruct((t, n), BF16),
        compiler_params=pltpu.CompilerParams(
            dimension_semantics=("arbitrary", "arbitrary"),
            vmem_limit_bytes=_vmem_limit(blocks, 0, 2 * _nbytes((tm, tn), F32)),
        ),
        name="inproj0",
    )(h, w, *tables)


def _attn_kernel(*refs, tq, ls, has_prev, is_last):
    it = iter(refs)
    q_ref, kp_ref, kc_ref, kn_ref, vp_ref, vc_ref, vn_ref = (next(it) for _ in range(7))
    op_ref = lp_ref = z_ref = l_ref = None
    if has_prev:
        op_ref, lp_ref = next(it), next(it)
    if is_last:
        z_ref = next(it)
    o_ref = next(it)
    if not is_last:
        l_ref = next(it)
    kx, vx = next(it), next(it)

    hb = ATT_HALF
    sub = 2 * hb
    i = pl.program_id(2)
    kx[0:hb, :] = kp_ref[0]
    kx[hb:hb + tq, :] = kc_ref[0]
    kx[hb + tq:, :] = kn_ref[0]
    vx[0:hb, :] = vp_ref[0]
    vx[hb:hb + tq, :] = vc_ref[0]
    vx[hb + tq:, :] = vn_ref[0]

    row = lax.broadcasted_iota(jnp.int32, (sub, 2 * sub), 0)
    col = lax.broadcasted_iota(jnp.int32, (sub, 2 * sub), 1)
    band = jnp.abs(col - hb - row) <= hb
    lane = lax.broadcasted_iota(jnp.int32, (sub, LANES), 1)

    for a in range(tq // sub):
        r0 = a * sub
        kidx = i * tq + r0 + col - hb
        valid = band & (kidx >= 0) & (kidx < ls)
        lse_tile = jnp.zeros((sub, LANES), F32)
        for h in range(ATT_HEADS):
            hs = slice(h * ATT_HEAD_DIM, (h + 1) * ATT_HEAD_DIM)
            qh = q_ref[0, r0:r0 + sub, hs]
            kh = kx[r0:r0 + 2 * sub, hs]
            vh = vx[r0:r0 + 2 * sub, hs]
            s = lax.dot_general(qh, kh, (((1,), (1,)), ((), ())), preferred_element_type=F32)
            s = jnp.where(valid, s, NEG_INF)
            m = jnp.max(s, axis=-1, keepdims=True)
            p = jnp.exp(s - m)
            den = jnp.sum(p, axis=-1, keepdims=True)
            o = jnp.dot(p.astype(BF16), vh, preferred_element_type=F32) / den
            lse = m + jnp.log(den)
            if has_prev:
                lp = lp_ref[0, r0:r0 + sub, h:h + 1]
                op = op_ref[0, r0:r0 + sub, hs].astype(F32)
                mx = jnp.maximum(lp, lse)
                tot = mx + jnp.log(jnp.exp(lp - mx) + jnp.exp(lse - mx))
                o = op * jnp.exp(lp - tot) + o * jnp.exp(lse - tot)
                lse = tot
            if is_last:
                z = z_ref[0, r0:r0 + sub, hs].astype(F32)
                o = o * (z * jax.nn.sigmoid(z))
            else:
                lse_tile = jnp.where(lane == h, lse, lse_tile)
            o_ref[0, r0:r0 + sub, hs] = o.astype(o_ref.dtype)
        if not is_last:
            l_ref[0, r0:r0 + sub, :] = lse_tile


def _attention_group(proj, g, prev, bsz, seq, is_last):
    _, dil = ATT_GROUPS[g]
    w = ATT_WIDTH
    ls = seq // dil
    tq = min(256, ls)
    hb = ATT_HALF
    ncb = ATT_IN_COLS // w
    nhb = ls // hb
    per = tq // hb
    pv = proj.reshape(bsz, ls, dil * ATT_IN_COLS)

    def col_spec(cb):
        return pl.BlockSpec((1, tq, w), lambda b, r, i: (b, i, r * ncb + cb))

    def halo_specs(cb):
        prev_s = pl.BlockSpec((1, hb, w), lambda b, r, i: (b, jnp.maximum(i * per - 1, 0), r * ncb + cb))
        next_s = pl.BlockSpec((1, hb, w), lambda b, r, i: (b, jnp.minimum((i + 1) * per, nhb - 1), r * ncb + cb))
        return prev_s, next_s

    kp_s, kn_s = halo_specs(3 * g + 1)
    vp_s, vn_s = halo_specs(3 * g + 2)
    in_specs = [col_spec(3 * g), kp_s, col_spec(3 * g + 1), kn_s, vp_s, col_spec(3 * g + 2), vn_s]
    args = [pv] * 7
    o_spec = pl.BlockSpec((1, tq, w), lambda b, r, i: (b, i, r))
    l_spec = pl.BlockSpec((1, tq, LANES), lambda b, r, i: (b, i, r))
    blocks = 4 * _nbytes((tq, w), BF16) + 4 * _nbytes((hb, w), BF16)
    if prev is not None:
        o_prev, l_prev = prev
        in_specs += [o_spec, l_spec]
        args += [o_prev.reshape(bsz, ls, dil * w), l_prev.reshape(bsz, ls, dil * LANES)]
        blocks += _nbytes((tq, w), BF16) + _nbytes((tq, LANES), F32)
    if is_last:
        in_specs.append(col_spec(3 * N_GROUPS))
        args.append(pv)
        blocks += _nbytes((tq, w), BF16)
        out_specs = o_spec
        out_shape = jax.ShapeDtypeStruct((bsz, ls, dil * w), BF16)
    else:
        out_specs = (o_spec, l_spec)
        out_shape = (jax.ShapeDtypeStruct((bsz, ls, dil * w), BF16),
                     jax.ShapeDtypeStruct((bsz, ls, dil * LANES), F32))
        blocks += _nbytes((tq, LANES), F32)
    scratch = [pltpu.VMEM((tq + 2 * hb, w), BF16), pltpu.VMEM((tq + 2 * hb, w), BF16)]
    out = pl.pallas_call(
        functools.partial(_attn_kernel, tq=tq, ls=ls, has_prev=prev is not None, is_last=is_last),
        grid=(bsz, dil, ls // tq),
        in_specs=in_specs,
        out_specs=out_specs,
        out_shape=out_shape,
        scratch_shapes=scratch,
        compiler_params=pltpu.CompilerParams(
            dimension_semantics=("arbitrary", "arbitrary", "arbitrary"),
            vmem_limit_bytes=_vmem_limit(blocks, 2 * _nbytes((tq + 2 * hb, w), BF16), 8 * 1024 * 1024),
        ),
        name=f"attn_g{g}",
    )(*args)
    if is_last:
        return out.reshape(bsz, seq, w)
    o, l = out
    return o.reshape(bsz, seq, w), l.reshape(bsz, seq, LANES)


def _outproj_kernel(*refs, gated, emit_next):
    it = iter(refs)
    if gated:
        hn_ref, og_ref, z_ref = next(it), next(it), next(it)
    else:
        y_ref = next(it)
    w_ref, x_ref, gpost_ref = next(it), next(it), next(it)
    gnext_ref = next(it) if emit_next else None
    o_ref = next(it)
    hnext_ref = next(it) if emit_next else None

    if gated:
        z = z_ref[...].astype(F32)
        y = (jax.nn.sigmoid(og_ref[...].astype(F32)) * hn_ref[...].astype(F32)) * (z * jax.nn.sigmoid(z))
        y = y.astype(BF16)
    else:
        y = y_ref[...]
    out = jnp.dot(y, w_ref[...], preferred_element_type=F32)
    ms = jnp.mean(out * out, axis=-1, keepdims=True)
    x1 = x_ref[...] + out * lax.rsqrt(ms + NORM_EPS) * gpost_ref[...]
    o_ref[...] = x1
    if emit_next:
        ms1 = jnp.mean(x1 * x1, axis=-1, keepdims=True)
        hnext_ref[...] = (x1 * lax.rsqrt(ms1 + NORM_EPS) * gnext_ref[...]).astype(hnext_ref.dtype)


def _outproj(lhs, w, x2d, g_post, g_next, tm):
    gated = isinstance(lhs, tuple)
    emit_next = g_next is not None
    t, d = x2d.shape
    k = w.shape[0]
    row = lambda i: (i, 0)
    if gated:
        hn, proj1 = lhs
        ob = (2 * M_QK_WIDTH + M_V_WIDTH) // k
        in_specs = [pl.BlockSpec((tm, k), row), pl.BlockSpec((tm, k), lambda i: (i, ob)),
                    pl.BlockSpec((tm, k), lambda i: (i, ob + 1))]
        args = [hn, proj1, proj1]
        blocks = 3 * _nbytes((tm, k), BF16)
    else:
        in_specs = [pl.BlockSpec((tm, k), row)]
        args = [lhs]
        blocks = _nbytes((tm, k), BF16)
    in_specs += [pl.BlockSpec((k, d), lambda i: (0, 0), pipeline_mode=pl.Buffered(1)),
                 pl.BlockSpec((tm, d), row), pl.BlockSpec((1, d), lambda i: (0, 0))]
    args += [w, x2d, g_post.reshape(1, d).astype(F32)]
    blocks += 2 * _nbytes((tm, d), F32)
    out_specs = pl.BlockSpec((tm, d), row)
    out_shape = jax.ShapeDtypeStruct((t, d), F32)
    if emit_next:
        in_specs.append(pl.BlockSpec((1, d), lambda i: (0, 0)))
        args.append(g_next.reshape(1, d).astype(F32))
        out_specs = (out_specs, pl.BlockSpec((tm, d), row))
        out_shape = (out_shape, jax.ShapeDtypeStruct((t, d), BF16))
        blocks += _nbytes((tm, d), BF16)
    return pl.pallas_call(
        functools.partial(_outproj_kernel, gated=gated, emit_next=emit_next),
        grid=(t // tm,),
        in_specs=in_specs,
        out_specs=out_specs,
        out_shape=out_shape,
        compiler_params=pltpu.CompilerParams(
            dimension_semantics=("arbitrary",),
            vmem_limit_bytes=_vmem_limit(blocks, _nbytes((k, d), BF16), 4 * _nbytes((tm, max(k, d)), F32)),
        ),
        name="outproj1" if gated else "outproj0",
    )(*args)


def _inproj1_kernel(h_ref, w_ref, wg_ref, o_ref, g_ref):
    o_ref[...] = jnp.dot(h_ref[...], w_ref[...], preferred_element_type=F32).astype(o_ref.dtype)

    @pl.when(pl.program_id(1) == 0)
    def _():
        g_ref[...] = jnp.dot(h_ref[...], wg_ref[...], preferred_element_type=F32)


def _inproj1(h, w, wg, tm=1024, tn=1024):
    t, k = h.shape
    n = w.shape[1]
    tm = min(tm, t)
    blocks = (_nbytes((tm, k), BF16) + _nbytes((k, tn), BF16) + _nbytes((k, GATE_LANES), BF16)
              + _nbytes((tm, tn), BF16) + _nbytes((tm, GATE_LANES), F32))
    return pl.pallas_call(
        _inproj1_kernel,
        grid=(t // tm, n // tn),
        in_specs=[
            pl.BlockSpec((tm, k), lambda i, j: (i, 0)),
            pl.BlockSpec((k, tn), lambda i, j: (0, j)),
            pl.BlockSpec((k, GATE_LANES), lambda i, j: (0, 0)),
        ],
        out_specs=(pl.BlockSpec((tm, tn), lambda i, j: (i, j)), pl.BlockSpec((tm, GATE_LANES), lambda i, j: (i, 0))),
        out_shape=(jax.ShapeDtypeStruct((t, n), BF16), jax.ShapeDtypeStruct((t, GATE_LANES), F32)),
        compiler_params=pltpu.CompilerParams(
            dimension_semantics=("arbitrary", "arbitrary"),
            vmem_limit_bytes=_vmem_limit(blocks, 0, 2 * _nbytes((tm, tn), F32)),
        ),
        name="inproj1",
    )(h, w, wg)


def _scan_rows(x, rows, op, fill, reverse):
    n = x.shape[0]
    k = 1
    while k < n:
        if reverse:
            shifted = jnp.where(rows < n - k, pltpu.roll(x, n - k, 0), fill)
        else:
            shifted = jnp.where(rows >= k, pltpu.roll(x, k, 0), fill)
        x = op(x, shifted)
        k *= 2
    return x


def _gate_prep_kernel(g_ref, b_ref, o_ref):
    g = g_ref[0] + b_ref[...]
    rows = lax.broadcasted_iota(jnp.int32, g.shape, 0)
    kind = lax.broadcasted_iota(jnp.int32, g.shape, 1) % 8
    lf = jnp.minimum(g, 0.0) - jnp.log1p(jnp.exp(-jnp.abs(g)))
    csum = _scan_rows(lf, rows, jnp.add, 0.0, reverse=False)
    rsum = _scan_rows(lf, rows, jnp.add, 0.0, reverse=True)
    b_at_i = pltpu.roll(jnp.where(kind == 1, csum, rsum), LANES - 1, 1)
    grow = g - b_at_i
    cm_f = _scan_rows(grow, rows, jnp.maximum, NEG_INF, reverse=False)
    cm_b = _scan_rows(grow, rows, jnp.maximum, NEG_INF, reverse=True)
    y = jnp.where(kind == 0, b_at_i, 0.0)
    y = jnp.where(kind == 1, pltpu.roll(grow, 1, 1), y)
    y = jnp.where(kind == 2, pltpu.roll(cm_f, 2, 1), y)
    y = jnp.where(kind == 3, rsum, y)
    y = jnp.where(kind == 4, pltpu.roll(grow, 2, 1), y)
    y = jnp.where(kind == 5, pltpu.roll(cm_b, 3, 1), y)
    yt = y.T
    o_ref[0, :, 0] = yt[:M_HEADS * 8].reshape(M_HEADS, 8, g.shape[0])


def _gate_prep(gates, bias, bsz, seq):
    nc = seq // M_CHUNK
    return pl.pallas_call(
        _gate_prep_kernel,
        grid=(bsz, nc),
        in_specs=[pl.BlockSpec((1, M_CHUNK, GATE_LANES), lambda b, c: (b, c, 0)),
                  pl.BlockSpec((1, GATE_LANES), lambda b, c: (0, 0))],
        out_specs=pl.BlockSpec((1, M_HEADS, 1, 8, M_CHUNK), lambda b, c: (b, 0, c, 0, 0)),
        out_shape=jax.ShapeDtypeStruct((bsz, M_HEADS, nc, 8, M_CHUNK), F32),
        compiler_params=pltpu.CompilerParams(dimension_semantics=("arbitrary", "arbitrary")),
        name="gate_prep",
    )(gates.reshape(bsz, seq, GATE_LANES), bias)


def _mlstm_kernel(q_ref, k_ref, v_ref, r_ref, gain_ref, o_ref, hs_scr, cf_scr, cb_scr, *, nc):
    L = M_CHUNK
    rows = lax.broadcasted_iota(jnp.int32, (L, L), 0)
    cols = lax.broadcasted_iota(jnp.int32, (L, L), 1)
    eye = rows == cols
    causal = cols <= rows
    anti = cols >= rows
    gain = gain_ref[0]

    def to_col(r):
        return jnp.sum(jnp.where(eye, r, 0.0), axis=1, keepdims=True)

    def chunk(c, c_scr, n, m, fwd):
        st = pl.multiple_of(c * L, L)
        qc = q_ref[0, pl.ds(st, L), :] * (M_QK_DIM ** -0.5)
        kc = k_ref[0, pl.ds(st, L), :]
        vc = v_ref[0, pl.ds(st, L), :]
        r = r_ref[0, 0, pl.ds(c, 1)].reshape(8, L)
        base = 0 if fwd else 3
        grow = r[base + 1:base + 2]
        bcol = to_col(r[base:base + 1])
        gcol = to_col(grow)
        mm = jnp.maximum(to_col(r[base + 2:base + 3]), m)
        e = L - 1 if fwd else 0
        mm_l = mm[e:e + 1]
        s = lax.dot_general(qc, kc, (((1,), (1,)), ((), ())), preferred_element_type=F32)
        a = jnp.exp(jnp.where(causal if fwd else anti, grow - mm, NEG_INF)) * s
        gint = jnp.exp(m - mm)
        cmat = c_scr[...]
        num = (jnp.dot(a.astype(BF16), vc, preferred_element_type=F32)
               + gint * jnp.dot(qc, cmat.astype(BF16), preferred_element_type=F32))
        qn = jnp.sum(qc.astype(F32) * n, axis=1, keepdims=True)
        den = jnp.sum(a, axis=1, keepdims=True) + gint * qn
        hc = num / jnp.maximum(jnp.abs(den), jnp.exp(-(bcol + mm)))
        decay = jnp.exp(m - mm_l)
        wk = jnp.exp(gcol - mm_l) * kc.astype(F32)
        c_scr[...] = decay * cmat + lax.dot_general(wk.astype(BF16), vc, (((0,), (0,)), ((), ())),
                                                    preferred_element_type=F32)
        n_new = decay * n + jnp.sum(wk, axis=0, keepdims=True)
        m_new = bcol[e:e + 1] + mm_l
        return hc, n_new, m_new

    def finish(c, hsum):
        st = pl.multiple_of(c * L, L)
        ms = jnp.mean(hsum * hsum, axis=-1, keepdims=True)
        o_ref[0, pl.ds(st, L), :] = (hsum * lax.rsqrt(ms + NORM_EPS) * gain).astype(o_ref.dtype)

    def step(j, carry, second_half):
        nf, mf, nb, mb = carry
        cf, cb = j, nc - 1 - j
        hf, nf, mf = chunk(cf, cf_scr, nf, mf, True)
        hb, nb, mb = chunk(cb, cb_scr, nb, mb, False)
        sf = pl.multiple_of(cf * L, L)
        sb = pl.multiple_of(cb * L, L)
        if second_half:
            finish(cf, hs_scr[pl.ds(sf, L), :] + hf)
            finish(cb, hs_scr[pl.ds(sb, L), :] + hb)
        else:
            hs_scr[pl.ds(sf, L), :] = hf
            hs_scr[pl.ds(sb, L), :] = hb
        return nf, mf, nb, mb

    cf_scr[...] = jnp.zeros_like(cf_scr)
    cb_scr[...] = jnp.zeros_like(cb_scr)
    n0 = jnp.zeros((1, M_QK_DIM), F32)
    m0 = jnp.zeros((1, 1), F32)
    carry = lax.fori_loop(0, nc // 2, functools.partial(step, second_half=False), (n0, m0, n0, m0))
    lax.fori_loop(nc // 2, nc, functools.partial(step, second_half=True), carry)


def _mlstm(proj1, prep, gain, bsz, seq):
    nc = seq // M_CHUNK
    assert nc % 2 == 0
    p3 = proj1.reshape(bsz, seq, M_MAIN_COLS)
    kb = M_QK_WIDTH // M_QK_DIM
    vb = 2 * M_QK_WIDTH // M_V_DIM
    blocks = (2 * _nbytes((seq, M_QK_DIM), BF16) + 2 * _nbytes((seq, M_V_DIM), BF16)
              + _nbytes((nc, 8, M_CHUNK), F32) + _nbytes((1, M_V_DIM), F32))
    scratch_bytes = _nbytes((seq, M_V_DIM), F32) + 2 * _nbytes((M_QK_DIM, M_V_DIM), F32)
    return pl.pallas_call(
        functools.partial(_mlstm_kernel, nc=nc),
        grid=(bsz, M_HEADS),
        in_specs=[
            pl.BlockSpec((1, seq, M_QK_DIM), lambda b, h: (b, 0, h)),
            pl.BlockSpec((1, seq, M_QK_DIM), lambda b, h: (b, 0, kb + h)),
            pl.BlockSpec((1, seq, M_V_DIM), lambda b, h: (b, 0, vb + h)),
            pl.BlockSpec((1, 1, nc, 8, M_CHUNK), lambda b, h: (b, h, 0, 0, 0)),
            pl.BlockSpec((1, 1, M_V_DIM), lambda b, h: (h, 0, 0)),
        ],
        out_specs=pl.BlockSpec((1, seq, M_V_DIM), lambda b, h: (b, 0, h)),
        out_shape=jax.ShapeDtypeStruct((bsz, seq, M_V_WIDTH), BF16),
        scratch_shapes=[pltpu.VMEM((seq, M_V_DIM), F32), pltpu.VMEM((M_QK_DIM, M_V_DIM), F32),
                        pltpu.VMEM((M_QK_DIM, M_V_DIM), F32)],
        compiler_params=pltpu.CompilerParams(
            dimension_semantics=("arbitrary", "arbitrary"),
            vmem_limit_bytes=_vmem_limit(blocks, scratch_bytes, 8 * 1024 * 1024),
        ),
        name="mlstm",
    )(p3, p3, p3, prep, gain.reshape(M_HEADS, 1, M_V_DIM).astype(F32))


def _gate_weights(w_in1, b_gate):
    wg = w_in1[:, M_MAIN_COLS:].reshape(D_MODEL, 4, M_HEADS).transpose(0, 2, 1)
    wg = jnp.pad(wg, ((0, 0), (0, 0), (0, 4))).reshape(D_MODEL, M_HEADS * 8)
    wg = jnp.pad(wg, ((0, 0), (0, GATE_LANES - M_HEADS * 8)))
    bg = jnp.pad(b_gate.astype(F32).reshape(4, M_HEADS).T, ((0, 0), (0, 4))).reshape(1, M_HEADS * 8)
    bg = jnp.pad(bg, ((0, 0), (0, GATE_LANES - M_HEADS * 8)))
    return wg.astype(BF16), bg


def _trunk(x, p):
    bsz, seq, d = x.shape
    t = bsz * seq
    x2d = x.reshape(t, d)
    h0 = _rmsnorm(x2d, p["l0_norm_pre"])
    proj0 = _inproj0(h0, p["l0_w_in"], p["rope"], seq)
    prev = None
    for g in range(N_GROUPS):
        prev = _attention_group(proj0, g, prev, bsz, seq, is_last=(g == N_GROUPS - 1))
    y0 = prev.reshape(t, ATT_WIDTH)
    x1, h1 = _outproj(y0, p["l0_w_out"], x2d, p["l0_norm_post"], p["l1_norm_pre"], tm=512)
    proj1, gates = _inproj1(h1, p["l1_w_main"], p["l1_w_gate"])
    prep = _gate_prep(gates, p["l1_b_gate"], bsz, seq)
    hn = _mlstm(proj1, prep, p["l1_head_norm"], bsz, seq)
    y = _outproj((hn.reshape(t, M_V_WIDTH), proj1), p["l1_w_out"], x1, p["l1_norm_post"], None, tm=256)
    return y.reshape(bsz, seq, d)


def kernel(x_prompt, x_sample, l0_norm_pre, l0_w_in, l0_w_out, l0_norm_post,
           l1_norm_pre, l1_w_in, l1_b_gate, l1_head_norm, l1_w_out, l1_norm_post):
    assert x_prompt.shape[1] == x_sample.shape[1]
    w_gate, b_gate = _gate_weights(l1_w_in, l1_b_gate)
    p = {
        "l0_norm_pre": l0_norm_pre, "l0_w_in": l0_w_in.astype(BF16), "l0_w_out": l0_w_out.astype(BF16),
        "l0_norm_post": l0_norm_post, "l1_norm_pre": l1_norm_pre,
        "l1_w_main": l1_w_in[:, :M_MAIN_COLS].astype(BF16), "l1_w_gate": w_gate, "l1_b_gate": b_gate,
        "l1_head_norm": l1_head_norm, "l1_w_out": l1_w_out.astype(BF16), "l1_norm_post": l1_norm_post,
        "rope": _rope_tables(x_prompt.shape[1]),
    }
    return (_trunk(x_prompt, p), _trunk(x_sample, p))
```

```python
import functools

import jax
import jax.numpy as jnp
from jax import lax
from jax.experimental import pallas as pl
from jax.experimental.pallas import tpu as pltpu

F32 = jnp.float32
BF16 = jnp.bfloat16

D_MODEL = 2048
ATT_GROUPS = ((128, 1), (512, 4), (2048, 16))
N_GROUPS = len(ATT_GROUPS)
ATT_HEADS = 16
ATT_HEAD_DIM = 128
ATT_WIDTH = ATT_HEADS * ATT_HEAD_DIM
ATT_IN_COLS = 3 * N_GROUPS * ATT_WIDTH + ATT_WIDTH
ATT_HALF = 64
ROPE_THETA = 500000.0
ROPE_DIMS = ATT_HEAD_DIM // 4
ROPE_HALF = ROPE_DIMS // 2

M_HEADS = 8
M_QK_DIM = 256
M_V_DIM = 512
M_QK_WIDTH = M_HEADS * M_QK_DIM
M_V_WIDTH = M_HEADS * M_V_DIM
M_MAIN_COLS = 2 * M_QK_WIDTH + 3 * M_V_WIDTH
M_CHUNK = 128
GATE_LANES = 128

NORM_EPS = 1e-6
NEG_INF = -1e30

LANES = 128
VMEM_CAP_BYTES = 56 * 1024 * 1024


def _vmem_limit(block_bytes, scratch_bytes=0, temp_bytes=0):
    need = 2 * block_bytes + scratch_bytes + temp_bytes
    return int(min(max(need, 16 * 1024 * 1024), VMEM_CAP_BYTES))


def _nbytes(shape, dtype):
    n = 1
    for s in shape:
        n *= s
    return n * jnp.dtype(dtype).itemsize


PERM_BLOCK = 1024


def _rmsnorm_orders_kernel(x_ref, g_ref, o_ref, slab_scr):
    x = x_ref[...]
    inv = lax.rsqrt(jnp.mean(x * x, axis=-1, keepdims=True) + NORM_EPS)
    nslab = x_ref.shape[1] // LANES
    for c in range(nslab):
        ls = slice(c * LANES, (c + 1) * LANES)
        y = x_ref[:, ls] * inv * g_ref[:, ls]
        slab_scr[c] = y
        for g, (_, dil) in enumerate(ATT_GROUPS):
            if dil == 1:
                o_ref[g, :, ls] = y.astype(o_ref.dtype)
    for g, (_, dil) in enumerate(ATT_GROUPS):
        if dil == 1:
            continue
        n = PERM_BLOCK // dil
        for r in range(dil):
            for c in range(nslab):
                o_ref[g, r * n:(r + 1) * n, c * LANES:(c + 1) * LANES] = (
                    slab_scr[c, pl.ds(r, n, stride=dil), :].astype(o_ref.dtype))


def _rmsnorm_orders(x2d, gain):
    t, d = x2d.shape
    tm = PERM_BLOCK
    return pl.pallas_call(
        _rmsnorm_orders_kernel,
        grid=(t // tm,),
        in_specs=[pl.BlockSpec((tm, d), lambda i: (i, 0)), pl.BlockSpec((1, d), lambda i: (0, 0))],
        out_specs=pl.BlockSpec((N_GROUPS, tm, d), lambda i: (0, i, 0)),
        out_shape=jax.ShapeDtypeStruct((N_GROUPS, t, d), BF16),
        scratch_shapes=[pltpu.VMEM((d // LANES, tm, LANES), F32)],
        compiler_params=pltpu.CompilerParams(
            dimension_semantics=("arbitrary",),
            vmem_limit_bytes=_vmem_limit(_nbytes((tm, d), F32) + _nbytes((N_GROUPS, tm, d), BF16),
                                         _nbytes((tm, d), F32), _nbytes((tm, d), F32)),
        ),
        name="rmsnorm_orders",
    )(x2d, gain.reshape(1, d).astype(F32))


def _order_index(seq, dil):
    n = PERM_BLOCK // dil
    return jnp.arange(seq, dtype=jnp.int32).reshape(seq // PERM_BLOCK, n, dil).transpose(0, 2, 1).reshape(seq)


INPROJ0_TN = 1024
INPROJ0_CHUNK = 256
_GROUP_TILES = 3 * ATT_WIDTH // INPROJ0_TN
_Z_TILES = ATT_WIDTH // INPROJ0_TN


def _inproj0_col_tile(jj):
    return jnp.where(jj < _GROUP_TILES, jj,
                     jnp.where(jj < _GROUP_TILES + _Z_TILES, jj + (N_GROUPS - 1) * _GROUP_TILES, jj - _Z_TILES))


def _inproj0_order(jj):
    return jnp.where(jj < _GROUP_TILES + _Z_TILES, 0, (jj - _Z_TILES) // _GROUP_TILES)


def _inproj0_kernel(h_ref, w_ref, c_ref, s1_ref, s2_ref, o_ref):
    col = _inproj0_col_tile(pl.program_id(1))
    seg = (col * INPROJ0_TN) // ATT_WIDTH
    kind = seg % 3
    is_rope = jnp.logical_and(seg < 3 * N_GROUPS, kind < 2)
    scale = jnp.where(kind == 0, ATT_HEAD_DIM ** -0.5, 1.0).astype(F32)
    c = jnp.where(is_rope, c_ref[...] * scale, 1.0)
    s1 = jnp.where(is_rope, s1_ref[...] * scale, 0.0)
    s2 = jnp.where(is_rope, s2_ref[...] * scale, 0.0)
    h = h_ref[...]
    for cc in range(INPROJ0_TN // INPROJ0_CHUNK):
        acc = jnp.dot(h, w_ref[:, cc * INPROJ0_CHUNK:(cc + 1) * INPROJ0_CHUNK], preferred_element_type=F32)
        for t in range(INPROJ0_CHUNK // LANES):
            a = acc[:, t * LANES:(t + 1) * LANES]
            r = a * c + pltpu.roll(a, LANES - ROPE_HALF, 1) * s1 + pltpu.roll(a, ROPE_HALF, 1) * s2
            lo = cc * INPROJ0_CHUNK + t * LANES
            o_ref[:, lo:lo + LANES] = r.astype(o_ref.dtype)


def _rope_tables(seq):
    inv = jnp.power(ROPE_THETA, -jnp.arange(ROPE_HALF, dtype=F32) / ROPE_HALF)
    ang = jnp.arange(seq, dtype=F32)[:, None] * inv[None, :]
    cos, sin = jnp.cos(ang), jnp.sin(ang)
    zeros = jnp.zeros((seq, LANES - ROPE_DIMS), F32)
    zh = jnp.zeros((seq, ROPE_HALF), F32)
    c = jnp.concatenate([cos, cos, jnp.ones((seq, LANES - ROPE_DIMS), F32)], axis=1)
    s1 = jnp.concatenate([-sin, zh, zeros], axis=1)
    s2 = jnp.concatenate([zh, sin, zeros], axis=1)
    orders = [_order_index(seq, dil) for _, dil in ATT_GROUPS]
    return tuple(jnp.stack([tab[idx] for idx in orders]) for tab in (c, s1, s2))


def _inproj0(h_orders, w, tables, seq):
    _, t, k = h_orders.shape
    n = w.shape[1]
    tm, tn = PERM_BLOCK, INPROJ0_TN
    pos_blocks = seq // tm
    tab_spec = pl.BlockSpec((None, tm, LANES), lambda i, j: (_inproj0_order(j), i % pos_blocks, 0))
    blocks = _nbytes((tm, k), BF16) + _nbytes((k, tn), BF16) + _nbytes((tm, tn), BF16) + 3 * _nbytes((tm, LANES), F32)
    return pl.pallas_call(
        _inproj0_kernel,
        grid=(t // tm, n // tn),
        in_specs=[
            pl.BlockSpec((None, tm, k), lambda i, j: (_inproj0_order(j), i, 0)),
            pl.BlockSpec((k, tn), lambda i, j: (0, _inproj0_col_tile(j))),
            tab_spec, tab_spec, tab_spec,
        ],
        out_specs=pl.BlockSpec((tm, tn), lambda i, j: (i, _inproj0_col_tile(j))),
        out_shape=jax.ShapeDtypeStruct((t, n), BF16),
        compiler_params=pltpu.CompilerParams(
            dimension_semantics=("arbitrary", "arbitrary"),
            vmem_limit_bytes=_vmem_limit(blocks, 0, 4 * _nbytes((tm, INPROJ0_CHUNK), F32)),
        ),
        name="inproj0",
    )(h_orders, w, *tables)


ATT_TQ = 256


def _attn_kernel(q_ref, kp_ref, kc_ref, kn_ref, vp_ref, vc_ref, vn_ref, o_ref, l_ref, qx, kx, vx, ox, lx, *, tq, ls):
    hb = ATT_HALF
    sub = 2 * hb
    w = qx.shape[1]
    i = pl.program_id(2)
    qx[...] = q_ref[0].reshape(tq, w)
    kx[0:hb, :] = kp_ref[0].reshape(hb, w)
    kx[hb:hb + tq, :] = kc_ref[0].reshape(tq, w)
    kx[hb + tq:, :] = kn_ref[0].reshape(hb, w)
    vx[0:hb, :] = vp_ref[0].reshape(hb, w)
    vx[hb:hb + tq, :] = vc_ref[0].reshape(tq, w)
    vx[hb + tq:, :] = vn_ref[0].reshape(hb, w)

    row = lax.broadcasted_iota(jnp.int32, (sub, 2 * sub), 0)
    col = lax.broadcasted_iota(jnp.int32, (sub, 2 * sub), 1)
    band = jnp.abs(col - hb - row) <= hb
    lane = lax.broadcasted_iota(jnp.int32, (sub, LANES), 1)

    for a in range(tq // sub):
        r0 = a * sub
        kidx = i * tq + r0 + col - hb
        valid = band & (kidx >= 0) & (kidx < ls)
        lse_tile = jnp.zeros((sub, LANES), F32)
        for h in range(ATT_HEADS):
            hs = slice(h * ATT_HEAD_DIM, (h + 1) * ATT_HEAD_DIM)
            qh = qx[r0:r0 + sub, hs]
            kh = kx[r0:r0 + 2 * sub, hs]
            vh = vx[r0:r0 + 2 * sub, hs]
            s = lax.dot_general(qh, kh, (((1,), (1,)), ((), ())), preferred_element_type=F32)
            s = jnp.where(valid, s, NEG_INF)
            m = jnp.max(s, axis=-1, keepdims=True)
            p = jnp.exp(s - m)
            den = jnp.sum(p, axis=-1, keepdims=True)
            o = jnp.dot(p.astype(BF16), vh, preferred_element_type=F32) / den
            lse_tile = jnp.where(lane == h, m + jnp.log(den), lse_tile)
            ox[r0:r0 + sub, hs] = o.astype(ox.dtype)
        lx[r0:r0 + sub, :] = lse_tile
    o_ref[0] = ox[...].reshape(o_ref.shape[1:])
    l_ref[0] = lx[...].reshape(l_ref.shape[1:])


def _attention_group(proj, g, bsz, seq):
    _, dil = ATT_GROUPS[g]
    w = ATT_WIDTH
    hb = ATT_HALF
    ls = seq // dil
    if dil == 1:
        nb, n = seq // ATT_TQ, ATT_TQ
    else:
        nb, n = seq // PERM_BLOCK, PERM_BLOCK // dil
    whole_class = n < ATT_TQ
    tq = ls if whole_class else ATT_TQ
    assert tq % (2 * hb) == 0 and n % hb == 0
    pv = proj.reshape(bsz, nb, dil, n, ATT_IN_COLS)
    ncb = 3 * g

    if whole_class:
        grid = (bsz, dil, 1)
        main = lambda cb: pl.BlockSpec((1, nb, None, n, w), lambda b, r, i: (b, 0, r, 0, cb))
        prev_h = next_h = lambda cb: pl.BlockSpec((1, 1, None, hb, w), lambda b, r, i: (b, 0, r, 0, cb))
        out_block = lambda width: pl.BlockSpec((1, nb, None, n, width), lambda b, r, i: (b, 0, r, 0, 0))
    else:
        grid = (bsz, dil, nb)
        main = lambda cb: pl.BlockSpec((1, 1, None, n, w), lambda b, r, i: (b, i, r, 0, cb))
        prev_h = lambda cb: pl.BlockSpec((1, 1, None, hb, w),
                                         lambda b, r, i: (b, jnp.maximum(i - 1, 0), r, n // hb - 1, cb))
        next_h = lambda cb: pl.BlockSpec((1, 1, None, hb, w),
                                         lambda b, r, i: (b, jnp.minimum(i + 1, nb - 1), r, 0, cb))
        out_block = lambda width: pl.BlockSpec((1, 1, None, n, width), lambda b, r, i: (b, i, r, 0, 0))

    in_specs = [main(ncb), prev_h(ncb + 1), main(ncb + 1), next_h(ncb + 1),
                prev_h(ncb + 2), main(ncb + 2), next_h(ncb + 2)]
    blocks = 4 * _nbytes((tq, w), BF16) + 4 * _nbytes((hb, w), BF16) + _nbytes((tq, LANES), F32)
    scratch = [pltpu.VMEM((tq, w), BF16), pltpu.VMEM((tq + 2 * hb, w), BF16), pltpu.VMEM((tq + 2 * hb, w), BF16),
               pltpu.VMEM((tq, w), BF16), pltpu.VMEM((tq, LANES), F32)]
    scratch_bytes = 4 * _nbytes((tq + 2 * hb, w), BF16) + _nbytes((tq, LANES), F32)
    o, l = pl.pallas_call(
        functools.partial(_attn_kernel, tq=tq, ls=ls),
        grid=grid,
        in_specs=in_specs,
        out_specs=(out_block(w), out_block(LANES)),
        out_shape=(jax.ShapeDtypeStruct((bsz, nb, dil, n, w), BF16),
                   jax.ShapeDtypeStruct((bsz, nb, dil, n, LANES), F32)),
        scratch_shapes=scratch,
        compiler_params=pltpu.CompilerParams(
            dimension_semantics=("arbitrary", "arbitrary", "arbitrary"),
            vmem_limit_bytes=_vmem_limit(blocks, scratch_bytes, 8 * 1024 * 1024),
        ),
        name=f"attn_g{g}",
    )(*([pv] * 7))
    if dil > 1:
        o = o.transpose(0, 1, 3, 2, 4)
        l = l.transpose(0, 1, 3, 2, 4)
    return o.reshape(bsz * seq, w), l.reshape(bsz * seq, LANES)


def _post_norm_residual(out, x_ref, gpost_ref, o_ref, gnext_ref=None, hnext_ref=None):
    ms = jnp.mean(out * out, axis=-1, keepdims=True)
    x1 = x_ref[...] + out * lax.rsqrt(ms + NORM_EPS) * gpost_ref[...]
    o_ref[...] = x1
    if hnext_ref is not None:
        ms1 = jnp.mean(x1 * x1, axis=-1, keepdims=True)
        hnext_ref[...] = (x1 * lax.rsqrt(ms1 + NORM_EPS) * gnext_ref[...]).astype(hnext_ref.dtype)


def _outproj0_kernel(o0_ref, o1_ref, o2_ref, l0_ref, l1_ref, l2_ref, z_ref, w_ref, x_ref, gpost_ref, gnext_ref,
                     x1_ref, hnext_ref, y_scr):
    o_refs = (o0_ref, o1_ref, o2_ref)
    lses = [r[...] for r in (l0_ref, l1_ref, l2_ref)]
    lmax = jnp.maximum(jnp.maximum(lses[0], lses[1]), lses[2])
    es = [jnp.exp(l - lmax) for l in lses]
    tot = es[0] + es[1] + es[2]
    wts = [e / tot for e in es]
    for h in range(ATT_HEADS):
        hs = slice(h * ATT_HEAD_DIM, (h + 1) * ATT_HEAD_DIM)
        o = wts[0][:, h:h + 1] * o_refs[0][:, hs].astype(F32)
        for g in range(1, N_GROUPS):
            o = o + wts[g][:, h:h + 1] * o_refs[g][:, hs].astype(F32)
        z = z_ref[:, hs].astype(F32)
        y_scr[:, hs] = (o * (z * jax.nn.sigmoid(z))).astype(y_scr.dtype)
    out = jnp.dot(y_scr[...], w_ref[...], preferred_element_type=F32)
    _post_norm_residual(out, x_ref, gpost_ref, x1_ref, gnext_ref, hnext_ref)


def _outproj0(outs, lses, proj0, w, x2d, g_post, g_next, tm=256):
    t, d = x2d.shape
    k = w.shape[0]
    row = lambda i: (i, 0)
    vec = pl.BlockSpec((1, d), lambda i: (0, 0))
    zb = 3 * N_GROUPS
    in_specs = ([pl.BlockSpec((tm, k), row)] * N_GROUPS + [pl.BlockSpec((tm, LANES), row)] * N_GROUPS
                + [pl.BlockSpec((tm, k), lambda i: (i, zb)),
                   pl.BlockSpec((k, d), lambda i: (0, 0), pipeline_mode=pl.Buffered(1)),
                   pl.BlockSpec((tm, d), row), vec, vec])
    blocks = ((N_GROUPS + 1) * _nbytes((tm, k), BF16) + N_GROUPS * _nbytes((tm, LANES), F32)
              + 2 * _nbytes((tm, d), F32) + _nbytes((tm, d), BF16))
    return pl.pallas_call(
        _outproj0_kernel,
        grid=(t // tm,),
        in_specs=in_specs,
        out_specs=(pl.BlockSpec((tm, d), row), pl.BlockSpec((tm, d), row)),
        out_shape=(jax.ShapeDtypeStruct((t, d), F32), jax.ShapeDtypeStruct((t, d), BF16)),
        scratch_shapes=[pltpu.VMEM((tm, k), BF16)],
        compiler_params=pltpu.CompilerParams(
            dimension_semantics=("arbitrary",),
            vmem_limit_bytes=_vmem_limit(blocks, _nbytes((k, d), BF16) + _nbytes((tm, k), BF16),
                                         4 * _nbytes((tm, d), F32)),
        ),
        name="outproj0",
    )(*outs, *lses, proj0, w, x2d, g_post.reshape(1, d).astype(F32), g_next.reshape(1, d).astype(F32))


def _outproj1_kernel(hn_ref, og_ref, z_ref, w_ref, x_ref, gpost_ref, o_ref):
    z = z_ref[...].astype(F32)
    y = (jax.nn.sigmoid(og_ref[...].astype(F32)) * hn_ref[...].astype(F32)) * (z * jax.nn.sigmoid(z))
    out = jnp.dot(y.astype(BF16), w_ref[...], preferred_element_type=F32)
    _post_norm_residual(out, x_ref, gpost_ref, o_ref)


def _outproj1(hn, proj1, w, x2d, g_post, tm=256):
    t, d = x2d.shape
    k = w.shape[0]
    row = lambda i: (i, 0)
    ob = (2 * M_QK_WIDTH + M_V_WIDTH) // k
    blocks = 3 * _nbytes((tm, k), BF16) + 2 * _nbytes((tm, d), F32)
    return pl.pallas_call(
        _outproj1_kernel,
        grid=(t // tm,),
        in_specs=[pl.BlockSpec((tm, k), row), pl.BlockSpec((tm, k), lambda i: (i, ob)),
                  pl.BlockSpec((tm, k), lambda i: (i, ob + 1)),
                  pl.BlockSpec((k, d), lambda i: (0, 0), pipeline_mode=pl.Buffered(1)),
                  pl.BlockSpec((tm, d), row), pl.BlockSpec((1, d), lambda i: (0, 0))],
        out_specs=pl.BlockSpec((tm, d), row),
        out_shape=jax.ShapeDtypeStruct((t, d), F32),
        compiler_params=pltpu.CompilerParams(
            dimension_semantics=("arbitrary",),
            vmem_limit_bytes=_vmem_limit(blocks, _nbytes((k, d), BF16), 4 * _nbytes((tm, k), F32)),
        ),
        name="outproj1",
    )(hn, proj1, proj1, w, x2d, g_post.reshape(1, d).astype(F32))


def _inproj1_kernel(h_ref, w_ref, wg_ref, o_ref, g_ref):
    o_ref[...] = jnp.dot(h_ref[...], w_ref[...], preferred_element_type=F32).astype(o_ref.dtype)

    @pl.when(pl.program_id(1) == 0)
    def _():
        g_ref[...] = jnp.dot(h_ref[...], wg_ref[...], preferred_element_type=F32)


def _inproj1(h, w, wg, tm=1024, tn=1024):
    t, k = h.shape
    n = w.shape[1]
    tm = min(tm, t)
    blocks = (_nbytes((tm, k), BF16) + _nbytes((k, tn), BF16) + _nbytes((k, GATE_LANES), BF16)
              + _nbytes((tm, tn), BF16) + _nbytes((tm, GATE_LANES), F32))
    return pl.pallas_call(
        _inproj1_kernel,
        grid=(t // tm, n // tn),
        in_specs=[
            pl.BlockSpec((tm, k), lambda i, j: (i, 0)),
            pl.BlockSpec((k, tn), lambda i, j: (0, j)),
            pl.BlockSpec((k, GATE_LANES), lambda i, j: (0, 0)),
        ],
        out_specs=(pl.BlockSpec((tm, tn), lambda i, j: (i, j)), pl.BlockSpec((tm, GATE_LANES), lambda i, j: (i, 0))),
        out_shape=(jax.ShapeDtypeStruct((t, n), BF16), jax.ShapeDtypeStruct((t, GATE_LANES), F32)),
        compiler_params=pltpu.CompilerParams(
            dimension_semantics=("arbitrary", "arbitrary"),
            vmem_limit_bytes=_vmem_limit(blocks, 0, 2 * _nbytes((tm, tn), F32)),
        ),
        name="inproj1",
    )(h, w, wg)


def _scan_rows(x, rows, op, fill, reverse):
    n = x.shape[0]
    k = 1
    while k < n:
        if reverse:
            shifted = jnp.where(rows < n - k, pltpu.roll(x, n - k, 0), fill)
        else:
            shifted = jnp.where(rows >= k, pltpu.roll(x, k, 0), fill)
        x = op(x, shifted)
        k *= 2
    return x


def _gate_prep_kernel(g_ref, b_ref, o_ref):
    g = g_ref[0] + b_ref[...]
    rows = lax.broadcasted_iota(jnp.int32, g.shape, 0)
    kind = lax.broadcasted_iota(jnp.int32, g.shape, 1) % 8
    lf = jnp.minimum(g, 0.0) - jnp.log1p(jnp.exp(-jnp.abs(g)))
    csum = _scan_rows(lf, rows, jnp.add, 0.0, reverse=False)
    rsum = _scan_rows(lf, rows, jnp.add, 0.0, reverse=True)
    b_at_i = pltpu.roll(jnp.where(kind == 1, csum, rsum), LANES - 1, 1)
    grow = g - b_at_i
    cm_f = _scan_rows(grow, rows, jnp.maximum, NEG_INF, reverse=False)
    cm_b = _scan_rows(grow, rows, jnp.maximum, NEG_INF, reverse=True)
    y = jnp.where(kind == 0, b_at_i, 0.0)
    y = jnp.where(kind == 1, pltpu.roll(grow, 1, 1), y)
    y = jnp.where(kind == 2, pltpu.roll(cm_f, 2, 1), y)
    y = jnp.where(kind == 3, rsum, y)
    y = jnp.where(kind == 4, pltpu.roll(grow, 2, 1), y)
    y = jnp.where(kind == 5, pltpu.roll(cm_b, 3, 1), y)
    yt = y.T
    o_ref[0, :, 0] = yt[:M_HEADS * 8].reshape(M_HEADS, 8, g.shape[0])


def _gate_prep(gates, bias, bsz, seq):
    nc = seq // M_CHUNK
    return pl.pallas_call(
        _gate_prep_kernel,
        grid=(bsz, nc),
        in_specs=[pl.BlockSpec((1, M_CHUNK, GATE_LANES), lambda b, c: (b, c, 0)),
                  pl.BlockSpec((1, GATE_LANES), lambda b, c: (0, 0))],
        out_specs=pl.BlockSpec((1, M_HEADS, 1, 8, M_CHUNK), lambda b, c: (b, 0, c, 0, 0)),
        out_shape=jax.ShapeDtypeStruct((bsz, M_HEADS, nc, 8, M_CHUNK), F32),
        compiler_params=pltpu.CompilerParams(dimension_semantics=("arbitrary", "arbitrary")),
        name="gate_prep",
    )(gates.reshape(bsz, seq, GATE_LANES), bias)


def _mlstm_kernel(q_ref, k_ref, v_ref, r_ref, gain_ref, o_ref, hs_scr, cf_scr, cb_scr, *, nc):
    L = M_CHUNK
    rows = lax.broadcasted_iota(jnp.int32, (L, L), 0)
    cols = lax.broadcasted_iota(jnp.int32, (L, L), 1)
    eye = rows == cols
    causal = cols <= rows
    anti = cols >= rows
    gain = gain_ref[0]

    def to_col(r):
        return jnp.sum(jnp.where(eye, r, 0.0), axis=1, keepdims=True)

    def chunk(c, c_scr, n, m, fwd):
        st = pl.multiple_of(c * L, L)
        qc = q_ref[0, pl.ds(st, L), :] * (M_QK_DIM ** -0.5)
        kc = k_ref[0, pl.ds(st, L), :]
        vc = v_ref[0, pl.ds(st, L), :]
        r = r_ref[0, 0, pl.ds(c, 1)].reshape(8, L)
        base = 0 if fwd else 3
        grow = r[base + 1:base + 2]
        bcol = to_col(r[base:base + 1])
        gcol = to_col(grow)
        mm = jnp.maximum(to_col(r[base + 2:base + 3]), m)
        e = L - 1 if fwd else 0
        mm_l = mm[e:e + 1]
        s = lax.dot_general(qc, kc, (((1,), (1,)), ((), ())), preferred_element_type=F32)
        a = jnp.exp(jnp.where(causal if fwd else anti, grow - mm, NEG_INF)) * s
        gint = jnp.exp(m - mm)
        cmat = c_scr[...]
        num = (jnp.dot(a.astype(BF16), vc, preferred_element_type=F32)
               + gint * jnp.dot(qc, cmat.astype(BF16), preferred_element_type=F32))
        qn = jnp.sum(qc.astype(F32) * n, axis=1, keepdims=True)
        den = jnp.sum(a, axis=1, keepdims=True) + gint * qn
        hc = num / jnp.maximum(jnp.abs(den), jnp.exp(-(bcol + mm)))
        decay = jnp.exp(m - mm_l)
        wk = jnp.exp(gcol - mm_l) * kc.astype(F32)
        c_scr[...] = decay * cmat + lax.dot_general(wk.astype(BF16), vc, (((0,), (0,)), ((), ())),
                                                    preferred_element_type=F32)
        n_new = decay * n + jnp.sum(wk, axis=0, keepdims=True)
        m_new = bcol[e:e + 1] + mm_l
        return hc, n_new, m_new

    def finish(c, hsum):
        st = pl.multiple_of(c * L, L)
        ms = jnp.mean(hsum * hsum, axis=-1, keepdims=True)
        o_ref[0, pl.ds(st, L), :] = (hsum * lax.rsqrt(ms + NORM_EPS) * gain).astype(o_ref.dtype)

    def step(j, carry, second_half):
        nf, mf, nb, mb = carry
        cf, cb = j, nc - 1 - j
        hf, nf, mf = chunk(cf, cf_scr, nf, mf, True)
        hb, nb, mb = chunk(cb, cb_scr, nb, mb, False)
        sf = pl.multiple_of(cf * L, L)
        sb = pl.multiple_of(cb * L, L)
        if second_half:
            finish(cf, hs_scr[pl.ds(sf, L), :] + hf)
            finish(cb, hs_scr[pl.ds(sb, L), :] + hb)
        else:
            hs_scr[pl.ds(sf, L), :] = hf
            hs_scr[pl.ds(sb, L), :] = hb
        return nf, mf, nb, mb

    cf_scr[...] = jnp.zeros_like(cf_scr)
    cb_scr[...] = jnp.zeros_like(cb_scr)
    n0 = jnp.zeros((1, M_QK_DIM), F32)
    m0 = jnp.zeros((1, 1), F32)
    carry = lax.fori_loop(0, nc // 2, functools.partial(step, second_half=False), (n0, m0, n0, m0))
    lax.fori_loop(nc // 2, nc, functools.partial(step, second_half=True), carry)


def _mlstm(proj1, prep, gain, bsz, seq):
    nc = seq // M_CHUNK
    assert nc % 2 == 0
    p3 = proj1.reshape(bsz, seq, M_MAIN_COLS)
    kb = M_QK_WIDTH // M_QK_DIM
    vb = 2 * M_QK_WIDTH // M_V_DIM
    blocks = (2 * _nbytes((seq, M_QK_DIM), BF16) + 2 * _nbytes((seq, M_V_DIM), BF16)
              + _nbytes((nc, 8, M_CHUNK), F32) + _nbytes((1, M_V_DIM), F32))
    scratch_bytes = _nbytes((seq, M_V_DIM), F32) + 2 * _nbytes((M_QK_DIM, M_V_DIM), F32)
    return pl.pallas_call(
        functools.partial(_mlstm_kernel, nc=nc),
        grid=(bsz, M_HEADS),
        in_specs=[
            pl.BlockSpec((1, seq, M_QK_DIM), lambda b, h: (b, 0, h)),
            pl.BlockSpec((1, seq, M_QK_DIM), lambda b, h: (b, 0, kb + h)),
            pl.BlockSpec((1, seq, M_V_DIM), lambda b, h: (b, 0, vb + h)),
            pl.BlockSpec((1, 1, nc, 8, M_CHUNK), lambda b, h: (b, h, 0, 0, 0)),
            pl.BlockSpec((1, 1, M_V_DIM), lambda b, h: (h, 0, 0)),
        ],
        out_specs=pl.BlockSpec((1, seq, M_V_DIM), lambda b, h: (b, 0, h)),
        out_shape=jax.ShapeDtypeStruct((bsz, seq, M_V_WIDTH), BF16),
        scratch_shapes=[pltpu.VMEM((seq, M_V_DIM), F32), pltpu.VMEM((M_QK_DIM, M_V_DIM), F32),
                        pltpu.VMEM((M_QK_DIM, M_V_DIM), F32)],
        compiler_params=pltpu.CompilerParams(
            dimension_semantics=("arbitrary", "arbitrary"),
            vmem_limit_bytes=_vmem_limit(blocks, scratch_bytes, 8 * 1024 * 1024),
        ),
        name="mlstm",
    )(p3, p3, p3, prep, gain.reshape(M_HEADS, 1, M_V_DIM).astype(F32))


def _gate_weights(w_in1, b_gate):
    wg = w_in1[:, M_MAIN_COLS:].reshape(D_MODEL, 4, M_HEADS).transpose(0, 2, 1)
    wg = jnp.pad(wg, ((0, 0), (0, 0), (0, 4))).reshape(D_MODEL, M_HEADS * 8)
    wg = jnp.pad(wg, ((0, 0), (0, GATE_LANES - M_HEADS * 8)))
    bg = jnp.pad(b_gate.astype(F32).reshape(4, M_HEADS).T, ((0, 0), (0, 4))).reshape(1, M_HEADS * 8)
    bg = jnp.pad(bg, ((0, 0), (0, GATE_LANES - M_HEADS * 8)))
    return wg.astype(BF16), bg


def _trunk(x, p):
    bsz, seq, d = x.shape
    t = bsz * seq
    assert seq % (2 * PERM_BLOCK) == 0
    x2d = x.reshape(t, d)
    h0 = _rmsnorm_orders(x2d, p["l0_norm_pre"])
    proj0 = _inproj0(h0, p["l0_w_in"], p["rope"], seq)
    outs, lses = zip(*[_attention_group(proj0, g, bsz, seq) for g in range(N_GROUPS)])
    x1, h1 = _outproj0(outs, lses, proj0, p["l0_w_out"], x2d, p["l0_norm_post"], p["l1_norm_pre"])
    proj1, gates = _inproj1(h1, p["l1_w_main"], p["l1_w_gate"])
    prep = _gate_prep(gates, p["l1_b_gate"], bsz, seq)
    hn = _mlstm(proj1, prep, p["l1_head_norm"], bsz, seq)
    y = _outproj1(hn.reshape(t, M_V_WIDTH), proj1, p["l1_w_out"], x1, p["l1_norm_post"])
    return y.reshape(bsz, seq, d)


def kernel(x_prompt, x_sample, l0_norm_pre, l0_w_in, l0_w_out, l0_norm_post,
           l1_norm_pre, l1_w_in, l1_b_gate, l1_head_norm, l1_w_out, l1_norm_post):
    assert x_prompt.shape[1] == x_sample.shape[1]
    w_gate, b_gate = _gate_weights(l1_w_in, l1_b_gate)
    p = {
        "l0_norm_pre": l0_norm_pre, "l0_w_in": l0_w_in.astype(BF16), "l0_w_out": l0_w_out.astype(BF16),
        "l0_norm_post": l0_norm_post, "l1_norm_pre": l1_norm_pre,
        "l1_w_main": l1_w_in[:, :M_MAIN_COLS].astype(BF16), "l1_w_gate": w_gate, "l1_b_gate": b_gate,
        "l1_head_norm": l1_head_norm, "l1_w_out": l1_w_out.astype(BF16), "l1_norm_post": l1_norm_post,
        "rope": _rope_tables(x_prompt.shape[1]),
    }
    return (_trunk(x_prompt, p), _trunk(x_sample, p))
```

```python
import functools

import jax
import jax.numpy as jnp
from jax import lax
from jax.experimental import pallas as pl
from jax.experimental.pallas import tpu as pltpu

F32 = jnp.float32
BF16 = jnp.bfloat16

D_MODEL = 2048
ATT_GROUPS = ((128, 1), (512, 4), (2048, 16))
N_GROUPS = len(ATT_GROUPS)
ATT_HEADS = 16
ATT_HEAD_DIM = 128
ATT_WIDTH = ATT_HEADS * ATT_HEAD_DIM
ATT_IN_COLS = 3 * N_GROUPS * ATT_WIDTH + ATT_WIDTH
ATT_HALF = 64
ROPE_THETA = 500000.0
ROPE_DIMS = ATT_HEAD_DIM // 4
ROPE_HALF = ROPE_DIMS // 2

M_HEADS = 8
M_QK_DIM = 256
M_V_DIM = 512
M_QK_WIDTH = M_HEADS * M_QK_DIM
M_V_WIDTH = M_HEADS * M_V_DIM
M_MAIN_COLS = 2 * M_QK_WIDTH + 3 * M_V_WIDTH
M_CHUNK = 128
GATE_LANES = 128

NORM_EPS = 1e-6
NEG_INF = -1e30

LANES = 128
VMEM_CAP_BYTES = 56 * 1024 * 1024


def _vmem_limit(block_bytes, scratch_bytes=0, temp_bytes=0):
    need = 2 * block_bytes + scratch_bytes + temp_bytes
    return int(min(max(need, 16 * 1024 * 1024), VMEM_CAP_BYTES))


def _nbytes(shape, dtype):
    n = 1
    for s in shape:
        n *= s
    return n * jnp.dtype(dtype).itemsize


PERM_BLOCK = 1024


def _rmsnorm_orders_kernel(x_ref, g_ref, o_ref, slab_scr):
    x = x_ref[...]
    inv = lax.rsqrt(jnp.mean(x * x, axis=-1, keepdims=True) + NORM_EPS)
    nslab = x_ref.shape[1] // LANES
    for c in range(nslab):
        ls = slice(c * LANES, (c + 1) * LANES)
        y = x_ref[:, ls] * inv * g_ref[:, ls]
        slab_scr[c] = y
        for g, (_, dil) in enumerate(ATT_GROUPS):
            if dil == 1:
                o_ref[g, :, ls] = y.astype(o_ref.dtype)
    for g, (_, dil) in enumerate(ATT_GROUPS):
        if dil == 1:
            continue
        n = PERM_BLOCK // dil
        for r in range(dil):
            for c in range(nslab):
                o_ref[g, r * n:(r + 1) * n, c * LANES:(c + 1) * LANES] = (
                    slab_scr[c, pl.ds(r, n, stride=dil), :].astype(o_ref.dtype))


def _rmsnorm_orders(x2d, gain):
    t, d = x2d.shape
    tm = PERM_BLOCK
    return pl.pallas_call(
        _rmsnorm_orders_kernel,
        grid=(t // tm,),
        in_specs=[pl.BlockSpec((tm, d), lambda i: (i, 0)), pl.BlockSpec((1, d), lambda i: (0, 0))],
        out_specs=pl.BlockSpec((N_GROUPS, tm, d), lambda i: (0, i, 0)),
        out_shape=jax.ShapeDtypeStruct((N_GROUPS, t, d), BF16),
        scratch_shapes=[pltpu.VMEM((d // LANES, tm, LANES), F32)],
        compiler_params=pltpu.CompilerParams(
            dimension_semantics=("arbitrary",),
            vmem_limit_bytes=_vmem_limit(_nbytes((tm, d), F32) + _nbytes((N_GROUPS, tm, d), BF16),
                                         _nbytes((tm, d), F32), _nbytes((tm, d), F32)),
        ),
        name="rmsnorm_orders",
    )(x2d, gain.reshape(1, d).astype(F32))


def _order_index(seq, dil):
    n = PERM_BLOCK // dil
    return jnp.arange(seq, dtype=jnp.int32).reshape(seq // PERM_BLOCK, n, dil).transpose(0, 2, 1).reshape(seq)


INPROJ0_TN = 2048
INPROJ0_CHUNK = 256
_GROUP_TILES = 3 * ATT_WIDTH // INPROJ0_TN
_Z_TILES = ATT_WIDTH // INPROJ0_TN


def _inproj0_col_tile(jj):
    return jnp.where(jj < _GROUP_TILES, jj,
                     jnp.where(jj < _GROUP_TILES + _Z_TILES, jj + (N_GROUPS - 1) * _GROUP_TILES, jj - _Z_TILES))


def _inproj0_order(jj):
    return jnp.where(jj < _GROUP_TILES + _Z_TILES, 0, (jj - _Z_TILES) // _GROUP_TILES)


def _inproj0_kernel(h_ref, w_ref, c_ref, s1_ref, s2_ref, o_ref):
    col = _inproj0_col_tile(pl.program_id(1))
    seg = (col * INPROJ0_TN) // ATT_WIDTH
    kind = seg % 3
    is_rope = jnp.logical_and(seg < 3 * N_GROUPS, kind < 2)
    scale = jnp.where(kind == 0, ATT_HEAD_DIM ** -0.5, 1.0).astype(F32)
    c = jnp.where(is_rope, c_ref[...] * scale, 1.0)
    s1 = jnp.where(is_rope, s1_ref[...] * scale, 0.0)
    s2 = jnp.where(is_rope, s2_ref[...] * scale, 0.0)
    h = h_ref[...]
    for cc in range(INPROJ0_TN // INPROJ0_CHUNK):
        acc = jnp.dot(h, w_ref[:, cc * INPROJ0_CHUNK:(cc + 1) * INPROJ0_CHUNK], preferred_element_type=F32)
        for t in range(INPROJ0_CHUNK // LANES):
            a = acc[:, t * LANES:(t + 1) * LANES]
            r = a * c + pltpu.roll(a, LANES - ROPE_HALF, 1) * s1 + pltpu.roll(a, ROPE_HALF, 1) * s2
            lo = cc * INPROJ0_CHUNK + t * LANES
            o_ref[:, lo:lo + LANES] = r.astype(o_ref.dtype)


def _rope_tables(seq):
    inv = jnp.power(ROPE_THETA, -jnp.arange(ROPE_HALF, dtype=F32) / ROPE_HALF)
    ang = jnp.arange(seq, dtype=F32)[:, None] * inv[None, :]
    cos, sin = jnp.cos(ang), jnp.sin(ang)
    zeros = jnp.zeros((seq, LANES - ROPE_DIMS), F32)
    zh = jnp.zeros((seq, ROPE_HALF), F32)
    c = jnp.concatenate([cos, cos, jnp.ones((seq, LANES - ROPE_DIMS), F32)], axis=1)
    s1 = jnp.concatenate([-sin, zh, zeros], axis=1)
    s2 = jnp.concatenate([zh, sin, zeros], axis=1)
    orders = [_order_index(seq, dil) for _, dil in ATT_GROUPS]
    return tuple(jnp.stack([tab[idx] for idx in orders]) for tab in (c, s1, s2))


def _inproj0(h_orders, w, tables, seq):
    _, t, k = h_orders.shape
    n = w.shape[1]
    tm, tn = PERM_BLOCK, INPROJ0_TN
    pos_blocks = seq // tm
    tab_spec = pl.BlockSpec((None, tm, LANES), lambda i, j: (_inproj0_order(j), i % pos_blocks, 0))
    blocks = _nbytes((tm, k), BF16) + _nbytes((k, tn), BF16) + _nbytes((tm, tn), BF16) + 3 * _nbytes((tm, LANES), F32)
    return pl.pallas_call(
        _inproj0_kernel,
        grid=(t // tm, n // tn),
        in_specs=[
            pl.BlockSpec((None, tm, k), lambda i, j: (_inproj0_order(j), i, 0)),
            pl.BlockSpec((k, tn), lambda i, j: (0, _inproj0_col_tile(j))),
            tab_spec, tab_spec, tab_spec,
        ],
        out_specs=pl.BlockSpec((tm, tn), lambda i, j: (i, _inproj0_col_tile(j))),
        out_shape=jax.ShapeDtypeStruct((t, n), BF16),
        compiler_params=pltpu.CompilerParams(
            dimension_semantics=("arbitrary", "arbitrary"),
            vmem_limit_bytes=_vmem_limit(blocks, 0, _nbytes((tm, k), BF16) + 3 * _nbytes((tm, LANES), F32)
                                         + 4 * _nbytes((tm, INPROJ0_CHUNK), F32)),
        ),
        name="inproj0",
    )(h_orders, w, *tables)


ATT_TQ = 256


def _attn_kernel(q_ref, kp_ref, kc_ref, kn_ref, vp_ref, vc_ref, vn_ref, o_ref, l_ref, qx, kx, vx, ox, lx, *, tq, ls):
    hb = ATT_HALF
    sub = 2 * hb
    w = qx.shape[1]
    i = pl.program_id(2)
    qx[...] = q_ref[0].reshape(tq, w)
    kx[0:hb, :] = kp_ref[0].reshape(hb, w)
    kx[hb:hb + tq, :] = kc_ref[0].reshape(tq, w)
    kx[hb + tq:, :] = kn_ref[0].reshape(hb, w)
    vx[0:hb, :] = vp_ref[0].reshape(hb, w)
    vx[hb:hb + tq, :] = vc_ref[0].reshape(tq, w)
    vx[hb + tq:, :] = vn_ref[0].reshape(hb, w)

    row = lax.broadcasted_iota(jnp.int32, (sub, 2 * sub), 0)
    col = lax.broadcasted_iota(jnp.int32, (sub, 2 * sub), 1)
    band = jnp.abs(col - hb - row) <= hb
    lane = lax.broadcasted_iota(jnp.int32, (sub, LANES), 1)

    for a in range(tq // sub):
        r0 = a * sub
        kidx = i * tq + r0 + col - hb
        valid = band & (kidx >= 0) & (kidx < ls)
        lse_tile = jnp.zeros((sub, LANES), F32)
        for h in range(ATT_HEADS):
            hs = slice(h * ATT_HEAD_DIM, (h + 1) * ATT_HEAD_DIM)
            qh = qx[r0:r0 + sub, hs]
            kh = kx[r0:r0 + 2 * sub, hs]
            vh = vx[r0:r0 + 2 * sub, hs]
            s = lax.dot_general(qh, kh, (((1,), (1,)), ((), ())), preferred_element_type=F32)
            s = jnp.where(valid, s, NEG_INF)
            m = jnp.max(s, axis=-1, keepdims=True)
            p = jnp.exp(s - m)
            den = jnp.sum(p, axis=-1, keepdims=True)
            o = jnp.dot(p.astype(BF16), vh, preferred_element_type=F32) / den
            lse_tile = jnp.where(lane == h, m + jnp.log(den), lse_tile)
            ox[r0:r0 + sub, hs] = o.astype(ox.dtype)
        lx[r0:r0 + sub, :] = lse_tile
    o_ref[0] = ox[...].reshape(o_ref.shape[1:])
    l_ref[0] = lx[...].reshape(l_ref.shape[1:])


def _attention_group(proj, g, bsz, seq):
    _, dil = ATT_GROUPS[g]
    w = ATT_WIDTH
    hb = ATT_HALF
    ls = seq // dil
    if dil == 1:
        nb, n = seq // ATT_TQ, ATT_TQ
    else:
        nb, n = seq // PERM_BLOCK, PERM_BLOCK // dil
    whole_class = n < ATT_TQ
    tq = ls if whole_class else ATT_TQ
    assert tq % (2 * hb) == 0 and n % hb == 0
    pv = proj.reshape(bsz, nb, dil, n, ATT_IN_COLS)
    ncb = 3 * g

    if whole_class:
        grid = (bsz, dil, 1)
        main = lambda cb: pl.BlockSpec((1, nb, None, n, w), lambda b, r, i: (b, 0, r, 0, cb))
        prev_h = next_h = lambda cb: pl.BlockSpec((1, 1, None, hb, w), lambda b, r, i: (b, 0, r, 0, cb))
        out_block = lambda width: pl.BlockSpec((1, nb, None, n, width), lambda b, r, i: (b, 0, r, 0, 0))
    else:
        grid = (bsz, dil, nb)
        main = lambda cb: pl.BlockSpec((1, 1, None, n, w), lambda b, r, i: (b, i, r, 0, cb))
        prev_h = lambda cb: pl.BlockSpec((1, 1, None, hb, w),
                                         lambda b, r, i: (b, jnp.maximum(i - 1, 0), r, n // hb - 1, cb))
        next_h = lambda cb: pl.BlockSpec((1, 1, None, hb, w),
                                         lambda b, r, i: (b, jnp.minimum(i + 1, nb - 1), r, 0, cb))
        out_block = lambda width: pl.BlockSpec((1, 1, None, n, width), lambda b, r, i: (b, i, r, 0, 0))

    in_specs = [main(ncb), prev_h(ncb + 1), main(ncb + 1), next_h(ncb + 1),
                prev_h(ncb + 2), main(ncb + 2), next_h(ncb + 2)]
    blocks = 4 * _nbytes((tq, w), BF16) + 4 * _nbytes((hb, w), BF16) + _nbytes((tq, LANES), F32)
    scratch = [pltpu.VMEM((tq, w), BF16), pltpu.VMEM((tq + 2 * hb, w), BF16), pltpu.VMEM((tq + 2 * hb, w), BF16),
               pltpu.VMEM((tq, w), BF16), pltpu.VMEM((tq, LANES), F32)]
    scratch_bytes = 4 * _nbytes((tq + 2 * hb, w), BF16) + _nbytes((tq, LANES), F32)
    o, l = pl.pallas_call(
        functools.partial(_attn_kernel, tq=tq, ls=ls),
        grid=grid,
        in_specs=in_specs,
        out_specs=(out_block(w), out_block(LANES)),
        out_shape=(jax.ShapeDtypeStruct((bsz, nb, dil, n, w), BF16),
                   jax.ShapeDtypeStruct((bsz, nb, dil, n, LANES), F32)),
        scratch_shapes=scratch,
        compiler_params=pltpu.CompilerParams(
            dimension_semantics=("arbitrary", "arbitrary", "arbitrary"),
            vmem_limit_bytes=_vmem_limit(blocks, scratch_bytes, 8 * 1024 * 1024),
        ),
        name=f"attn_g{g}",
    )(*([pv] * 7))
    if dil > 1:
        o = o.transpose(0, 1, 3, 2, 4)
        l = l.transpose(0, 1, 3, 2, 4)
    return o.reshape(bsz * seq, w), l.reshape(bsz * seq, LANES)


def _post_norm_residual(out, x_ref, gpost_ref, o_ref, gnext_ref=None, hnext_ref=None):
    ms = jnp.mean(out * out, axis=-1, keepdims=True)
    x1 = x_ref[...] + out * lax.rsqrt(ms + NORM_EPS) * gpost_ref[...]
    o_ref[...] = x1
    if hnext_ref is not None:
        ms1 = jnp.mean(x1 * x1, axis=-1, keepdims=True)
        hnext_ref[...] = (x1 * lax.rsqrt(ms1 + NORM_EPS) * gnext_ref[...]).astype(hnext_ref.dtype)


OUTPROJ_KCHUNK = 512


def _sigmoid(x):
    return 0.5 * jnp.tanh(0.5 * x) + 0.5


def _outproj0_kernel(o0_ref, o1_ref, o2_ref, l0_ref, l1_ref, l2_ref, z_ref, w_ref, x_ref, gpost_ref, gnext_ref,
                     x1_ref, hnext_ref):
    o_refs = (o0_ref, o1_ref, o2_ref)
    lses = [r[...] for r in (l0_ref, l1_ref, l2_ref)]
    lmax = jnp.maximum(jnp.maximum(lses[0], lses[1]), lses[2])
    es = [jnp.exp(l - lmax) for l in lses]
    tot = es[0] + es[1] + es[2]
    wts = [e / tot for e in es]
    heads_per_chunk = OUTPROJ_KCHUNK // ATT_HEAD_DIM
    out = None
    for c in range(ATT_HEADS // heads_per_chunk):
        ys = []
        for h in range(c * heads_per_chunk, (c + 1) * heads_per_chunk):
            hs = slice(h * ATT_HEAD_DIM, (h + 1) * ATT_HEAD_DIM)
            o = wts[0][:, h:h + 1] * o_refs[0][:, hs].astype(F32)
            for g in range(1, N_GROUPS):
                o = o + wts[g][:, h:h + 1] * o_refs[g][:, hs].astype(F32)
            z = z_ref[:, hs].astype(F32)
            ys.append((o * (z * _sigmoid(z))).astype(BF16))
        part = jnp.dot(jnp.concatenate(ys, axis=1), w_ref[c * OUTPROJ_KCHUNK:(c + 1) * OUTPROJ_KCHUNK, :],
                       preferred_element_type=F32)
        out = part if out is None else out + part
    _post_norm_residual(out, x_ref, gpost_ref, x1_ref, gnext_ref, hnext_ref)


def _outproj0(outs, lses, proj0, w, x2d, g_post, g_next, tm=256):
    t, d = x2d.shape
    k = w.shape[0]
    row = lambda i: (i, 0)
    vec = pl.BlockSpec((1, d), lambda i: (0, 0))
    zb = 3 * N_GROUPS
    in_specs = ([pl.BlockSpec((tm, k), row)] * N_GROUPS + [pl.BlockSpec((tm, LANES), row)] * N_GROUPS
                + [pl.BlockSpec((tm, k), lambda i: (i, zb)),
                   pl.BlockSpec((k, d), lambda i: (0, 0), pipeline_mode=pl.Buffered(1)),
                   pl.BlockSpec((tm, d), row), vec, vec])
    blocks = ((N_GROUPS + 1) * _nbytes((tm, k), BF16) + N_GROUPS * _nbytes((tm, LANES), F32)
              + 2 * _nbytes((tm, d), F32) + _nbytes((tm, d), BF16))
    return pl.pallas_call(
        _outproj0_kernel,
        grid=(t // tm,),
        in_specs=in_specs,
        out_specs=(pl.BlockSpec((tm, d), row), pl.BlockSpec((tm, d), row)),
        out_shape=(jax.ShapeDtypeStruct((t, d), F32), jax.ShapeDtypeStruct((t, d), BF16)),
        compiler_params=pltpu.CompilerParams(
            dimension_semantics=("arbitrary",),
            vmem_limit_bytes=_vmem_limit(blocks, _nbytes((k, d), BF16), 4 * _nbytes((tm, d), F32)),
        ),
        name="outproj0",
    )(*outs, *lses, proj0, w, x2d, g_post.reshape(1, d).astype(F32), g_next.reshape(1, d).astype(F32))


def _outproj1_kernel(hn_ref, og_ref, z_ref, w_ref, x_ref, gpost_ref, o_ref):
    out = None
    for c in range(hn_ref.shape[1] // OUTPROJ_KCHUNK):
        ks = slice(c * OUTPROJ_KCHUNK, (c + 1) * OUTPROJ_KCHUNK)
        z = z_ref[:, ks].astype(F32)
        y = (_sigmoid(og_ref[:, ks].astype(F32)) * hn_ref[:, ks].astype(F32)) * (z * _sigmoid(z))
        part = jnp.dot(y.astype(BF16), w_ref[ks, :], preferred_element_type=F32)
        out = part if out is None else out + part
    _post_norm_residual(out, x_ref, gpost_ref, o_ref)


def _outproj1(hn, proj1, w, x2d, g_post, tm=256):
    t, d = x2d.shape
    k = w.shape[0]
    row = lambda i: (i, 0)
    ob = (2 * M_QK_WIDTH + M_V_WIDTH) // k
    blocks = 3 * _nbytes((tm, k), BF16) + 2 * _nbytes((tm, d), F32)
    return pl.pallas_call(
        _outproj1_kernel,
        grid=(t // tm,),
        in_specs=[pl.BlockSpec((tm, k), row), pl.BlockSpec((tm, k), lambda i: (i, ob)),
                  pl.BlockSpec((tm, k), lambda i: (i, ob + 1)),
                  pl.BlockSpec((k, d), lambda i: (0, 0), pipeline_mode=pl.Buffered(1)),
                  pl.BlockSpec((tm, d), row), pl.BlockSpec((1, d), lambda i: (0, 0))],
        out_specs=pl.BlockSpec((tm, d), row),
        out_shape=jax.ShapeDtypeStruct((t, d), F32),
        compiler_params=pltpu.CompilerParams(
            dimension_semantics=("arbitrary",),
            vmem_limit_bytes=_vmem_limit(blocks, _nbytes((k, d), BF16), 4 * _nbytes((tm, k), F32)),
        ),
        name="outproj1",
    )(hn, proj1, proj1, w, x2d, g_post.reshape(1, d).astype(F32))


INPROJ1_CHUNK = 512


def _inproj1_kernel(h_ref, w_ref, wg_ref, o_ref, g_ref):
    h = h_ref[...]
    for cc in range(o_ref.shape[1] // INPROJ1_CHUNK):
        cs = slice(cc * INPROJ1_CHUNK, (cc + 1) * INPROJ1_CHUNK)
        o_ref[:, cs] = jnp.dot(h, w_ref[:, cs], preferred_element_type=F32).astype(o_ref.dtype)

    @pl.when(pl.program_id(1) == 0)
    def _():
        g_ref[...] = jnp.dot(h_ref[...], wg_ref[...], preferred_element_type=F32)


def _inproj1(h, w, wg, tm=1024, tn=2048):
    t, k = h.shape
    n = w.shape[1]
    tm = min(tm, t)
    blocks = (_nbytes((tm, k), BF16) + _nbytes((k, tn), BF16) + _nbytes((k, GATE_LANES), BF16)
              + _nbytes((tm, tn), BF16) + _nbytes((tm, GATE_LANES), F32))
    return pl.pallas_call(
        _inproj1_kernel,
        grid=(t // tm, n // tn),
        in_specs=[
            pl.BlockSpec((tm, k), lambda i, j: (i, 0)),
            pl.BlockSpec((k, tn), lambda i, j: (0, j)),
            pl.BlockSpec((k, GATE_LANES), lambda i, j: (0, 0)),
        ],
        out_specs=(pl.BlockSpec((tm, tn), lambda i, j: (i, j)), pl.BlockSpec((tm, GATE_LANES), lambda i, j: (i, 0))),
        out_shape=(jax.ShapeDtypeStruct((t, n), BF16), jax.ShapeDtypeStruct((t, GATE_LANES), F32)),
        compiler_params=pltpu.CompilerParams(
            dimension_semantics=("arbitrary", "arbitrary"),
            vmem_limit_bytes=_vmem_limit(blocks, 0, 4 * _nbytes((tm, INPROJ1_CHUNK), F32)),
        ),
        name="inproj1",
    )(h, w, wg)


def _scan_rows(x, rows, op, fill, reverse):
    n = x.shape[0]
    k = 1
    while k < n:
        if reverse:
            shifted = jnp.where(rows < n - k, pltpu.roll(x, n - k, 0), fill)
        else:
            shifted = jnp.where(rows >= k, pltpu.roll(x, k, 0), fill)
        x = op(x, shifted)
        k *= 2
    return x


GATE_PREP_CHUNKS = 4


def _gate_prep_kernel(g_ref, b_ref, o_ref):
    for ci in range(GATE_PREP_CHUNKS):
        _gate_prep_chunk(g_ref[0, ci * M_CHUNK:(ci + 1) * M_CHUNK, :] + b_ref[...], o_ref, ci)


def _gate_prep_chunk(g, o_ref, ci):
    rows = lax.broadcasted_iota(jnp.int32, g.shape, 0)
    kind = lax.broadcasted_iota(jnp.int32, g.shape, 1) % 8
    lf = jnp.minimum(g, 0.0) - jnp.log1p(jnp.exp(-jnp.abs(g)))
    csum = _scan_rows(lf, rows, jnp.add, 0.0, reverse=False)
    rsum = _scan_rows(lf, rows, jnp.add, 0.0, reverse=True)
    b_at_i = pltpu.roll(jnp.where(kind == 1, csum, rsum), LANES - 1, 1)
    grow = g - b_at_i
    cm_f = _scan_rows(grow, rows, jnp.maximum, NEG_INF, reverse=False)
    cm_b = _scan_rows(grow, rows, jnp.maximum, NEG_INF, reverse=True)
    y = jnp.where(kind == 0, b_at_i, 0.0)
    y = jnp.where(kind == 1, pltpu.roll(grow, 1, 1), y)
    y = jnp.where(kind == 2, pltpu.roll(cm_f, 2, 1), y)
    y = jnp.where(kind == 3, rsum, y)
    y = jnp.where(kind == 4, pltpu.roll(grow, 2, 1), y)
    y = jnp.where(kind == 5, pltpu.roll(cm_b, 3, 1), y)
    yt = y.T
    o_ref[0, :, ci] = yt[:M_HEADS * 8].reshape(M_HEADS, 8, g.shape[0])


def _gate_prep(gates, bias, bsz, seq):
    nc = seq // M_CHUNK
    per = GATE_PREP_CHUNKS
    return pl.pallas_call(
        _gate_prep_kernel,
        grid=(bsz, nc // per),
        in_specs=[pl.BlockSpec((1, per * M_CHUNK, GATE_LANES), lambda b, c: (b, c, 0)),
                  pl.BlockSpec((1, GATE_LANES), lambda b, c: (0, 0))],
        out_specs=pl.BlockSpec((1, M_HEADS, per, 8, M_CHUNK), lambda b, c: (b, 0, c, 0, 0)),
        out_shape=jax.ShapeDtypeStruct((bsz, M_HEADS, nc, 8, M_CHUNK), F32),
        compiler_params=pltpu.CompilerParams(dimension_semantics=("arbitrary", "arbitrary")),
        name="gate_prep",
    )(gates.reshape(bsz, seq, GATE_LANES), bias)


def _mlstm_kernel(q_ref, k_ref, v_ref, r_ref, gain_ref, o_ref, hs_scr, cf_scr, cb_scr, *, nc):
    L = M_CHUNK
    rows = lax.broadcasted_iota(jnp.int32, (L, L), 0)
    cols = lax.broadcasted_iota(jnp.int32, (L, L), 1)
    eye = rows == cols
    causal = cols <= rows
    anti = cols >= rows
    gain = gain_ref[0]

    def to_col(r):
        return jnp.sum(jnp.where(eye, r, 0.0), axis=1, keepdims=True)

    def chunk(c, c_scr, n, m, fwd):
        st = pl.multiple_of(c * L, L)
        qc = q_ref[0, pl.ds(st, L), :] * (M_QK_DIM ** -0.5)
        kc = k_ref[0, pl.ds(st, L), :]
        vc = v_ref[0, pl.ds(st, L), :]
        r = r_ref[0, 0, pl.ds(c, 1)].reshape(8, L)
        base = 0 if fwd else 3
        grow = r[base + 1:base + 2]
        bcol = to_col(r[base:base + 1])
        gcol = to_col(grow)
        mm = jnp.maximum(to_col(r[base + 2:base + 3]), m)
        e = L - 1 if fwd else 0
        mm_l = mm[e:e + 1]
        s = lax.dot_general(qc, kc, (((1,), (1,)), ((), ())), preferred_element_type=F32)
        a = jnp.exp(jnp.where(causal if fwd else anti, grow - mm, NEG_INF)) * s
        gint = jnp.exp(m - mm)
        cmat = c_scr[...]
        num = (jnp.dot(a.astype(BF16), vc, preferred_element_type=F32)
               + gint * jnp.dot(qc, cmat.astype(BF16), preferred_element_type=F32))
        qn = jnp.sum(qc.astype(F32) * n, axis=1, keepdims=True)
        den = jnp.sum(a, axis=1, keepdims=True) + gint * qn
        hc = num * (1.0 / jnp.maximum(jnp.abs(den), jnp.exp(-(bcol + mm))))
        decay = jnp.exp(m - mm_l)
        wk = jnp.exp(gcol - mm_l) * kc.astype(F32)
        c_scr[...] = decay * cmat + lax.dot_general(wk.astype(BF16), vc, (((0,), (0,)), ((), ())),
                                                    preferred_element_type=F32)
        n_new = decay * n + jnp.sum(wk, axis=0, keepdims=True)
        m_new = bcol[e:e + 1] + mm_l
        return hc, n_new, m_new

    def finish(c, hsum):
        st = pl.multiple_of(c * L, L)
        ms = jnp.mean(hsum * hsum, axis=-1, keepdims=True)
        o_ref[0, pl.ds(st, L), :] = (hsum * lax.rsqrt(ms + NORM_EPS) * gain).astype(o_ref.dtype)

    def step(j, carry, second_half):
        nf, mf, nb, mb = carry
        cf, cb = j, nc - 1 - j
        hf, nf, mf = chunk(cf, cf_scr, nf, mf, True)
        hb, nb, mb = chunk(cb, cb_scr, nb, mb, False)
        sf = pl.multiple_of(cf * L, L)
        sb = pl.multiple_of(cb * L, L)
        if second_half:
            finish(cf, hs_scr[pl.ds(sf, L), :] + hf)
            finish(cb, hs_scr[pl.ds(sb, L), :] + hb)
        else:
            hs_scr[pl.ds(sf, L), :] = hf
            hs_scr[pl.ds(sb, L), :] = hb
        return nf, mf, nb, mb

    cf_scr[...] = jnp.zeros_like(cf_scr)
    cb_scr[...] = jnp.zeros_like(cb_scr)
    n0 = jnp.zeros((1, M_QK_DIM), F32)
    m0 = jnp.zeros((1, 1), F32)
    carry = lax.fori_loop(0, nc // 2, functools.partial(step, second_half=False), (n0, m0, n0, m0))
    lax.fori_loop(nc // 2, nc, functools.partial(step, second_half=True), carry)


def _mlstm(proj1, prep, gain, bsz, seq):
    nc = seq // M_CHUNK
    assert nc % 2 == 0
    p3 = proj1.reshape(bsz, seq, M_MAIN_COLS)
    kb = M_QK_WIDTH // M_QK_DIM
    vb = 2 * M_QK_WIDTH // M_V_DIM
    blocks = (2 * _nbytes((seq, M_QK_DIM), BF16) + 2 * _nbytes((seq, M_V_DIM), BF16)
              + _nbytes((nc, 8, M_CHUNK), F32) + _nbytes((1, M_V_DIM), F32))
    scratch_bytes = _nbytes((seq, M_V_DIM), F32) + 2 * _nbytes((M_QK_DIM, M_V_DIM), F32)
    return pl.pallas_call(
        functools.partial(_mlstm_kernel, nc=nc),
        grid=(bsz, M_HEADS),
        in_specs=[
            pl.BlockSpec((1, seq, M_QK_DIM), lambda b, h: (b, 0, h)),
            pl.BlockSpec((1, seq, M_QK_DIM), lambda b, h: (b, 0, kb + h)),
            pl.BlockSpec((1, seq, M_V_DIM), lambda b, h: (b, 0, vb + h)),
            pl.BlockSpec((1, 1, nc, 8, M_CHUNK), lambda b, h: (b, h, 0, 0, 0)),
            pl.BlockSpec((1, 1, M_V_DIM), lambda b, h: (h, 0, 0)),
        ],
        out_specs=pl.BlockSpec((1, seq, M_V_DIM), lambda b, h: (b, 0, h)),
        out_shape=jax.ShapeDtypeStruct((bsz, seq, M_V_WIDTH), BF16),
        scratch_shapes=[pltpu.VMEM((seq, M_V_DIM), F32), pltpu.VMEM((M_QK_DIM, M_V_DIM), F32),
                        pltpu.VMEM((M_QK_DIM, M_V_DIM), F32)],
        compiler_params=pltpu.CompilerParams(
            dimension_semantics=("arbitrary", "arbitrary"),
            vmem_limit_bytes=_vmem_limit(blocks, scratch_bytes, 8 * 1024 * 1024),
        ),
        name="mlstm",
    )(p3, p3, p3, prep, gain.reshape(M_HEADS, 1, M_V_DIM).astype(F32))


def _gate_weights(w_in1, b_gate):
    wg = w_in1[:, M_MAIN_COLS:].reshape(D_MODEL, 4, M_HEADS).transpose(0, 2, 1)
    wg = jnp.pad(wg, ((0, 0), (0, 0), (0, 4))).reshape(D_MODEL, M_HEADS * 8)
    wg = jnp.pad(wg, ((0, 0), (0, GATE_LANES - M_HEADS * 8)))
    bg = jnp.pad(b_gate.astype(F32).reshape(4, M_HEADS).T, ((0, 0), (0, 4))).reshape(1, M_HEADS * 8)
    bg = jnp.pad(bg, ((0, 0), (0, GATE_LANES - M_HEADS * 8)))
    return wg.astype(BF16), bg


def _trunk(x, p):
    bsz, seq, d = x.shape
    t = bsz * seq
    assert seq % (2 * PERM_BLOCK) == 0
    x2d = x.reshape(t, d)
    h0 = _rmsnorm_orders(x2d, p["l0_norm_pre"])
    proj0 = _inproj0(h0, p["l0_w_in"], p["rope"], seq)
    outs, lses = zip(*[_attention_group(proj0, g, bsz, seq) for g in range(N_GROUPS)])
    x1, h1 = _outproj0(outs, lses, proj0, p["l0_w_out"], x2d, p["l0_norm_post"], p["l1_norm_pre"])
    proj1, gates = _inproj1(h1, p["l1_w_main"], p["l1_w_gate"])
    prep = _gate_prep(gates, p["l1_b_gate"], bsz, seq)
    hn = _mlstm(proj1, prep, p["l1_head_norm"], bsz, seq)
    y = _outproj1(hn.reshape(t, M_V_WIDTH), proj1, p["l1_w_out"], x1, p["l1_norm_post"])
    return y.reshape(bsz, seq, d)


def kernel(x_prompt, x_sample, l0_norm_pre, l0_w_in, l0_w_out, l0_norm_post,
           l1_norm_pre, l1_w_in, l1_b_gate, l1_head_norm, l1_w_out, l1_norm_post):
    assert x_prompt.shape[1] == x_sample.shape[1]
    w_gate, b_gate = _gate_weights(l1_w_in, l1_b_gate)
    p = {
        "l0_norm_pre": l0_norm_pre, "l0_w_in": l0_w_in.astype(BF16), "l0_w_out": l0_w_out.astype(BF16),
        "l0_norm_post": l0_norm_post, "l1_norm_pre": l1_norm_pre,
        "l1_w_main": l1_w_in[:, :M_MAIN_COLS].astype(BF16), "l1_w_gate": w_gate, "l1_b_gate": b_gate,
        "l1_head_norm": l1_head_norm, "l1_w_out": l1_w_out.astype(BF16), "l1_norm_post": l1_norm_post,
        "rope": _rope_tables(x_prompt.shape[1]),
    }
    return (_trunk(x_prompt, p), _trunk(x_sample, p))
```

```python
import functools

import jax
import jax.numpy as jnp
from jax import lax
from jax.experimental import pallas as pl
from jax.experimental.pallas import tpu as pltpu

F32 = jnp.float32
BF16 = jnp.bfloat16

D_MODEL = 2048
ATT_GROUPS = ((128, 1), (512, 4), (2048, 16))
N_GROUPS = len(ATT_GROUPS)
ATT_HEADS = 16
ATT_HEAD_DIM = 128
ATT_WIDTH = ATT_HEADS * ATT_HEAD_DIM
ATT_IN_COLS = 3 * N_GROUPS * ATT_WIDTH + ATT_WIDTH
ATT_HALF = 64
ROPE_THETA = 500000.0
ROPE_DIMS = ATT_HEAD_DIM // 4
ROPE_HALF = ROPE_DIMS // 2
LOG2E = 1.4426950408889634
ATT_Q_SCALE = ATT_HEAD_DIM ** -0.5 * LOG2E

M_HEADS = 8
M_QK_DIM = 256
M_V_DIM = 512
M_QK_WIDTH = M_HEADS * M_QK_DIM
M_V_WIDTH = M_HEADS * M_V_DIM
M_MAIN_COLS = 2 * M_QK_WIDTH + 3 * M_V_WIDTH
M_CHUNK = 128
GATE_LANES = 128

NORM_EPS = 1e-6
NEG_INF = -1e30

LANES = 128
VMEM_CAP_BYTES = 56 * 1024 * 1024


def _vmem_limit(block_bytes, scratch_bytes=0, temp_bytes=0):
    need = 2 * block_bytes + scratch_bytes + temp_bytes
    return int(min(max(need, 16 * 1024 * 1024), VMEM_CAP_BYTES))


def _nbytes(shape, dtype):
    n = 1
    for s in shape:
        n *= s
    return n * jnp.dtype(dtype).itemsize


PERM_BLOCK = 1024


def _rmsnorm_orders_kernel(x_ref, g_ref, o_ref, slab_scr):
    x = x_ref[...]
    inv = lax.rsqrt(jnp.mean(x * x, axis=-1, keepdims=True) + NORM_EPS)
    nslab = x_ref.shape[1] // LANES
    for c in range(nslab):
        ls = slice(c * LANES, (c + 1) * LANES)
        y = x_ref[:, ls] * inv * g_ref[:, ls]
        slab_scr[c] = y
        for g, (_, dil) in enumerate(ATT_GROUPS):
            if dil == 1:
                o_ref[g, :, ls] = y.astype(o_ref.dtype)
    for g, (_, dil) in enumerate(ATT_GROUPS):
        if dil == 1:
            continue
        n = PERM_BLOCK // dil
        for r in range(dil):
            for c in range(nslab):
                o_ref[g, r * n:(r + 1) * n, c * LANES:(c + 1) * LANES] = (
                    slab_scr[c, pl.ds(r, n, stride=dil), :].astype(o_ref.dtype))


def _rmsnorm_orders(x2d, gain):
    t, d = x2d.shape
    tm = PERM_BLOCK
    return pl.pallas_call(
        _rmsnorm_orders_kernel,
        grid=(t // tm,),
        in_specs=[pl.BlockSpec((tm, d), lambda i: (i, 0)), pl.BlockSpec((1, d), lambda i: (0, 0))],
        out_specs=pl.BlockSpec((N_GROUPS, tm, d), lambda i: (0, i, 0)),
        out_shape=jax.ShapeDtypeStruct((N_GROUPS, t, d), BF16),
        scratch_shapes=[pltpu.VMEM((d // LANES, tm, LANES), F32)],
        compiler_params=pltpu.CompilerParams(
            dimension_semantics=("arbitrary",),
            vmem_limit_bytes=_vmem_limit(_nbytes((tm, d), F32) + _nbytes((N_GROUPS, tm, d), BF16),
                                         _nbytes((tm, d), F32), _nbytes((tm, d), F32)),
        ),
        name="rmsnorm_orders",
    )(x2d, gain.reshape(1, d).astype(F32))


def _order_index(seq, dil):
    n = PERM_BLOCK // dil
    return jnp.arange(seq, dtype=jnp.int32).reshape(seq // PERM_BLOCK, n, dil).transpose(0, 2, 1).reshape(seq)


INPROJ0_TN = 2048
INPROJ0_CHUNK = 256
_GROUP_TILES = 3 * ATT_WIDTH // INPROJ0_TN
_Z_TILES = ATT_WIDTH // INPROJ0_TN


def _inproj0_col_tile(jj):
    return jnp.where(jj < _GROUP_TILES, jj,
                     jnp.where(jj < _GROUP_TILES + _Z_TILES, jj + (N_GROUPS - 1) * _GROUP_TILES, jj - _Z_TILES))


def _inproj0_order(jj):
    return jnp.where(jj < _GROUP_TILES + _Z_TILES, 0, (jj - _Z_TILES) // _GROUP_TILES)


def _inproj0_kernel(h_ref, w_ref, c_ref, s1_ref, s2_ref, o_ref):
    col = _inproj0_col_tile(pl.program_id(1))
    seg = (col * INPROJ0_TN) // ATT_WIDTH
    kind = seg % 3
    is_rope = jnp.logical_and(seg < 3 * N_GROUPS, kind < 2)
    nchunk = INPROJ0_TN // INPROJ0_CHUNK
    tm = o_ref.shape[0]

    def pieces():
        for cc in range(nchunk):
            cols = slice(cc * INPROJ0_CHUNK, (cc + 1) * INPROJ0_CHUNK)
            if cc < nchunk - 1:
                yield slice(0, tm), cols
            else:
                yield slice(0, tm // 2), cols
                yield slice(tm // 2, tm), cols

    @pl.when(is_rope)
    def _():
        scale = jnp.where(kind == 0, ATT_Q_SCALE, 1.0).astype(F32)
        c_all = c_ref[...] * scale
        s1_all = s1_ref[...] * scale
        s2_all = s2_ref[...] * scale
        for rows, cols in pieces():
            c, s1, s2 = c_all[rows], s1_all[rows], s2_all[rows]
            acc = jnp.dot(h_ref[rows, :], w_ref[:, cols], preferred_element_type=F32)
            for t in range(INPROJ0_CHUNK // LANES):
                a = acc[:, t * LANES:(t + 1) * LANES]
                r = a * c + pltpu.roll(a, LANES - ROPE_HALF, 1) * s1 + pltpu.roll(a, ROPE_HALF, 1) * s2
                lo = cols.start + t * LANES
                o_ref[rows, lo:lo + LANES] = r.astype(o_ref.dtype)

    @pl.when(jnp.logical_not(is_rope))
    def _():
        for rows, cols in pieces():
            o_ref[rows, cols] = jnp.dot(h_ref[rows, :], w_ref[:, cols], preferred_element_type=F32).astype(o_ref.dtype)


def _rope_tables(seq):
    inv = jnp.power(ROPE_THETA, -jnp.arange(ROPE_HALF, dtype=F32) / ROPE_HALF)
    ang = jnp.arange(seq, dtype=F32)[:, None] * inv[None, :]
    cos, sin = jnp.cos(ang), jnp.sin(ang)
    zeros = jnp.zeros((seq, LANES - ROPE_DIMS), F32)
    zh = jnp.zeros((seq, ROPE_HALF), F32)
    c = jnp.concatenate([cos, cos, jnp.ones((seq, LANES - ROPE_DIMS), F32)], axis=1)
    s1 = jnp.concatenate([-sin, zh, zeros], axis=1)
    s2 = jnp.concatenate([zh, sin, zeros], axis=1)
    orders = [_order_index(seq, dil) for _, dil in ATT_GROUPS]
    return tuple(jnp.stack([tab[idx] for idx in orders]) for tab in (c, s1, s2))


def _inproj0(h_orders, w, tables, seq):
    _, t, k = h_orders.shape
    n = w.shape[1]
    tm, tn = PERM_BLOCK, INPROJ0_TN
    pos_blocks = seq // tm
    tab_spec = pl.BlockSpec((None, tm, LANES), lambda i, j: (_inproj0_order(j), i % pos_blocks, 0))
    blocks = _nbytes((tm, k), BF16) + _nbytes((k, tn), BF16) + _nbytes((tm, tn), BF16) + 3 * _nbytes((tm, LANES), F32)
    return pl.pallas_call(
        _inproj0_kernel,
        grid=(t // tm, n // tn),
        in_specs=[
            pl.BlockSpec((None, tm, k), lambda i, j: (_inproj0_order(j), i, 0)),
            pl.BlockSpec((k, tn), lambda i, j: (0, _inproj0_col_tile(j))),
            tab_spec, tab_spec, tab_spec,
        ],
        out_specs=pl.BlockSpec((tm, tn), lambda i, j: (i, _inproj0_col_tile(j))),
        out_shape=jax.ShapeDtypeStruct((t, n), BF16),
        compiler_params=pltpu.CompilerParams(
            dimension_semantics=("arbitrary", "arbitrary"),
            vmem_limit_bytes=_vmem_limit(blocks, 0, _nbytes((tm, k), BF16) + 3 * _nbytes((tm, LANES), F32)
                                         + 4 * _nbytes((tm, INPROJ0_CHUNK), F32)),
        ),
        name="inproj0",
    )(h_orders, w, *tables)


ATT_TQ = 256


def _attn_kernel(q_ref, kp_ref, kc_ref, kn_ref, vp_ref, vc_ref, vn_ref, o_ref, l_ref, qx, kx, vx, ox, lx, *, tq, ls):
    hb = ATT_HALF
    sub = 2 * hb
    w = qx.shape[1]
    i = pl.program_id(2)
    qx[...] = q_ref[0].reshape(tq, w)
    kx[0:hb, :] = kp_ref[0].reshape(hb, w)
    kx[hb:hb + tq, :] = kc_ref[0].reshape(tq, w)
    kx[hb + tq:, :] = kn_ref[0].reshape(hb, w)
    vx[0:hb, :] = vp_ref[0].reshape(hb, w)
    vx[hb:hb + tq, :] = vc_ref[0].reshape(tq, w)
    vx[hb + tq:, :] = vn_ref[0].reshape(hb, w)

    row = lax.broadcasted_iota(jnp.int32, (sub, 2 * sub), 0)
    col = lax.broadcasted_iota(jnp.int32, (sub, 2 * sub), 1)
    band = jnp.abs(col - hb - row) <= hb
    lane = lax.broadcasted_iota(jnp.int32, (sub, LANES), 1)

    for a in range(tq // sub):
        r0 = a * sub
        kidx = i * tq + r0 + col - hb
        valid = band & (kidx >= 0) & (kidx < ls)
        lse_tile = jnp.zeros((sub, LANES), F32)
        for h in range(ATT_HEADS):
            hs = slice(h * ATT_HEAD_DIM, (h + 1) * ATT_HEAD_DIM)
            qh = qx[r0:r0 + sub, hs]
            kh = kx[r0:r0 + 2 * sub, hs]
            vh = vx[r0:r0 + 2 * sub, hs]
            s = lax.dot_general(qh, kh, (((1,), (1,)), ((), ())), preferred_element_type=F32)
            s = jnp.where(valid, s, NEG_INF)
            m = jnp.max(s, axis=-1, keepdims=True)
            p = jnp.exp2(s - m)
            den = jnp.sum(p, axis=-1, keepdims=True)
            o = jnp.dot(p.astype(BF16), vh, preferred_element_type=F32) * (1.0 / den)
            lse_tile = jnp.where(lane == h, m + jnp.log2(den), lse_tile)
            ox[r0:r0 + sub, hs] = o.astype(ox.dtype)
        lx[r0:r0 + sub, :] = lse_tile
    o_ref[0] = ox[...].reshape(o_ref.shape[1:])
    l_ref[0] = lx[...].reshape(l_ref.shape[1:])


def _attention_group(proj, g, bsz, seq):
    _, dil = ATT_GROUPS[g]
    w = ATT_WIDTH
    hb = ATT_HALF
    ls = seq // dil
    if dil == 1:
        nb, n = seq // ATT_TQ, ATT_TQ
    else:
        nb, n = seq // PERM_BLOCK, PERM_BLOCK // dil
    whole_class = n < ATT_TQ
    tq = ls if whole_class else ATT_TQ
    assert tq % (2 * hb) == 0 and n % hb == 0
    pv = proj.reshape(bsz, nb, dil, n, ATT_IN_COLS)
    ncb = 3 * g

    if whole_class:
        grid = (bsz, dil, 1)
        main = lambda cb: pl.BlockSpec((1, nb, None, n, w), lambda b, r, i: (b, 0, r, 0, cb))
        prev_h = next_h = lambda cb: pl.BlockSpec((1, 1, None, hb, w), lambda b, r, i: (b, 0, r, 0, cb))
        out_block = lambda width: pl.BlockSpec((1, nb, None, n, width), lambda b, r, i: (b, 0, r, 0, 0))
    else:
        grid = (bsz, dil, nb)
        main = lambda cb: pl.BlockSpec((1, 1, None, n, w), lambda b, r, i: (b, i, r, 0, cb))
        prev_h = lambda cb: pl.BlockSpec((1, 1, None, hb, w),
                                         lambda b, r, i: (b, jnp.maximum(i - 1, 0), r, n // hb - 1, cb))
        next_h = lambda cb: pl.BlockSpec((1, 1, None, hb, w),
                                         lambda b, r, i: (b, jnp.minimum(i + 1, nb - 1), r, 0, cb))
        out_block = lambda width: pl.BlockSpec((1, 1, None, n, width), lambda b, r, i: (b, i, r, 0, 0))

    in_specs = [main(ncb), prev_h(ncb + 1), main(ncb + 1), next_h(ncb + 1),
                prev_h(ncb + 2), main(ncb + 2), next_h(ncb + 2)]
    blocks = 4 * _nbytes((tq, w), BF16) + 4 * _nbytes((hb, w), BF16) + _nbytes((tq, LANES), F32)
    scratch = [pltpu.VMEM((tq, w), BF16), pltpu.VMEM((tq + 2 * hb, w), BF16), pltpu.VMEM((tq + 2 * hb, w), BF16),
               pltpu.VMEM((tq, w), BF16), pltpu.VMEM((tq, LANES), F32)]
    scratch_bytes = 4 * _nbytes((tq + 2 * hb, w), BF16) + _nbytes((tq, LANES), F32)
    o, l = pl.pallas_call(
        functools.partial(_attn_kernel, tq=tq, ls=ls),
        grid=grid,
        in_specs=in_specs,
        out_specs=(out_block(w), out_block(LANES)),
        out_shape=(jax.ShapeDtypeStruct((bsz, nb, dil, n, w), BF16),
                   jax.ShapeDtypeStruct((bsz, nb, dil, n, LANES), F32)),
        scratch_shapes=scratch,
        compiler_params=pltpu.CompilerParams(
            dimension_semantics=("arbitrary", "arbitrary", "arbitrary"),
            vmem_limit_bytes=_vmem_limit(blocks, scratch_bytes, 8 * 1024 * 1024),
        ),
        name=f"attn_g{g}",
    )(*([pv] * 7))
    if dil > 1:
        o = o.transpose(0, 1, 3, 2, 4)
        l = l.transpose(0, 1, 3, 2, 4)
    return o.reshape(bsz * seq, w), l.reshape(bsz * seq, LANES)


def _post_norm_residual(out, x_ref, gpost_ref, o_ref, gnext_ref=None, hnext_ref=None):
    ms = jnp.mean(out * out, axis=-1, keepdims=True)
    x1 = x_ref[...] + out * lax.rsqrt(ms + NORM_EPS) * gpost_ref[...]
    o_ref[...] = x1
    if hnext_ref is not None:
        ms1 = jnp.mean(x1 * x1, axis=-1, keepdims=True)
        hnext_ref[...] = (x1 * lax.rsqrt(ms1 + NORM_EPS) * gnext_ref[...]).astype(hnext_ref.dtype)


OUTPROJ_KCHUNK = 512


def _sigmoid(x):
    return 0.5 * jnp.tanh(0.5 * x) + 0.5


def _outproj0_kernel(o0_ref, o1_ref, o2_ref, l0_ref, l1_ref, l2_ref, z_ref, w_ref, x_ref, gpost_ref, gnext_ref,
                     x1_ref, hnext_ref):
    o_refs = (o0_ref, o1_ref, o2_ref)
    lses = [r[...] for r in (l0_ref, l1_ref, l2_ref)]
    lmax = jnp.maximum(jnp.maximum(lses[0], lses[1]), lses[2])
    es = [jnp.exp2(l - lmax) for l in lses]
    tot = es[0] + es[1] + es[2]
    wts = [e / tot for e in es]
    heads_per_chunk = OUTPROJ_KCHUNK // ATT_HEAD_DIM
    out = None
    for c in range(ATT_HEADS // heads_per_chunk):
        ys = []
        for h in range(c * heads_per_chunk, (c + 1) * heads_per_chunk):
            hs = slice(h * ATT_HEAD_DIM, (h + 1) * ATT_HEAD_DIM)
            o = wts[0][:, h:h + 1] * o_refs[0][:, hs].astype(F32)
            for g in range(1, N_GROUPS):
                o = o + wts[g][:, h:h + 1] * o_refs[g][:, hs].astype(F32)
            z = z_ref[:, hs].astype(F32)
            ys.append((o * (z * _sigmoid(z))).astype(BF16))
        part = jnp.dot(jnp.concatenate(ys, axis=1), w_ref[c * OUTPROJ_KCHUNK:(c + 1) * OUTPROJ_KCHUNK, :],
                       preferred_element_type=F32)
        out = part if out is None else out + part
    _post_norm_residual(out, x_ref, gpost_ref, x1_ref, gnext_ref, hnext_ref)


def _outproj0(outs, lses, proj0, w, x2d, g_post, g_next, tm=256):
    t, d = x2d.shape
    k = w.shape[0]
    row = lambda i: (i, 0)
    vec = pl.BlockSpec((1, d), lambda i: (0, 0))
    zb = 3 * N_GROUPS
    in_specs = ([pl.BlockSpec((tm, k), row)] * N_GROUPS + [pl.BlockSpec((tm, LANES), row)] * N_GROUPS
                + [pl.BlockSpec((tm, k), lambda i: (i, zb)),
                   pl.BlockSpec((k, d), lambda i: (0, 0), pipeline_mode=pl.Buffered(1)),
                   pl.BlockSpec((tm, d), row), vec, vec])
    blocks = ((N_GROUPS + 1) * _nbytes((tm, k), BF16) + N_GROUPS * _nbytes((tm, LANES), F32)
              + 2 * _nbytes((tm, d), F32) + _nbytes((tm, d), BF16))
    return pl.pallas_call(
        _outproj0_kernel,
        grid=(t // tm,),
        in_specs=in_specs,
        out_specs=(pl.BlockSpec((tm, d), row), pl.BlockSpec((tm, d), row)),
        out_shape=(jax.ShapeDtypeStruct((t, d), F32), jax.ShapeDtypeStruct((t, d), BF16)),
        compiler_params=pltpu.CompilerParams(
            dimension_semantics=("arbitrary",),
            vmem_limit_bytes=_vmem_limit(blocks, _nbytes((k, d), BF16), 4 * _nbytes((tm, d), F32)),
        ),
        name="outproj0",
    )(*outs, *lses, proj0, w, x2d, g_post.reshape(1, d).astype(F32), g_next.reshape(1, d).astype(F32))


def _outproj1_kernel(hn_ref, og_ref, z_ref, w_ref, x_ref, gpost_ref, o_ref):
    out = None
    for c in range(hn_ref.shape[1] // OUTPROJ_KCHUNK):
        ks = slice(c * OUTPROJ_KCHUNK, (c + 1) * OUTPROJ_KCHUNK)
        z = z_ref[:, ks].astype(F32)
        y = (_sigmoid(og_ref[:, ks].astype(F32)) * hn_ref[:, ks].astype(F32)) * (z * _sigmoid(z))
        part = jnp.dot(y.astype(BF16), w_ref[ks, :], preferred_element_type=F32)
        out = part if out is None else out + part
    _post_norm_residual(out, x_ref, gpost_ref, o_ref)


def _outproj1(hn, proj1, w, x2d, g_post, tm=256):
    t, d = x2d.shape
    k = w.shape[0]
    row = lambda i: (i, 0)
    ob = (2 * M_QK_WIDTH + M_V_WIDTH) // k
    blocks = 3 * _nbytes((tm, k), BF16) + 2 * _nbytes((tm, d), F32)
    return pl.pallas_call(
        _outproj1_kernel,
        grid=(t // tm,),
        in_specs=[pl.BlockSpec((tm, k), row), pl.BlockSpec((tm, k), lambda i: (i, ob)),
                  pl.BlockSpec((tm, k), lambda i: (i, ob + 1)),
                  pl.BlockSpec((k, d), lambda i: (0, 0), pipeline_mode=pl.Buffered(1)),
                  pl.BlockSpec((tm, d), row), pl.BlockSpec((1, d), lambda i: (0, 0))],
        out_specs=pl.BlockSpec((tm, d), row),
        out_shape=jax.ShapeDtypeStruct((t, d), F32),
        compiler_params=pltpu.CompilerParams(
            dimension_semantics=("arbitrary",),
            vmem_limit_bytes=_vmem_limit(blocks, _nbytes((k, d), BF16), 4 * _nbytes((tm, k), F32)),
        ),
        name="outproj1",
    )(hn, proj1, proj1, w, x2d, g_post.reshape(1, d).astype(F32))


INPROJ1_CHUNK = 512


def _inproj1_kernel(h_ref, w_ref, wg_ref, o_ref, g_ref):
    h = h_ref[...]
    for cc in range(o_ref.shape[1] // INPROJ1_CHUNK):
        cs = slice(cc * INPROJ1_CHUNK, (cc + 1) * INPROJ1_CHUNK)
        o_ref[:, cs] = jnp.dot(h, w_ref[:, cs], preferred_element_type=F32).astype(o_ref.dtype)

    @pl.when(pl.program_id(1) == 0)
    def _():
        g_ref[...] = jnp.dot(h_ref[...], wg_ref[...], preferred_element_type=F32)


def _inproj1(h, w, wg, tm=1024, tn=2048):
    t, k = h.shape
    n = M_MAIN_COLS
    tm = min(tm, t)
    blocks = (_nbytes((tm, k), BF16) + _nbytes((k, tn), BF16) + _nbytes((k, GATE_LANES), BF16)
              + _nbytes((tm, tn), BF16) + _nbytes((tm, GATE_LANES), F32))
    return pl.pallas_call(
        _inproj1_kernel,
        grid=(t // tm, n // tn),
        in_specs=[
            pl.BlockSpec((tm, k), lambda i, j: (i, 0)),
            pl.BlockSpec((k, tn), lambda i, j: (0, j)),
            pl.BlockSpec((k, GATE_LANES), lambda i, j: (0, 0)),
        ],
        out_specs=(pl.BlockSpec((tm, tn), lambda i, j: (i, j)), pl.BlockSpec((tm, GATE_LANES), lambda i, j: (i, 0))),
        out_shape=(jax.ShapeDtypeStruct((t, n), BF16), jax.ShapeDtypeStruct((t, GATE_LANES), F32)),
        compiler_params=pltpu.CompilerParams(
            dimension_semantics=("arbitrary", "arbitrary"),
            vmem_limit_bytes=_vmem_limit(blocks, 0, 4 * _nbytes((tm, INPROJ1_CHUNK), F32)),
        ),
        name="inproj1",
    )(h, w, wg)


def _scan_rows(x, rows, op, fill, reverse):
    n = x.shape[0]
    k = 1
    while k < n:
        if reverse:
            shifted = jnp.where(rows < n - k, pltpu.roll(x, n - k, 0), fill)
        else:
            shifted = jnp.where(rows >= k, pltpu.roll(x, k, 0), fill)
        x = op(x, shifted)
        k *= 2
    return x


GATE_PREP_CHUNKS = 4


def _gate_prep_kernel(g_ref, b_ref, o_ref):
    for ci in range(GATE_PREP_CHUNKS):
        _gate_prep_chunk(g_ref[0, ci * M_CHUNK:(ci + 1) * M_CHUNK, :] + b_ref[...], o_ref, ci)


def _gate_prep_chunk(g, o_ref, ci):
    rows = lax.broadcasted_iota(jnp.int32, g.shape, 0)
    kind = lax.broadcasted_iota(jnp.int32, g.shape, 1) % 8
    lf = jnp.minimum(g, 0.0) - jnp.log1p(jnp.exp(-jnp.abs(g)))
    csum = _scan_rows(lf, rows, jnp.add, 0.0, reverse=False)
    rsum = _scan_rows(lf, rows, jnp.add, 0.0, reverse=True)
    b_at_i = pltpu.roll(jnp.where(kind == 1, csum, rsum), LANES - 1, 1)
    grow = g - b_at_i
    cm_f = _scan_rows(grow, rows, jnp.maximum, NEG_INF, reverse=False)
    cm_b = _scan_rows(grow, rows, jnp.maximum, NEG_INF, reverse=True)
    y = jnp.where(kind == 0, b_at_i, 0.0)
    y = jnp.where(kind == 1, pltpu.roll(grow, 1, 1), y)
    y = jnp.where(kind == 2, pltpu.roll(cm_f, 2, 1), y)
    y = jnp.where(kind == 3, rsum, y)
    y = jnp.where(kind == 4, pltpu.roll(grow, 2, 1), y)
    y = jnp.where(kind == 5, pltpu.roll(cm_b, 3, 1), y)
    yt = y.T
    o_ref[0, :, ci] = yt[:M_HEADS * 8].reshape(M_HEADS, 8, g.shape[0])


def _gate_prep(gates, bias, bsz, seq):
    nc = seq // M_CHUNK
    per = GATE_PREP_CHUNKS
    return pl.pallas_call(
        _gate_prep_kernel,
        grid=(bsz, nc // per),
        in_specs=[pl.BlockSpec((1, per * M_CHUNK, GATE_LANES), lambda b, c: (b, c, 0)),
                  pl.BlockSpec((1, GATE_LANES), lambda b, c: (0, 0))],
        out_specs=pl.BlockSpec((1, M_HEADS, per, 8, M_CHUNK), lambda b, c: (b, 0, c, 0, 0)),
        out_shape=jax.ShapeDtypeStruct((bsz, M_HEADS, nc, 8, M_CHUNK), F32),
        compiler_params=pltpu.CompilerParams(dimension_semantics=("arbitrary", "arbitrary")),
        name="gate_prep",
    )(gates.reshape(bsz, seq, GATE_LANES), bias)


def _mlstm_kernel(q_ref, k_ref, v_ref, r_ref, gain_ref, o_ref, hs_scr, cf_scr, cb_scr, *, nc):
    L = M_CHUNK
    rows = lax.broadcasted_iota(jnp.int32, (L, L), 0)
    cols = lax.broadcasted_iota(jnp.int32, (L, L), 1)
    eye = rows == cols
    causal = cols <= rows
    anti = cols >= rows
    gain = gain_ref[0]

    def to_col(r):
        return jnp.sum(jnp.where(eye, r, 0.0), axis=1, keepdims=True)

    def chunk(c, c_scr, n, m, fwd):
        st = pl.multiple_of(c * L, L)
        qc = q_ref[0, pl.ds(st, L), :] * (M_QK_DIM ** -0.5)
        kc = k_ref[0, pl.ds(st, L), :]
        vc = v_ref[0, pl.ds(st, L), :]
        r = r_ref[0, 0, pl.ds(c, 1)].reshape(8, L)
        base = 0 if fwd else 3
        grow = r[base + 1:base + 2]
        bcol = to_col(r[base:base + 1])
        gcol = to_col(grow)
        mm = jnp.maximum(to_col(r[base + 2:base + 3]), m)
        e = L - 1 if fwd else 0
        mm_l = mm[e:e + 1]
        s = lax.dot_general(qc, kc, (((1,), (1,)), ((), ())), preferred_element_type=F32)
        a = jnp.exp(jnp.where(causal if fwd else anti, grow - mm, NEG_INF)) * s
        gint = jnp.exp(m - mm)
        cmat = c_scr[...]
        num = (jnp.dot(a.astype(BF16), vc, preferred_element_type=F32)
               + gint * jnp.dot(qc, cmat.astype(BF16), preferred_element_type=F32))
        qn = jnp.sum(qc.astype(F32) * n, axis=1, keepdims=True)
        den = jnp.sum(a, axis=1, keepdims=True) + gint * qn
        hc = num * (1.0 / jnp.maximum(jnp.abs(den), jnp.exp(-(bcol + mm))))
        decay = jnp.exp(m - mm_l)
        wk = jnp.exp(gcol - mm_l) * kc.astype(F32)
        c_scr[...] = decay * cmat + lax.dot_general(wk.astype(BF16), vc, (((0,), (0,)), ((), ())),
                                                    preferred_element_type=F32)
        n_new = decay * n + jnp.sum(wk, axis=0, keepdims=True)
        m_new = bcol[e:e + 1] + mm_l
        return hc, n_new, m_new

    def finish(c, hsum):
        st = pl.multiple_of(c * L, L)
        ms = jnp.mean(hsum * hsum, axis=-1, keepdims=True)
        o_ref[0, pl.ds(st, L), :] = (hsum * lax.rsqrt(ms + NORM_EPS) * gain).astype(o_ref.dtype)

    def step(j, carry, second_half):
        nf, mf, nb, mb = carry
        cf, cb = j, nc - 1 - j
        hf, nf, mf = chunk(cf, cf_scr, nf, mf, True)
        hb, nb, mb = chunk(cb, cb_scr, nb, mb, False)
        sf = pl.multiple_of(cf * L, L)
        sb = pl.multiple_of(cb * L, L)
        if second_half:
            finish(cf, hs_scr[pl.ds(sf, L), :] + hf)
            finish(cb, hs_scr[pl.ds(sb, L), :] + hb)
        else:
            hs_scr[pl.ds(sf, L), :] = hf
            hs_scr[pl.ds(sb, L), :] = hb
        return nf, mf, nb, mb

    cf_scr[...] = jnp.zeros_like(cf_scr)
    cb_scr[...] = jnp.zeros_like(cb_scr)
    n0 = jnp.zeros((1, M_QK_DIM), F32)
    m0 = jnp.zeros((1, 1), F32)
    carry = lax.fori_loop(0, nc // 2, functools.partial(step, second_half=False), (n0, m0, n0, m0), unroll=4)
    lax.fori_loop(nc // 2, nc, functools.partial(step, second_half=True), carry, unroll=4)


def _mlstm(proj1, prep, gain, bsz, seq):
    nc = seq // M_CHUNK
    assert nc % 2 == 0
    p3 = proj1.reshape(bsz, seq, M_MAIN_COLS)
    kb = M_QK_WIDTH // M_QK_DIM
    vb = 2 * M_QK_WIDTH // M_V_DIM
    blocks = (2 * _nbytes((seq, M_QK_DIM), BF16) + 2 * _nbytes((seq, M_V_DIM), BF16)
              + _nbytes((nc, 8, M_CHUNK), F32) + _nbytes((1, M_V_DIM), F32))
    scratch_bytes = _nbytes((seq, M_V_DIM), F32) + 2 * _nbytes((M_QK_DIM, M_V_DIM), F32)
    return pl.pallas_call(
        functools.partial(_mlstm_kernel, nc=nc),
        grid=(bsz, M_HEADS),
        in_specs=[
            pl.BlockSpec((1, seq, M_QK_DIM), lambda b, h: (b, 0, h)),
            pl.BlockSpec((1, seq, M_QK_DIM), lambda b, h: (b, 0, kb + h)),
            pl.BlockSpec((1, seq, M_V_DIM), lambda b, h: (b, 0, vb + h)),
            pl.BlockSpec((1, 1, nc, 8, M_CHUNK), lambda b, h: (b, h, 0, 0, 0)),
            pl.BlockSpec((1, 1, M_V_DIM), lambda b, h: (h, 0, 0)),
        ],
        out_specs=pl.BlockSpec((1, seq, M_V_DIM), lambda b, h: (b, 0, h)),
        out_shape=jax.ShapeDtypeStruct((bsz, seq, M_V_WIDTH), BF16),
        scratch_shapes=[pltpu.VMEM((seq, M_V_DIM), F32), pltpu.VMEM((M_QK_DIM, M_V_DIM), F32),
                        pltpu.VMEM((M_QK_DIM, M_V_DIM), F32)],
        compiler_params=pltpu.CompilerParams(
            dimension_semantics=("arbitrary", "arbitrary"),
            vmem_limit_bytes=_vmem_limit(blocks, scratch_bytes, 8 * 1024 * 1024),
        ),
        name="mlstm",
    )(p3, p3, p3, prep, gain.reshape(M_HEADS, 1, M_V_DIM).astype(F32))


def _gate_weights(w_in1, b_gate):
    wg = w_in1[:, M_MAIN_COLS:].reshape(D_MODEL, 4, M_HEADS).transpose(0, 2, 1)
    wg = jnp.pad(wg, ((0, 0), (0, 0), (0, 4))).reshape(D_MODEL, M_HEADS * 8)
    wg = jnp.pad(wg, ((0, 0), (0, GATE_LANES - M_HEADS * 8)))
    bg = jnp.pad(b_gate.astype(F32).reshape(4, M_HEADS).T, ((0, 0), (0, 4))).reshape(1, M_HEADS * 8)
    bg = jnp.pad(bg, ((0, 0), (0, GATE_LANES - M_HEADS * 8)))
    return wg.astype(BF16), bg


def _trunk(x, p):
    bsz, seq, d = x.shape
    t = bsz * seq
    assert seq % (2 * PERM_BLOCK) == 0
    x2d = x.reshape(t, d)
    h0 = _rmsnorm_orders(x2d, p["l0_norm_pre"])
    proj0 = _inproj0(h0, p["l0_w_in"], p["rope"], seq)
    outs, lses = zip(*[_attention_group(proj0, g, bsz, seq) for g in range(N_GROUPS)])
    x1, h1 = _outproj0(outs, lses, proj0, p["l0_w_out"], x2d, p["l0_norm_post"], p["l1_norm_pre"])
    proj1, gates = _inproj1(h1, p["l1_w_main"], p["l1_w_gate"])
    prep = _gate_prep(gates, p["l1_b_gate"], bsz, seq)
    hn = _mlstm(proj1, prep, p["l1_head_norm"], bsz, seq)
    y = _outproj1(hn.reshape(t, M_V_WIDTH), proj1, p["l1_w_out"], x1, p["l1_norm_post"])
    return y.reshape(bsz, seq, d)


def kernel(x_prompt, x_sample, l0_norm_pre, l0_w_in, l0_w_out, l0_norm_post,
           l1_norm_pre, l1_w_in, l1_b_gate, l1_head_norm, l1_w_out, l1_norm_post):
    assert x_prompt.shape[1] == x_sample.shape[1]
    w_gate, b_gate = _gate_weights(l1_w_in, l1_b_gate)
    p = {
        "l0_norm_pre": l0_norm_pre, "l0_w_in": l0_w_in.astype(BF16), "l0_w_out": l0_w_out.astype(BF16),
        "l0_norm_post": l0_norm_post, "l1_norm_pre": l1_norm_pre,
        "l1_w_main": l1_w_in.astype(BF16), "l1_w_gate": w_gate, "l1_b_gate": b_gate,
        "l1_head_norm": l1_head_norm, "l1_w_out": l1_w_out.astype(BF16), "l1_norm_post": l1_norm_post,
        "rope": _rope_tables(x_prompt.shape[1]),
    }
    return (_trunk(x_prompt, p), _trunk(x_sample, p))
```

```python
import functools

import jax
import jax.numpy as jnp
from jax import lax
from jax.experimental import pallas as pl
from jax.experimental.pallas import tpu as pltpu

F32 = jnp.float32
BF16 = jnp.bfloat16

D_MODEL = 2048
ATT_GROUPS = ((128, 1), (512, 4), (2048, 16))
N_GROUPS = len(ATT_GROUPS)
ATT_HEADS = 16
ATT_HEAD_DIM = 128
ATT_WIDTH = ATT_HEADS * ATT_HEAD_DIM
ATT_IN_COLS = 3 * N_GROUPS * ATT_WIDTH + ATT_WIDTH
ATT_HALF = 64
ROPE_THETA = 500000.0
ROPE_DIMS = ATT_HEAD_DIM // 4
ROPE_HALF = ROPE_DIMS // 2
LOG2E = 1.4426950408889634
ATT_Q_SCALE = ATT_HEAD_DIM ** -0.5 * LOG2E

M_HEADS = 8
M_QK_DIM = 256
M_V_DIM = 512
M_QK_WIDTH = M_HEADS * M_QK_DIM
M_V_WIDTH = M_HEADS * M_V_DIM
M_MAIN_COLS = 2 * M_QK_WIDTH + 3 * M_V_WIDTH
M_CHUNK = 128
GATE_LANES = 128

NORM_EPS = 1e-6
NEG_INF = -1e30

LANES = 128
VMEM_CAP_BYTES = 56 * 1024 * 1024


def _vmem_limit(block_bytes, scratch_bytes=0, temp_bytes=0):
    need = 2 * block_bytes + scratch_bytes + temp_bytes
    return int(min(max(need, 16 * 1024 * 1024), VMEM_CAP_BYTES))


def _nbytes(shape, dtype):
    n = 1
    for s in shape:
        n *= s
    return n * jnp.dtype(dtype).itemsize


PERM_BLOCK = 1024


def _rmsnorm_orders_kernel(x_ref, g_ref, o_ref, slab_scr):
    x = x_ref[...]
    inv = lax.rsqrt(jnp.mean(x * x, axis=-1, keepdims=True) + NORM_EPS)
    nslab = x_ref.shape[1] // LANES
    for c in range(nslab):
        ls = slice(c * LANES, (c + 1) * LANES)
        y = x_ref[:, ls] * inv * g_ref[:, ls]
        slab_scr[c] = y
        for g, (_, dil) in enumerate(ATT_GROUPS):
            if dil == 1:
                o_ref[g, :, ls] = y.astype(o_ref.dtype)
    for g, (_, dil) in enumerate(ATT_GROUPS):
        if dil == 1:
            continue
        n = PERM_BLOCK // dil
        for r in range(dil):
            for c in range(nslab):
                o_ref[g, r * n:(r + 1) * n, c * LANES:(c + 1) * LANES] = (
                    slab_scr[c, pl.ds(r, n, stride=dil), :].astype(o_ref.dtype))


def _rmsnorm_orders(x2d, gain):
    t, d = x2d.shape
    tm = PERM_BLOCK
    return pl.pallas_call(
        _rmsnorm_orders_kernel,
        grid=(t // tm,),
        in_specs=[pl.BlockSpec((tm, d), lambda i: (i, 0)), pl.BlockSpec((1, d), lambda i: (0, 0))],
        out_specs=pl.BlockSpec((N_GROUPS, tm, d), lambda i: (0, i, 0)),
        out_shape=jax.ShapeDtypeStruct((N_GROUPS, t, d), BF16),
        scratch_shapes=[pltpu.VMEM((d // LANES, tm, LANES), F32)],
        compiler_params=pltpu.CompilerParams(
            dimension_semantics=("arbitrary",),
            vmem_limit_bytes=_vmem_limit(_nbytes((tm, d), F32) + _nbytes((N_GROUPS, tm, d), BF16),
                                         _nbytes((tm, d), F32), _nbytes((tm, d), F32)),
        ),
        name="rmsnorm_orders",
    )(x2d, gain.reshape(1, d).astype(F32))


def _order_index(seq, dil):
    n = PERM_BLOCK // dil
    return jnp.arange(seq, dtype=jnp.int32).reshape(seq // PERM_BLOCK, n, dil).transpose(0, 2, 1).reshape(seq)


INPROJ0_TN = 2048
INPROJ0_CHUNK = 256
INPROJ0_TAIL_ROWS = ((0, 512), (512, 768), (768, 1024))
_GROUP_TILES = 3 * ATT_WIDTH // INPROJ0_TN
_Z_TILES = ATT_WIDTH // INPROJ0_TN


def _inproj0_col_tile(jj):
    return jnp.where(jj < _GROUP_TILES, jj,
                     jnp.where(jj < _GROUP_TILES + _Z_TILES, jj + (N_GROUPS - 1) * _GROUP_TILES, jj - _Z_TILES))


def _inproj0_order(jj):
    return jnp.where(jj < _GROUP_TILES + _Z_TILES, 0, (jj - _Z_TILES) // _GROUP_TILES)


def _inproj0_kernel(h_ref, w_ref, c_ref, s1_ref, s2_ref, o_ref):
    col = _inproj0_col_tile(pl.program_id(1))
    seg = (col * INPROJ0_TN) // ATT_WIDTH
    kind = seg % 3
    is_rope = jnp.logical_and(seg < 3 * N_GROUPS, kind < 2)
    nchunk = INPROJ0_TN // INPROJ0_CHUNK
    tm = o_ref.shape[0]

    def pieces():
        for cc in range(nchunk):
            cols = slice(cc * INPROJ0_CHUNK, (cc + 1) * INPROJ0_CHUNK)
            if cc < nchunk - 1:
                yield slice(0, tm), cols
            else:
                for lo, hi in INPROJ0_TAIL_ROWS:
                    yield slice(lo, hi), cols

    @pl.when(is_rope)
    def _():
        scale = jnp.where(kind == 0, ATT_Q_SCALE, 1.0).astype(F32)
        c_all = c_ref[...] * scale
        s1_all = s1_ref[...] * scale
        s2_all = s2_ref[...] * scale
        for rows, cols in pieces():
            c, s1, s2 = c_all[rows], s1_all[rows], s2_all[rows]
            acc = jnp.dot(h_ref[rows, :], w_ref[:, cols], preferred_element_type=F32)
            for t in range(INPROJ0_CHUNK // LANES):
                a = acc[:, t * LANES:(t + 1) * LANES]
                r = a * c + pltpu.roll(a, LANES - ROPE_HALF, 1) * s1 + pltpu.roll(a, ROPE_HALF, 1) * s2
                lo = cols.start + t * LANES
                o_ref[rows, lo:lo + LANES] = r.astype(o_ref.dtype)

    @pl.when(jnp.logical_not(is_rope))
    def _():
        for rows, cols in pieces():
            o_ref[rows, cols] = jnp.dot(h_ref[rows, :], w_ref[:, cols], preferred_element_type=F32).astype(o_ref.dtype)


def _rope_tables(seq):
    inv = jnp.power(ROPE_THETA, -jnp.arange(ROPE_HALF, dtype=F32) / ROPE_HALF)
    ang = jnp.arange(seq, dtype=F32)[:, None] * inv[None, :]
    cos, sin = jnp.cos(ang), jnp.sin(ang)
    zeros = jnp.zeros((seq, LANES - ROPE_DIMS), F32)
    zh = jnp.zeros((seq, ROPE_HALF), F32)
    c = jnp.concatenate([cos, cos, jnp.ones((seq, LANES - ROPE_DIMS), F32)], axis=1)
    s1 = jnp.concatenate([-sin, zh, zeros], axis=1)
    s2 = jnp.concatenate([zh, sin, zeros], axis=1)
    orders = [_order_index(seq, dil) for _, dil in ATT_GROUPS]
    return tuple(jnp.stack([tab[idx] for idx in orders]) for tab in (c, s1, s2))


def _inproj0(h_orders, w, tables, seq):
    _, t, k = h_orders.shape
    n = w.shape[1]
    tm, tn = PERM_BLOCK, INPROJ0_TN
    pos_blocks = seq // tm
    tab_spec = pl.BlockSpec((None, tm, LANES), lambda i, j: (_inproj0_order(j), i % pos_blocks, 0))
    blocks = _nbytes((tm, k), BF16) + _nbytes((k, tn), BF16) + _nbytes((tm, tn), BF16) + 3 * _nbytes((tm, LANES), F32)
    return pl.pallas_call(
        _inproj0_kernel,
        grid=(t // tm, n // tn),
        in_specs=[
            pl.BlockSpec((None, tm, k), lambda i, j: (_inproj0_order(j), i, 0)),
            pl.BlockSpec((k, tn), lambda i, j: (0, _inproj0_col_tile(j))),
            tab_spec, tab_spec, tab_spec,
        ],
        out_specs=pl.BlockSpec((tm, tn), lambda i, j: (i, _inproj0_col_tile(j))),
        out_shape=jax.ShapeDtypeStruct((t, n), BF16),
        compiler_params=pltpu.CompilerParams(
            dimension_semantics=("arbitrary", "arbitrary"),
            vmem_limit_bytes=_vmem_limit(blocks, 0, _nbytes((tm, k), BF16) + 3 * _nbytes((tm, LANES), F32)
                                         + 4 * _nbytes((tm, INPROJ0_CHUNK), F32)),
        ),
        name="inproj0",
    )(h_orders, w, *tables)


ATT_TQ = 512


def _attn_kernel(q_ref, kp_ref, kc_ref, kn_ref, vp_ref, vc_ref, vn_ref, o_ref, l_ref, qx, kx, vx, ox, lx, *, tq, ls):
    hb = ATT_HALF
    sub = 2 * hb
    w = qx.shape[1]
    i = pl.program_id(2)
    qx[...] = q_ref[0].reshape(tq, w)
    kx[0:hb, :] = kp_ref[0].reshape(hb, w)
    kx[hb:hb + tq, :] = kc_ref[0].reshape(tq, w)
    kx[hb + tq:, :] = kn_ref[0].reshape(hb, w)
    vx[0:hb, :] = vp_ref[0].reshape(hb, w)
    vx[hb:hb + tq, :] = vc_ref[0].reshape(tq, w)
    vx[hb + tq:, :] = vn_ref[0].reshape(hb, w)

    row = lax.broadcasted_iota(jnp.int32, (sub, 2 * sub), 0)
    col = lax.broadcasted_iota(jnp.int32, (sub, 2 * sub), 1)
    band = jnp.abs(col - hb - row) <= hb
    lane = lax.broadcasted_iota(jnp.int32, (sub, LANES), 1)

    for a in range(tq // sub):
        r0 = a * sub
        kidx = i * tq + r0 + col - hb
        valid = band & (kidx >= 0) & (kidx < ls)
        lse_tile = jnp.zeros((sub, LANES), F32)
        for h in range(ATT_HEADS):
            hs = slice(h * ATT_HEAD_DIM, (h + 1) * ATT_HEAD_DIM)
            qh = qx[r0:r0 + sub, hs]
            kh = kx[r0:r0 + 2 * sub, hs]
            vh = vx[r0:r0 + 2 * sub, hs]
            s = lax.dot_general(qh, kh, (((1,), (1,)), ((), ())), preferred_element_type=F32)
            s = jnp.where(valid, s, NEG_INF)
            m = jnp.max(s, axis=-1, keepdims=True)
            p = jnp.exp2(s - m)
            den = jnp.sum(p, axis=-1, keepdims=True)
            o = jnp.dot(p.astype(BF16), vh, preferred_element_type=F32) * (1.0 / den)
            lse_tile = jnp.where(lane == h, m + jnp.log2(den), lse_tile)
            ox[r0:r0 + sub, hs] = o.astype(ox.dtype)
        lx[r0:r0 + sub, :] = lse_tile
    o_ref[0] = ox[...].reshape(o_ref.shape[1:])
    l_ref[0] = lx[...].reshape(l_ref.shape[1:])


def _attention_group(proj, g, bsz, seq):
    _, dil = ATT_GROUPS[g]
    w = ATT_WIDTH
    hb = ATT_HALF
    ls = seq // dil
    if dil == 1:
        n = min(ATT_TQ, 256)
        nb = seq // n
    else:
        nb, n = seq // PERM_BLOCK, PERM_BLOCK // dil
    tq = min(ATT_TQ, ls)
    bpt = tq // n
    assert tq % (2 * hb) == 0 and n % hb == 0 and tq % n == 0 and nb % bpt == 0
    pv = proj.reshape(bsz, nb, dil, n, ATT_IN_COLS)
    ncb = 3 * g

    grid = (bsz, dil, nb // bpt)
    main = lambda cb: pl.BlockSpec((1, bpt, None, n, w), lambda b, r, i: (b, i, r, 0, cb))
    prev_h = lambda cb: pl.BlockSpec((1, 1, None, hb, w),
                                     lambda b, r, i: (b, jnp.maximum(i * bpt - 1, 0), r, n // hb - 1, cb))
    next_h = lambda cb: pl.BlockSpec((1, 1, None, hb, w),
                                     lambda b, r, i: (b, jnp.minimum((i + 1) * bpt, nb - 1), r, 0, cb))
    out_block = lambda width: pl.BlockSpec((1, bpt, None, n, width), lambda b, r, i: (b, i, r, 0, 0))

    in_specs = [main(ncb), prev_h(ncb + 1), main(ncb + 1), next_h(ncb + 1),
                prev_h(ncb + 2), main(ncb + 2), next_h(ncb + 2)]
    blocks = 4 * _nbytes((tq, w), BF16) + 4 * _nbytes((hb, w), BF16) + _nbytes((tq, LANES), F32)
    scratch = [pltpu.VMEM((tq, w), BF16), pltpu.VMEM((tq + 2 * hb, w), BF16), pltpu.VMEM((tq + 2 * hb, w), BF16),
               pltpu.VMEM((tq, w), BF16), pltpu.VMEM((tq, LANES), F32)]
    scratch_bytes = 4 * _nbytes((tq + 2 * hb, w), BF16) + _nbytes((tq, LANES), F32)
    o, l = pl.pallas_call(
        functools.partial(_attn_kernel, tq=tq, ls=ls),
        grid=grid,
        in_specs=in_specs,
        out_specs=(out_block(w), out_block(LANES)),
        out_shape=(jax.ShapeDtypeStruct((bsz, nb, dil, n, w), BF16),
                   jax.ShapeDtypeStruct((bsz, nb, dil, n, LANES), F32)),
        scratch_shapes=scratch,
        compiler_params=pltpu.CompilerParams(
            dimension_semantics=("arbitrary", "arbitrary", "arbitrary"),
            vmem_limit_bytes=_vmem_limit(blocks, scratch_bytes, 8 * 1024 * 1024),
        ),
        name=f"attn_g{g}",
    )(*([pv] * 7))
    if dil > 1:
        o = o.transpose(0, 1, 3, 2, 4)
        l = l.transpose(0, 1, 3, 2, 4)
    return o.reshape(bsz * seq, w), l.reshape(bsz * seq, LANES)


def _post_norm_residual(out, x_ref, gpost_ref, o_ref, gnext_ref=None, hnext_ref=None):
    ms = jnp.mean(out * out, axis=-1, keepdims=True)
    x1 = x_ref[...] + out * lax.rsqrt(ms + NORM_EPS) * gpost_ref[...]
    o_ref[...] = x1
    if hnext_ref is not None:
        ms1 = jnp.mean(x1 * x1, axis=-1, keepdims=True)
        hnext_ref[...] = (x1 * lax.rsqrt(ms1 + NORM_EPS) * gnext_ref[...]).astype(hnext_ref.dtype)


OUTPROJ_KCHUNK = 512


def _sigmoid(x):
    return 0.5 * jnp.tanh(0.5 * x) + 0.5


def _outproj0_kernel(o0_ref, o1_ref, o2_ref, l0_ref, l1_ref, l2_ref, z_ref, w_ref, x_ref, gpost_ref, gnext_ref,
                     x1_ref, hnext_ref):
    o_refs = (o0_ref, o1_ref, o2_ref)
    lses = [r[...] for r in (l0_ref, l1_ref, l2_ref)]
    lmax = jnp.maximum(jnp.maximum(lses[0], lses[1]), lses[2])
    es = [jnp.exp2(l - lmax) for l in lses]
    tot = es[0] + es[1] + es[2]
    wts = [e / tot for e in es]
    heads_per_chunk = OUTPROJ_KCHUNK // ATT_HEAD_DIM
    out = None
    for c in range(ATT_HEADS // heads_per_chunk):
        ys = []
        for h in range(c * heads_per_chunk, (c + 1) * heads_per_chunk):
            hs = slice(h * ATT_HEAD_DIM, (h + 1) * ATT_HEAD_DIM)
            o = wts[0][:, h:h + 1] * o_refs[0][:, hs].astype(F32)
            for g in range(1, N_GROUPS):
                o = o + wts[g][:, h:h + 1] * o_refs[g][:, hs].astype(F32)
            z = z_ref[:, hs].astype(F32)
            ys.append((o * (z * _sigmoid(z))).astype(BF16))
        part = jnp.dot(jnp.concatenate(ys, axis=1), w_ref[c * OUTPROJ_KCHUNK:(c + 1) * OUTPROJ_KCHUNK, :],
                       preferred_element_type=F32)
        out = part if out is None else out + part
    _post_norm_residual(out, x_ref, gpost_ref, x1_ref, gnext_ref, hnext_ref)


def _outproj0(outs, lses, proj0, w, x2d, g_post, g_next, tm=256):
    t, d = x2d.shape
    k = w.shape[0]
    row = lambda i: (i, 0)
    vec = pl.BlockSpec((1, d), lambda i: (0, 0))
    zb = 3 * N_GROUPS
    in_specs = ([pl.BlockSpec((tm, k), row)] * N_GROUPS + [pl.BlockSpec((tm, LANES), row)] * N_GROUPS
                + [pl.BlockSpec((tm, k), lambda i: (i, zb)),
                   pl.BlockSpec((k, d), lambda i: (0, 0), pipeline_mode=pl.Buffered(1)),
                   pl.BlockSpec((tm, d), row), vec, vec])
    blocks = ((N_GROUPS + 1) * _nbytes((tm, k), BF16) + N_GROUPS * _nbytes((tm, LANES), F32)
              + 2 * _nbytes((tm, d), F32) + _nbytes((tm, d), BF16))
    return pl.pallas_call(
        _outproj0_kernel,
        grid=(t // tm,),
        in_specs=in_specs,
        out_specs=(pl.BlockSpec((tm, d), row), pl.BlockSpec((tm, d), row)),
        out_shape=(jax.ShapeDtypeStruct((t, d), F32), jax.ShapeDtypeStruct((t, d), BF16)),
        compiler_params=pltpu.CompilerParams(
            dimension_semantics=("arbitrary",),
            vmem_limit_bytes=_vmem_limit(blocks, _nbytes((k, d), BF16), 4 * _nbytes((tm, d), F32)),
        ),
        name="outproj0",
    )(*outs, *lses, proj0, w, x2d, g_post.reshape(1, d).astype(F32), g_next.reshape(1, d).astype(F32))


def _outproj1_kernel(hn_ref, og_ref, z_ref, w_ref, x_ref, gpost_ref, o_ref):
    out = None
    for c in range(hn_ref.shape[1] // OUTPROJ_KCHUNK):
        ks = slice(c * OUTPROJ_KCHUNK, (c + 1) * OUTPROJ_KCHUNK)
        z = z_ref[:, ks].astype(F32)
        y = (_sigmoid(og_ref[:, ks].astype(F32)) * hn_ref[:, ks].astype(F32)) * (z * _sigmoid(z))
        part = jnp.dot(y.astype(BF16), w_ref[ks, :], preferred_element_type=F32)
        out = part if out is None else out + part
    _post_norm_residual(out, x_ref, gpost_ref, o_ref)


def _outproj1(hn, proj1, w, x2d, g_post, tm=256):
    t, d = x2d.shape
    k = w.shape[0]
    row = lambda i: (i, 0)
    ob = (2 * M_QK_WIDTH + M_V_WIDTH) // k
    blocks = 3 * _nbytes((tm, k), BF16) + 2 * _nbytes((tm, d), F32)
    return pl.pallas_call(
        _outproj1_kernel,
        grid=(t // tm,),
        in_specs=[pl.BlockSpec((tm, k), row), pl.BlockSpec((tm, k), lambda i: (i, ob)),
                  pl.BlockSpec((tm, k), lambda i: (i, ob + 1)),
                  pl.BlockSpec((k, d), lambda i: (0, 0), pipeline_mode=pl.Buffered(1)),
                  pl.BlockSpec((tm, d), row), pl.BlockSpec((1, d), lambda i: (0, 0))],
        out_specs=pl.BlockSpec((tm, d), row),
        out_shape=jax.ShapeDtypeStruct((t, d), F32),
        compiler_params=pltpu.CompilerParams(
            dimension_semantics=("arbitrary",),
            vmem_limit_bytes=_vmem_limit(blocks, _nbytes((k, d), BF16), 4 * _nbytes((tm, k), F32)),
        ),
        name="outproj1",
    )(hn, proj1, proj1, w, x2d, g_post.reshape(1, d).astype(F32))


INPROJ1_CHUNK = 512


def _inproj1_kernel(h_ref, w_ref, wg_ref, o_ref, g_ref):
    h = h_ref[...]
    for cc in range(o_ref.shape[1] // INPROJ1_CHUNK):
        cs = slice(cc * INPROJ1_CHUNK, (cc + 1) * INPROJ1_CHUNK)
        o_ref[:, cs] = jnp.dot(h, w_ref[:, cs], preferred_element_type=F32).astype(o_ref.dtype)

    @pl.when(pl.program_id(1) == 0)
    def _():
        g_ref[...] = jnp.dot(h_ref[...], wg_ref[...], preferred_element_type=F32)


def _inproj1(h, w, wg, tm=1024, tn=2048):
    t, k = h.shape
    n = M_MAIN_COLS
    tm = min(tm, t)
    blocks = (_nbytes((tm, k), BF16) + _nbytes((k, tn), BF16) + _nbytes((k, GATE_LANES), BF16)
              + _nbytes((tm, tn), BF16) + _nbytes((tm, GATE_LANES), F32))
    return pl.pallas_call(
        _inproj1_kernel,
        grid=(t // tm, n // tn),
        in_specs=[
            pl.BlockSpec((tm, k), lambda i, j: (i, 0)),
            pl.BlockSpec((k, tn), lambda i, j: (0, j)),
            pl.BlockSpec((k, GATE_LANES), lambda i, j: (0, 0)),
        ],
        out_specs=(pl.BlockSpec((tm, tn), lambda i, j: (i, j)), pl.BlockSpec((tm, GATE_LANES), lambda i, j: (i, 0))),
        out_shape=(jax.ShapeDtypeStruct((t, n), BF16), jax.ShapeDtypeStruct((t, GATE_LANES), F32)),
        compiler_params=pltpu.CompilerParams(
            dimension_semantics=("arbitrary", "arbitrary"),
            vmem_limit_bytes=_vmem_limit(blocks, 0, 4 * _nbytes((tm, INPROJ1_CHUNK), F32)),
        ),
        name="inproj1",
    )(h, w, wg)


def _scan_rows(x, rows, op, fill, reverse):
    n = x.shape[0]
    k = 1
    while k < n:
        if reverse:
            shifted = jnp.where(rows < n - k, pltpu.roll(x, n - k, 0), fill)
        else:
            shifted = jnp.where(rows >= k, pltpu.roll(x, k, 0), fill)
        x = op(x, shifted)
        k *= 2
    return x


GATE_PREP_CHUNKS = 4


def _gate_prep_kernel(g_ref, b_ref, o_ref):
    for ci in range(GATE_PREP_CHUNKS):
        _gate_prep_chunk(g_ref[0, ci * M_CHUNK:(ci + 1) * M_CHUNK, :] + b_ref[...], o_ref, ci)


def _gate_prep_chunk(g, o_ref, ci):
    rows = lax.broadcasted_iota(jnp.int32, g.shape, 0)
    kind = lax.broadcasted_iota(jnp.int32, g.shape, 1) % 8
    lf = jnp.minimum(g, 0.0) - jnp.log1p(jnp.exp(-jnp.abs(g)))
    csum = _scan_rows(lf, rows, jnp.add, 0.0, reverse=False)
    rsum = _scan_rows(lf, rows, jnp.add, 0.0, reverse=True)
    b_at_i = pltpu.roll(jnp.where(kind == 1, csum, rsum), LANES - 1, 1)
    grow = g - b_at_i
    cm_f = _scan_rows(grow, rows, jnp.maximum, NEG_INF, reverse=False)
    cm_b = _scan_rows(grow, rows, jnp.maximum, NEG_INF, reverse=True)
    y = jnp.where(kind == 0, b_at_i, 0.0)
    y = jnp.where(kind == 1, pltpu.roll(grow, 1, 1), y)
    y = jnp.where(kind == 2, pltpu.roll(cm_f, 2, 1), y)
    y = jnp.where(kind == 3, rsum, y)
    y = jnp.where(kind == 4, pltpu.roll(grow, 2, 1), y)
    y = jnp.where(kind == 5, pltpu.roll(cm_b, 3, 1), y)
    yt = y.T
    o_ref[0, :, ci] = yt[:M_HEADS * 8].reshape(M_HEADS, 8, g.shape[0])


def _gate_prep(gates, bias, bsz, seq):
    nc = seq // M_CHUNK
    per = GATE_PREP_CHUNKS
    return pl.pallas_call(
        _gate_prep_kernel,
        grid=(bsz, nc // per),
        in_specs=[pl.BlockSpec((1, per * M_CHUNK, GATE_LANES), lambda b, c: (b, c, 0)),
                  pl.BlockSpec((1, GATE_LANES), lambda b, c: (0, 0))],
        out_specs=pl.BlockSpec((1, M_HEADS, per, 8, M_CHUNK), lambda b, c: (b, 0, c, 0, 0)),
        out_shape=jax.ShapeDtypeStruct((bsz, M_HEADS, nc, 8, M_CHUNK), F32),
        compiler_params=pltpu.CompilerParams(dimension_semantics=("arbitrary", "arbitrary")),
        name="gate_prep",
    )(gates.reshape(bsz, seq, GATE_LANES), bias)


def _mlstm_kernel(q_ref, k_ref, v_ref, r_ref, gain_ref, o_ref, hs_scr, cf_scr, cb_scr, *, nc):
    L = M_CHUNK
    rows = lax.broadcasted_iota(jnp.int32, (L, L), 0)
    cols = lax.broadcasted_iota(jnp.int32, (L, L), 1)
    eye = rows == cols
    causal = cols <= rows
    anti = cols >= rows
    gain = gain_ref[0]

    def to_col(r):
        return jnp.sum(jnp.where(eye, r, 0.0), axis=1, keepdims=True)

    def chunk(c, c_scr, n, m, fwd):
        st = pl.multiple_of(c * L, L)
        qc = q_ref[0, pl.ds(st, L), :] * (M_QK_DIM ** -0.5)
        kc = k_ref[0, pl.ds(st, L), :]
        vc = v_ref[0, pl.ds(st, L), :]
        r = r_ref[0, 0, pl.ds(c, 1)].reshape(8, L)
        base = 0 if fwd else 3
        grow = r[base + 1:base + 2]
        bcol = to_col(r[base:base + 1])
        gcol = to_col(grow)
        mm = jnp.maximum(to_col(r[base + 2:base + 3]), m)
        e = L - 1 if fwd else 0
        mm_l = mm[e:e + 1]
        s = lax.dot_general(qc, kc, (((1,), (1,)), ((), ())), preferred_element_type=F32)
        a = jnp.exp(jnp.where(causal if fwd else anti, grow - mm, NEG_INF)) * s
        gint = jnp.exp(m - mm)
        cmat = c_scr[...]
        num = (jnp.dot(a.astype(BF16), vc, preferred_element_type=F32)
               + gint * jnp.dot(qc, cmat.astype(BF16), preferred_element_type=F32))
        qn = jnp.sum(qc.astype(F32) * n, axis=1, keepdims=True)
        den = jnp.sum(a, axis=1, keepdims=True) + gint * qn
        hc = num * (1.0 / jnp.maximum(jnp.abs(den), jnp.exp(-(bcol + mm))))
        decay = jnp.exp(m - mm_l)
        wk = jnp.exp(gcol - mm_l) * kc.astype(F32)
        c_scr[...] = decay * cmat + lax.dot_general(wk.astype(BF16), vc, (((0,), (0,)), ((), ())),
                                                    preferred_element_type=F32)
        n_new = decay * n + jnp.sum(wk, axis=0, keepdims=True)
        m_new = bcol[e:e + 1] + mm_l
        return hc, n_new, m_new

    def finish(c, hsum):
        st = pl.multiple_of(c * L, L)
        ms = jnp.mean(hsum * hsum, axis=-1, keepdims=True)
        o_ref[0, pl.ds(st, L), :] = (hsum * lax.rsqrt(ms + NORM_EPS) * gain).astype(o_ref.dtype)

    def step(j, carry, second_half):
        nf, mf, nb, mb = carry
        cf, cb = j, nc - 1 - j
        hf, nf, mf = chunk(cf, cf_scr, nf, mf, True)
        hb, nb, mb = chunk(cb, cb_scr, nb, mb, False)
        sf = pl.multiple_of(cf * L, L)
        sb = pl.multiple_of(cb * L, L)
        if second_half:
            finish(cf, hs_scr[pl.ds(sf, L), :] + hf)
            finish(cb, hs_scr[pl.ds(sb, L), :] + hb)
        else:
            hs_scr[pl.ds(sf, L), :] = hf
            hs_scr[pl.ds(sb, L), :] = hb
        return nf, mf, nb, mb

    cf_scr[...] = jnp.zeros_like(cf_scr)
    cb_scr[...] = jnp.zeros_like(cb_scr)
    n0 = jnp.zeros((1, M_QK_DIM), F32)
    m0 = jnp.zeros((1, 1), F32)
    carry = lax.fori_loop(0, nc // 2, functools.partial(step, second_half=False), (n0, m0, n0, m0), unroll=4)
    lax.fori_loop(nc // 2, nc, functools.partial(step, second_half=True), carry, unroll=4)


def _mlstm(proj1, prep, gain, bsz, seq):
    nc = seq // M_CHUNK
    assert nc % 2 == 0
    p3 = proj1.reshape(bsz, seq, M_MAIN_COLS)
    kb = M_QK_WIDTH // M_QK_DIM
    vb = 2 * M_QK_WIDTH // M_V_DIM
    blocks = (2 * _nbytes((seq, M_QK_DIM), BF16) + 2 * _nbytes((seq, M_V_DIM), BF16)
              + _nbytes((nc, 8, M_CHUNK), F32) + _nbytes((1, M_V_DIM), F32))
    scratch_bytes = _nbytes((seq, M_V_DIM), F32) + 2 * _nbytes((M_QK_DIM, M_V_DIM), F32)
    return pl.pallas_call(
        functools.partial(_mlstm_kernel, nc=nc),
        grid=(bsz, M_HEADS),
        in_specs=[
            pl.BlockSpec((1, seq, M_QK_DIM), lambda b, h: (b, 0, h)),
            pl.BlockSpec((1, seq, M_QK_DIM), lambda b, h: (b, 0, kb + h)),
            pl.BlockSpec((1, seq, M_V_DIM), lambda b, h: (b, 0, vb + h)),
            pl.BlockSpec((1, 1, nc, 8, M_CHUNK), lambda b, h: (b, h, 0, 0, 0)),
            pl.BlockSpec((1, 1, M_V_DIM), lambda b, h: (h, 0, 0)),
        ],
        out_specs=pl.BlockSpec((1, seq, M_V_DIM), lambda b, h: (b, 0, h)),
        out_shape=jax.ShapeDtypeStruct((bsz, seq, M_V_WIDTH), BF16),
        scratch_shapes=[pltpu.VMEM((seq, M_V_DIM), F32), pltpu.VMEM((M_QK_DIM, M_V_DIM), F32),
                        pltpu.VMEM((M_QK_DIM, M_V_DIM), F32)],
        compiler_params=pltpu.CompilerParams(
            dimension_semantics=("arbitrary", "arbitrary"),
            vmem_limit_bytes=_vmem_limit(blocks, scratch_bytes, 8 * 1024 * 1024),
        ),
        name="mlstm",
    )(p3, p3, p3, prep, gain.reshape(M_HEADS, 1, M_V_DIM).astype(F32))


def _gate_weights(w_in1, b_gate):
    wg = w_in1[:, M_MAIN_COLS:].reshape(D_MODEL, 4, M_HEADS).transpose(0, 2, 1)
    wg = jnp.pad(wg, ((0, 0), (0, 0), (0, 4))).reshape(D_MODEL, M_HEADS * 8)
    wg = jnp.pad(wg, ((0, 0), (0, GATE_LANES - M_HEADS * 8)))
    bg = jnp.pad(b_gate.astype(F32).reshape(4, M_HEADS).T, ((0, 0), (0, 4))).reshape(1, M_HEADS * 8)
    bg = jnp.pad(bg, ((0, 0), (0, GATE_LANES - M_HEADS * 8)))
    return wg.astype(BF16), bg


def _trunk(x, p):
    bsz, seq, d = x.shape
    t = bsz * seq
    assert seq % (2 * PERM_BLOCK) == 0
    x2d = x.reshape(t, d)
    h0 = _rmsnorm_orders(x2d, p["l0_norm_pre"])
    proj0 = _inproj0(h0, p["l0_w_in"], p["rope"], seq)
    outs, lses = zip(*[_attention_group(proj0, g, bsz, seq) for g in range(N_GROUPS)])
    x1, h1 = _outproj0(outs, lses, proj0, p["l0_w_out"], x2d, p["l0_norm_post"], p["l1_norm_pre"])
    proj1, gates = _inproj1(h1, p["l1_w_main"], p["l1_w_gate"])
    prep = _gate_prep(gates, p["l1_b_gate"], bsz, seq)
    hn = _mlstm(proj1, prep, p["l1_head_norm"], bsz, seq)
    y = _outproj1(hn.reshape(t, M_V_WIDTH), proj1, p["l1_w_out"], x1, p["l1_norm_post"])
    return y.reshape(bsz, seq, d)


def kernel(x_prompt, x_sample, l0_norm_pre, l0_w_in, l0_w_out, l0_norm_post,
           l1_norm_pre, l1_w_in, l1_b_gate, l1_head_norm, l1_w_out, l1_norm_post):
    assert x_prompt.shape[1] == x_sample.shape[1]
    w_gate, b_gate = _gate_weights(l1_w_in, l1_b_gate)
    p = {
        "l0_norm_pre": l0_norm_pre, "l0_w_in": l0_w_in.astype(BF16), "l0_w_out": l0_w_out.astype(BF16),
        "l0_norm_post": l0_norm_post, "l1_norm_pre": l1_norm_pre,
        "l1_w_main": l1_w_in.astype(BF16), "l1_w_gate": w_gate, "l1_b_gate": b_gate,
        "l1_head_norm": l1_head_norm, "l1_w_out": l1_w_out.astype(BF16), "l1_norm_post": l1_norm_post,
        "rope": _rope_tables(x_prompt.shape[1]),
    }
    return (_trunk(x_prompt, p), _trunk(x_sample, p))
```

```python
import functools

import jax
import jax.numpy as jnp
from jax import lax
from jax.experimental import pallas as pl
from jax.experimental.pallas import tpu as pltpu

F32 = jnp.float32
BF16 = jnp.bfloat16

D_MODEL = 2048
ATT_GROUPS = ((128, 1), (512, 4), (2048, 16))
N_GROUPS = len(ATT_GROUPS)
ATT_HEADS = 16
ATT_HEAD_DIM = 128
ATT_WIDTH = ATT_HEADS * ATT_HEAD_DIM
ATT_IN_COLS = 3 * N_GROUPS * ATT_WIDTH + ATT_WIDTH
ATT_HALF = 64
ROPE_THETA = 500000.0
ROPE_DIMS = ATT_HEAD_DIM // 4
ROPE_HALF = ROPE_DIMS // 2
LOG2E = 1.4426950408889634
ATT_Q_SCALE = ATT_HEAD_DIM ** -0.5 * LOG2E

M_HEADS = 8
M_QK_DIM = 256
M_V_DIM = 512
M_QK_WIDTH = M_HEADS * M_QK_DIM
M_V_WIDTH = M_HEADS * M_V_DIM
M_MAIN_COLS = 2 * M_QK_WIDTH + 3 * M_V_WIDTH
M_CHUNK = 128
GATE_LANES = 128

NORM_EPS = 1e-6
NEG_INF = -1e30

LANES = 128
VMEM_CAP_BYTES = 56 * 1024 * 1024


def _vmem_limit(block_bytes, scratch_bytes=0, temp_bytes=0):
    need = 2 * block_bytes + scratch_bytes + temp_bytes
    return int(min(max(need, 16 * 1024 * 1024), VMEM_CAP_BYTES))


def _nbytes(shape, dtype):
    n = 1
    for s in shape:
        n *= s
    return n * jnp.dtype(dtype).itemsize


PERM_BLOCK = 1024


def _rmsnorm_orders_kernel(x_ref, g_ref, o_ref, slab_scr):
    x = x_ref[...]
    inv = lax.rsqrt(jnp.mean(x * x, axis=-1, keepdims=True) + NORM_EPS)
    nslab = x_ref.shape[1] // LANES
    for c in range(nslab):
        ls = slice(c * LANES, (c + 1) * LANES)
        y = x_ref[:, ls] * inv * g_ref[:, ls]
        slab_scr[c] = y
        for g, (_, dil) in enumerate(ATT_GROUPS):
            if dil == 1:
                o_ref[g, :, ls] = y.astype(o_ref.dtype)
    for g, (_, dil) in enumerate(ATT_GROUPS):
        if dil == 1:
            continue
        n = PERM_BLOCK // dil
        for r in range(dil):
            for c in range(nslab):
                o_ref[g, r * n:(r + 1) * n, c * LANES:(c + 1) * LANES] = (
                    slab_scr[c, pl.ds(r, n, stride=dil), :].astype(o_ref.dtype))


def _rmsnorm_orders(x2d, gain):
    t, d = x2d.shape
    tm = PERM_BLOCK
    return pl.pallas_call(
        _rmsnorm_orders_kernel,
        grid=(t // tm,),
        in_specs=[pl.BlockSpec((tm, d), lambda i: (i, 0)), pl.BlockSpec((1, d), lambda i: (0, 0))],
        out_specs=pl.BlockSpec((N_GROUPS, tm, d), lambda i: (0, i, 0)),
        out_shape=jax.ShapeDtypeStruct((N_GROUPS, t, d), BF16),
        scratch_shapes=[pltpu.VMEM((d // LANES, tm, LANES), F32)],
        compiler_params=pltpu.CompilerParams(
            dimension_semantics=("arbitrary",),
            vmem_limit_bytes=_vmem_limit(_nbytes((tm, d), F32) + _nbytes((N_GROUPS, tm, d), BF16),
                                         _nbytes((tm, d), F32), _nbytes((tm, d), F32)),
        ),
        name="rmsnorm_orders",
    )(x2d, gain.reshape(1, d).astype(F32))


def _order_index(seq, dil):
    n = PERM_BLOCK // dil
    return jnp.arange(seq, dtype=jnp.int32).reshape(seq // PERM_BLOCK, n, dil).transpose(0, 2, 1).reshape(seq)


INPROJ0_TN = 2048
INPROJ0_CHUNK = 256
INPROJ0_TAIL_ROWS = ((0, 512), (512, 768), (768, 1024))
_GROUP_TILES = 3 * ATT_WIDTH // INPROJ0_TN
_Z_TILES = ATT_WIDTH // INPROJ0_TN


def _inproj0_col_tile(jj):
    return jnp.where(jj < _GROUP_TILES, jj,
                     jnp.where(jj < _GROUP_TILES + _Z_TILES, jj + (N_GROUPS - 1) * _GROUP_TILES, jj - _Z_TILES))


def _inproj0_order(jj):
    return jnp.where(jj < _GROUP_TILES + _Z_TILES, 0, (jj - _Z_TILES) // _GROUP_TILES)


def _inproj0_kernel(h_ref, w_ref, c_ref, s1_ref, s2_ref, o_ref):
    col = _inproj0_col_tile(pl.program_id(1))
    seg = (col * INPROJ0_TN) // ATT_WIDTH
    kind = seg % 3
    is_rope = jnp.logical_and(seg < 3 * N_GROUPS, kind < 2)
    nchunk = INPROJ0_TN // INPROJ0_CHUNK
    tm = o_ref.shape[0]

    def pieces():
        for cc in range(nchunk):
            cols = slice(cc * INPROJ0_CHUNK, (cc + 1) * INPROJ0_CHUNK)
            if cc < nchunk - 1:
                yield slice(0, tm), cols
            else:
                for lo, hi in INPROJ0_TAIL_ROWS:
                    yield slice(lo, hi), cols

    @pl.when(is_rope)
    def _():
        scale = jnp.where(kind == 0, ATT_Q_SCALE, 1.0).astype(F32)
        c_all = c_ref[...] * scale
        s1_all = s1_ref[...] * scale
        s2_all = s2_ref[...] * scale
        for rows, cols in pieces():
            c, s1, s2 = c_all[rows], s1_all[rows], s2_all[rows]
            acc = jnp.dot(h_ref[rows, :], w_ref[:, cols], preferred_element_type=F32)
            for t in range(INPROJ0_CHUNK // LANES):
                a = acc[:, t * LANES:(t + 1) * LANES]
                r = a * c + pltpu.roll(a, LANES - ROPE_HALF, 1) * s1 + pltpu.roll(a, ROPE_HALF, 1) * s2
                lo = cols.start + t * LANES
                o_ref[rows, lo:lo + LANES] = r.astype(o_ref.dtype)

    @pl.when(jnp.logical_not(is_rope))
    def _():
        for rows, cols in pieces():
            o_ref[rows, cols] = jnp.dot(h_ref[rows, :], w_ref[:, cols], preferred_element_type=F32).astype(o_ref.dtype)


def _rope_tables(seq):
    inv = jnp.power(ROPE_THETA, -jnp.arange(ROPE_HALF, dtype=F32) / ROPE_HALF)
    ang = jnp.arange(seq, dtype=F32)[:, None] * inv[None, :]
    cos, sin = jnp.cos(ang), jnp.sin(ang)
    zeros = jnp.zeros((seq, LANES - ROPE_DIMS), F32)
    zh = jnp.zeros((seq, ROPE_HALF), F32)
    c = jnp.concatenate([cos, cos, jnp.ones((seq, LANES - ROPE_DIMS), F32)], axis=1)
    s1 = jnp.concatenate([-sin, zh, zeros], axis=1)
    s2 = jnp.concatenate([zh, sin, zeros], axis=1)
    orders = [_order_index(seq, dil) for _, dil in ATT_GROUPS]
    return tuple(jnp.stack([tab[idx] for idx in orders]) for tab in (c, s1, s2))


def _inproj0(h_orders, w, tables, seq):
    _, t, k = h_orders.shape
    n = w.shape[1]
    tm, tn = PERM_BLOCK, INPROJ0_TN
    pos_blocks = seq // tm
    tab_spec = pl.BlockSpec((None, tm, LANES), lambda i, j: (_inproj0_order(j), i % pos_blocks, 0))
    blocks = _nbytes((tm, k), BF16) + _nbytes((k, tn), BF16) + _nbytes((tm, tn), BF16) + 3 * _nbytes((tm, LANES), F32)
    return pl.pallas_call(
        _inproj0_kernel,
        grid=(t // tm, n // tn),
        in_specs=[
            pl.BlockSpec((None, tm, k), lambda i, j: (_inproj0_order(j), i, 0)),
            pl.BlockSpec((k, tn), lambda i, j: (0, _inproj0_col_tile(j))),
            tab_spec, tab_spec, tab_spec,
        ],
        out_specs=pl.BlockSpec((tm, tn), lambda i, j: (i, _inproj0_col_tile(j))),
        out_shape=jax.ShapeDtypeStruct((t, n), BF16),
        compiler_params=pltpu.CompilerParams(
            dimension_semantics=("arbitrary", "arbitrary"),
            vmem_limit_bytes=_vmem_limit(blocks, 0, _nbytes((tm, k), BF16) + 3 * _nbytes((tm, LANES), F32)
                                         + 4 * _nbytes((tm, INPROJ0_CHUNK), F32)),
        ),
        name="inproj0",
    )(h_orders, w, *tables)


ATT_TQ = 512


def _attn_kernel(q_ref, kp_ref, kc_ref, kn_ref, vp_ref, vc_ref, vn_ref, o_ref, l_ref, qx, kx, vx, ox, lx, *, tq, ls):
    hb = ATT_HALF
    sub = 2 * hb
    w = qx.shape[1]
    i = pl.program_id(2)
    qx[...] = q_ref[0].reshape(tq, w)
    kx[0:hb, :] = kp_ref[0].reshape(hb, w)
    kx[hb:hb + tq, :] = kc_ref[0].reshape(tq, w)
    kx[hb + tq:, :] = kn_ref[0].reshape(hb, w)
    vx[0:hb, :] = vp_ref[0].reshape(hb, w)
    vx[hb:hb + tq, :] = vc_ref[0].reshape(tq, w)
    vx[hb + tq:, :] = vn_ref[0].reshape(hb, w)

    row = lax.broadcasted_iota(jnp.int32, (sub, 2 * sub), 0)
    col = lax.broadcasted_iota(jnp.int32, (sub, 2 * sub), 1)
    band = jnp.abs(col - hb - row) <= hb
    lane = lax.broadcasted_iota(jnp.int32, (sub, LANES), 1)

    for a in range(tq // sub):
        r0 = a * sub
        kidx = i * tq + r0 + col - hb
        valid = band & (kidx >= 0) & (kidx < ls)
        lse_tile = jnp.zeros((sub, LANES), F32)
        for h in range(ATT_HEADS):
            hs = slice(h * ATT_HEAD_DIM, (h + 1) * ATT_HEAD_DIM)
            qh = qx[r0:r0 + sub, hs]
            kh = kx[r0:r0 + 2 * sub, hs]
            vh = vx[r0:r0 + 2 * sub, hs]
            s = lax.dot_general(qh, kh, (((1,), (1,)), ((), ())), preferred_element_type=F32)
            s = jnp.where(valid, s, NEG_INF)
            m = jnp.max(s, axis=-1, keepdims=True)
            p = jnp.exp2(s - m)
            den = jnp.sum(p, axis=-1, keepdims=True)
            o = jnp.dot(p.astype(BF16), vh, preferred_element_type=F32) * (1.0 / den)
            lse_tile = jnp.where(lane == h, m + jnp.log2(den), lse_tile)
            ox[r0:r0 + sub, hs] = o.astype(ox.dtype)
        lx[r0:r0 + sub, :] = lse_tile
    o_ref[0] = ox[...].reshape(o_ref.shape[1:])
    l_ref[0] = lx[...].reshape(l_ref.shape[1:])


def _attention_group(proj, g, bsz, seq):
    _, dil = ATT_GROUPS[g]
    w = ATT_WIDTH
    hb = ATT_HALF
    ls = seq // dil
    if dil == 1:
        n = min(ATT_TQ, 256)
        nb = seq // n
    else:
        nb, n = seq // PERM_BLOCK, PERM_BLOCK // dil
    tq = min(ATT_TQ, ls)
    bpt = tq // n
    assert tq % (2 * hb) == 0 and n % hb == 0 and tq % n == 0 and nb % bpt == 0
    pv = proj.reshape(bsz, nb, dil, n, ATT_IN_COLS)
    ncb = 3 * g

    grid = (bsz, dil, nb // bpt)
    main = lambda cb: pl.BlockSpec((1, bpt, None, n, w), lambda b, r, i: (b, i, r, 0, cb))
    prev_h = lambda cb: pl.BlockSpec((1, 1, None, hb, w),
                                     lambda b, r, i: (b, jnp.maximum(i * bpt - 1, 0), r, n // hb - 1, cb))
    next_h = lambda cb: pl.BlockSpec((1, 1, None, hb, w),
                                     lambda b, r, i: (b, jnp.minimum((i + 1) * bpt, nb - 1), r, 0, cb))
    out_block = lambda width: pl.BlockSpec((1, bpt, None, n, width), lambda b, r, i: (b, i, r, 0, 0))

    in_specs = [main(ncb), prev_h(ncb + 1), main(ncb + 1), next_h(ncb + 1),
                prev_h(ncb + 2), main(ncb + 2), next_h(ncb + 2)]
    blocks = 4 * _nbytes((tq, w), BF16) + 4 * _nbytes((hb, w), BF16) + _nbytes((tq, LANES), F32)
    scratch = [pltpu.VMEM((tq, w), BF16), pltpu.VMEM((tq + 2 * hb, w), BF16), pltpu.VMEM((tq + 2 * hb, w), BF16),
               pltpu.VMEM((tq, w), BF16), pltpu.VMEM((tq, LANES), F32)]
    scratch_bytes = 4 * _nbytes((tq + 2 * hb, w), BF16) + _nbytes((tq, LANES), F32)
    o, l = pl.pallas_call(
        functools.partial(_attn_kernel, tq=tq, ls=ls),
        grid=grid,
        in_specs=in_specs,
        out_specs=(out_block(w), out_block(LANES)),
        out_shape=(jax.ShapeDtypeStruct((bsz, nb, dil, n, w), BF16),
                   jax.ShapeDtypeStruct((bsz, nb, dil, n, LANES), F32)),
        scratch_shapes=scratch,
        compiler_params=pltpu.CompilerParams(
            dimension_semantics=("arbitrary", "arbitrary", "arbitrary"),
            vmem_limit_bytes=_vmem_limit(blocks, scratch_bytes, 8 * 1024 * 1024),
        ),
        name=f"attn_g{g}",
    )(*([pv] * 7))
    if dil > 1:
        o = o.transpose(0, 1, 3, 2, 4)
        l = l.transpose(0, 1, 3, 2, 4)
    return o.reshape(bsz * seq, w), l.reshape(bsz * seq, LANES)


def _post_norm_residual(out, x_ref, gpost_ref, o_ref, gnext_ref=None, hnext_ref=None):
    ms = jnp.mean(out * out, axis=-1, keepdims=True)
    x1 = x_ref[...] + out * lax.rsqrt(ms + NORM_EPS) * gpost_ref[...]
    o_ref[...] = x1
    if hnext_ref is not None:
        ms1 = jnp.mean(x1 * x1, axis=-1, keepdims=True)
        hnext_ref[...] = (x1 * lax.rsqrt(ms1 + NORM_EPS) * gnext_ref[...]).astype(hnext_ref.dtype)


OUTPROJ_KCHUNK = 512


def _sigmoid(x):
    return 0.5 * jnp.tanh(0.5 * x) + 0.5


def _silu(x):
    xh = 0.5 * x
    return xh * jnp.tanh(xh) + xh


def _outproj0_kernel(o0_ref, o1_ref, o2_ref, l0_ref, l1_ref, l2_ref, z_ref, w_ref, x_ref, gpost_ref, gnext_ref,
                     x1_ref, hnext_ref):
    o_refs = (o0_ref, o1_ref, o2_ref)
    lses = [r[...] for r in (l0_ref, l1_ref, l2_ref)]
    lmax = jnp.maximum(jnp.maximum(lses[0], lses[1]), lses[2])
    es = [jnp.exp2(l - lmax) for l in lses]
    tot = es[0] + es[1] + es[2]
    wts = [e / tot for e in es]
    heads_per_chunk = OUTPROJ_KCHUNK // ATT_HEAD_DIM
    out = None
    for c in range(ATT_HEADS // heads_per_chunk):
        ys = []
        for h in range(c * heads_per_chunk, (c + 1) * heads_per_chunk):
            hs = slice(h * ATT_HEAD_DIM, (h + 1) * ATT_HEAD_DIM)
            o = wts[0][:, h:h + 1] * o_refs[0][:, hs].astype(F32)
            for g in range(1, N_GROUPS):
                o = o + wts[g][:, h:h + 1] * o_refs[g][:, hs].astype(F32)
            ys.append((o * _silu(z_ref[:, hs].astype(F32))).astype(BF16))
        part = jnp.dot(jnp.concatenate(ys, axis=1), w_ref[c * OUTPROJ_KCHUNK:(c + 1) * OUTPROJ_KCHUNK, :],
                       preferred_element_type=F32)
        out = part if out is None else out + part
    _post_norm_residual(out, x_ref, gpost_ref, x1_ref, gnext_ref, hnext_ref)


def _outproj0(outs, lses, proj0, w, x2d, g_post, g_next, tm=256):
    t, d = x2d.shape
    k = w.shape[0]
    row = lambda i: (i, 0)
    vec = pl.BlockSpec((1, d), lambda i: (0, 0))
    zb = 3 * N_GROUPS
    in_specs = ([pl.BlockSpec((tm, k), row)] * N_GROUPS + [pl.BlockSpec((tm, LANES), row)] * N_GROUPS
                + [pl.BlockSpec((tm, k), lambda i: (i, zb)),
                   pl.BlockSpec((k, d), lambda i: (0, 0), pipeline_mode=pl.Buffered(1)),
                   pl.BlockSpec((tm, d), row), vec, vec])
    blocks = ((N_GROUPS + 1) * _nbytes((tm, k), BF16) + N_GROUPS * _nbytes((tm, LANES), F32)
              + 2 * _nbytes((tm, d), F32) + _nbytes((tm, d), BF16))
    return pl.pallas_call(
        _outproj0_kernel,
        grid=(t // tm,),
        in_specs=in_specs,
        out_specs=(pl.BlockSpec((tm, d), row), pl.BlockSpec((tm, d), row)),
        out_shape=(jax.ShapeDtypeStruct((t, d), F32), jax.ShapeDtypeStruct((t, d), BF16)),
        compiler_params=pltpu.CompilerParams(
            dimension_semantics=("arbitrary",),
            vmem_limit_bytes=_vmem_limit(blocks, _nbytes((k, d), BF16), 4 * _nbytes((tm, d), F32)),
        ),
        name="outproj0",
    )(*outs, *lses, proj0, w, x2d, g_post.reshape(1, d).astype(F32), g_next.reshape(1, d).astype(F32))


def _outproj1_kernel(hs_ref, og_ref, z_ref, hgain_ref, w_ref, x_ref, gpost_ref, o_ref):
    assert OUTPROJ_KCHUNK == M_V_DIM
    out = None
    for c in range(hs_ref.shape[1] // OUTPROJ_KCHUNK):
        ks = slice(c * OUTPROJ_KCHUNK, (c + 1) * OUTPROJ_KCHUNK)
        hs = hs_ref[:, ks].astype(F32)
        hn = hs * lax.rsqrt(jnp.mean(hs * hs, axis=-1, keepdims=True) + NORM_EPS) * hgain_ref[:, ks]
        y = (og_ref[:, ks].astype(F32) * hn) * z_ref[:, ks].astype(F32)
        part = jnp.dot(y.astype(BF16), w_ref[ks, :], preferred_element_type=F32)
        out = part if out is None else out + part
    _post_norm_residual(out, x_ref, gpost_ref, o_ref)


def _outproj1(hs, proj1, head_gain, w, x2d, g_post, tm=256):
    t, d = x2d.shape
    k = w.shape[0]
    row = lambda i: (i, 0)
    ob = M_OGATE_COL // k
    blocks = 3 * _nbytes((tm, k), BF16) + _nbytes((1, k), F32) + 2 * _nbytes((tm, d), F32)
    return pl.pallas_call(
        _outproj1_kernel,
        grid=(t // tm,),
        in_specs=[pl.BlockSpec((tm, k), row), pl.BlockSpec((tm, k), lambda i: (i, ob)),
                  pl.BlockSpec((tm, k), lambda i: (i, ob + 1)),
                  pl.BlockSpec((1, k), lambda i: (0, 0)),
                  pl.BlockSpec((k, d), lambda i: (0, 0), pipeline_mode=pl.Buffered(1)),
                  pl.BlockSpec((tm, d), row), pl.BlockSpec((1, d), lambda i: (0, 0))],
        out_specs=pl.BlockSpec((tm, d), row),
        out_shape=jax.ShapeDtypeStruct((t, d), F32),
        compiler_params=pltpu.CompilerParams(
            dimension_semantics=("arbitrary",),
            vmem_limit_bytes=_vmem_limit(blocks, _nbytes((k, d), BF16), 4 * _nbytes((tm, k), F32)),
        ),
        name="outproj1",
    )(hs, proj1, proj1, head_gain.reshape(1, k).astype(F32), w, x2d, g_post.reshape(1, d).astype(F32))


INPROJ1_CHUNK = 512


M_OGATE_COL = 2 * M_QK_WIDTH + M_V_WIDTH
M_ZGATE_COL = M_OGATE_COL + M_V_WIDTH


def _inproj1_kernel(h_ref, w_ref, wg_ref, o_ref, g_ref):
    tn = o_ref.shape[1]
    col0 = pl.program_id(1) * tn

    def tile(act):
        for cc in range(tn // INPROJ1_CHUNK):
            cs = slice(cc * INPROJ1_CHUNK, (cc + 1) * INPROJ1_CHUNK)
            acc = jnp.dot(h_ref[...], w_ref[:, cs], preferred_element_type=F32)
            o_ref[:, cs] = (acc if act is None else act(acc)).astype(o_ref.dtype)

    pl.when(col0 < M_OGATE_COL)(lambda: tile(None))
    pl.when(jnp.logical_and(col0 >= M_OGATE_COL, col0 < M_ZGATE_COL))(lambda: tile(_sigmoid))
    pl.when(col0 >= M_ZGATE_COL)(lambda: tile(_silu))

    @pl.when(pl.program_id(1) == 0)
    def _():
        g_ref[...] = jnp.dot(h_ref[...], wg_ref[...], preferred_element_type=F32)


def _inproj1(h, w, wg, tm=1024, tn=2048):
    t, k = h.shape
    n = M_MAIN_COLS
    tm = min(tm, t)
    blocks = (_nbytes((tm, k), BF16) + _nbytes((k, tn), BF16) + _nbytes((k, GATE_LANES), BF16)
              + _nbytes((tm, tn), BF16) + _nbytes((tm, GATE_LANES), F32))
    return pl.pallas_call(
        _inproj1_kernel,
        grid=(t // tm, n // tn),
        in_specs=[
            pl.BlockSpec((tm, k), lambda i, j: (i, 0)),
            pl.BlockSpec((k, tn), lambda i, j: (0, j)),
            pl.BlockSpec((k, GATE_LANES), lambda i, j: (0, 0)),
        ],
        out_specs=(pl.BlockSpec((tm, tn), lambda i, j: (i, j)), pl.BlockSpec((tm, GATE_LANES), lambda i, j: (i, 0))),
        out_shape=(jax.ShapeDtypeStruct((t, n), BF16), jax.ShapeDtypeStruct((t, GATE_LANES), F32)),
        compiler_params=pltpu.CompilerParams(
            dimension_semantics=("arbitrary", "arbitrary"),
            vmem_limit_bytes=_vmem_limit(blocks, 0, 4 * _nbytes((tm, INPROJ1_CHUNK), F32)),
        ),
        name="inproj1",
    )(h, w, wg)


def _scan_rows(x, rows, op, fill, reverse):
    n = x.shape[0]
    k = 1
    while k < n:
        if reverse:
            shifted = jnp.where(rows < n - k, pltpu.roll(x, n - k, 0), fill)
        else:
            shifted = jnp.where(rows >= k, pltpu.roll(x, k, 0), fill)
        x = op(x, shifted)
        k *= 2
    return x


GATE_PREP_CHUNKS = 4


def _gate_prep_kernel(g_ref, b_ref, o_ref):
    for ci in range(GATE_PREP_CHUNKS):
        _gate_prep_chunk(g_ref[0, ci * M_CHUNK:(ci + 1) * M_CHUNK, :] + b_ref[...], o_ref, ci)


def _gate_prep_chunk(g, o_ref, ci):
    rows = lax.broadcasted_iota(jnp.int32, g.shape, 0)
    kind = lax.broadcasted_iota(jnp.int32, g.shape, 1) % 8
    lf = jnp.minimum(g, 0.0) - jnp.log1p(jnp.exp(-jnp.abs(g)))
    csum = _scan_rows(lf, rows, jnp.add, 0.0, reverse=False)
    rsum = _scan_rows(lf, rows, jnp.add, 0.0, reverse=True)
    b_at_i = pltpu.roll(jnp.where(kind == 1, csum, rsum), LANES - 1, 1)
    grow = g - b_at_i
    cm_f = _scan_rows(grow, rows, jnp.maximum, NEG_INF, reverse=False)
    cm_b = _scan_rows(grow, rows, jnp.maximum, NEG_INF, reverse=True)
    y = jnp.where(kind == 0, b_at_i, 0.0)
    y = jnp.where(kind == 1, pltpu.roll(grow, 1, 1), y)
    y = jnp.where(kind == 2, pltpu.roll(cm_f, 2, 1), y)
    y = jnp.where(kind == 3, rsum, y)
    y = jnp.where(kind == 4, pltpu.roll(grow, 2, 1), y)
    y = jnp.where(kind == 5, pltpu.roll(cm_b, 3, 1), y)
    yt = y.T
    o_ref[0, :, ci] = yt[:M_HEADS * 8].reshape(M_HEADS, 8, g.shape[0])


def _gate_prep(gates, bias, bsz, seq):
    nc = seq // M_CHUNK
    per = GATE_PREP_CHUNKS
    return pl.pallas_call(
        _gate_prep_kernel,
        grid=(bsz, nc // per),
        in_specs=[pl.BlockSpec((1, per * M_CHUNK, GATE_LANES), lambda b, c: (b, c, 0)),
                  pl.BlockSpec((1, GATE_LANES), lambda b, c: (0, 0))],
        out_specs=pl.BlockSpec((1, M_HEADS, per, 8, M_CHUNK), lambda b, c: (b, 0, c, 0, 0)),
        out_shape=jax.ShapeDtypeStruct((bsz, M_HEADS, nc, 8, M_CHUNK), F32),
        compiler_params=pltpu.CompilerParams(dimension_semantics=("arbitrary", "arbitrary")),
        name="gate_prep",
    )(gates.reshape(bsz, seq, GATE_LANES), bias)


def _mlstm_kernel(q_ref, k_ref, v_ref, r_ref, o_ref, hs_scr, cf_scr, cb_scr, *, nc):
    L = M_CHUNK
    rows = lax.broadcasted_iota(jnp.int32, (L, L), 0)
    cols = lax.broadcasted_iota(jnp.int32, (L, L), 1)
    eye = rows == cols
    causal = cols <= rows
    anti = cols >= rows

    def to_col(r):
        return jnp.sum(jnp.where(eye, r, 0.0), axis=1, keepdims=True)

    def chunk(c, c_scr, n, m, fwd):
        st = pl.multiple_of(c * L, L)
        qc = q_ref[0, pl.ds(st, L), :] * (M_QK_DIM ** -0.5)
        kc = k_ref[0, pl.ds(st, L), :]
        vc = v_ref[0, pl.ds(st, L), :]
        r = r_ref[0, 0, pl.ds(c, 1)].reshape(8, L)
        base = 0 if fwd else 3
        grow = r[base + 1:base + 2]
        bcol = to_col(r[base:base + 1])
        gcol = to_col(grow)
        mm = jnp.maximum(to_col(r[base + 2:base + 3]), m)
        e = L - 1 if fwd else 0
        mm_l = mm[e:e + 1]
        s = lax.dot_general(qc, kc, (((1,), (1,)), ((), ())), preferred_element_type=F32)
        a = jnp.exp(jnp.where(causal if fwd else anti, grow - mm, NEG_INF)) * s
        gint = jnp.exp(m - mm)
        cmat = c_scr[...]
        num = (jnp.dot(a.astype(BF16), vc, preferred_element_type=F32)
               + gint * jnp.dot(qc, cmat.astype(BF16), preferred_element_type=F32))
        qn = jnp.sum(qc.astype(F32) * n, axis=1, keepdims=True)
        den = jnp.sum(a, axis=1, keepdims=True) + gint * qn
        hc = num * (1.0 / jnp.maximum(jnp.abs(den), jnp.exp(-(bcol + mm))))
        decay = jnp.exp(m - mm_l)
        wk = jnp.exp(gcol - mm_l) * kc.astype(F32)
        c_scr[...] = decay * cmat + lax.dot_general(wk.astype(BF16), vc, (((0,), (0,)), ((), ())),
                                                    preferred_element_type=F32)
        n_new = decay * n + jnp.sum(wk, axis=0, keepdims=True)
        m_new = bcol[e:e + 1] + mm_l
        return hc, n_new, m_new

    def finish(c, hsum):
        st = pl.multiple_of(c * L, L)
        o_ref[0, pl.ds(st, L), :] = hsum.astype(o_ref.dtype)

    def step(j, carry, second_half):
        nf, mf, nb, mb = carry
        cf, cb = j, nc - 1 - j
        hf, nf, mf = chunk(cf, cf_scr, nf, mf, True)
        hb, nb, mb = chunk(cb, cb_scr, nb, mb, False)
        sf = pl.multiple_of(cf * L, L)
        sb = pl.multiple_of(cb * L, L)
        if second_half:
            finish(cf, hs_scr[pl.ds(sf, L), :] + hf)
            finish(cb, hs_scr[pl.ds(sb, L), :] + hb)
        else:
            hs_scr[pl.ds(sf, L), :] = hf
            hs_scr[pl.ds(sb, L), :] = hb
        return nf, mf, nb, mb

    cf_scr[...] = jnp.zeros_like(cf_scr)
    cb_scr[...] = jnp.zeros_like(cb_scr)
    n0 = jnp.zeros((1, M_QK_DIM), F32)
    m0 = jnp.zeros((1, 1), F32)
    carry = lax.fori_loop(0, nc // 2, functools.partial(step, second_half=False), (n0, m0, n0, m0), unroll=4)
    lax.fori_loop(nc // 2, nc, functools.partial(step, second_half=True), carry, unroll=4)


def _mlstm(proj1, prep, bsz, seq):
    nc = seq // M_CHUNK
    assert nc % 2 == 0
    p3 = proj1.reshape(bsz, seq, M_MAIN_COLS)
    kb = M_QK_WIDTH // M_QK_DIM
    vb = 2 * M_QK_WIDTH // M_V_DIM
    blocks = (2 * _nbytes((seq, M_QK_DIM), BF16) + 2 * _nbytes((seq, M_V_DIM), BF16)
              + _nbytes((nc, 8, M_CHUNK), F32))
    scratch_bytes = _nbytes((seq, M_V_DIM), F32) + 2 * _nbytes((M_QK_DIM, M_V_DIM), F32)
    return pl.pallas_call(
        functools.partial(_mlstm_kernel, nc=nc),
        grid=(bsz, M_HEADS),
        in_specs=[
            pl.BlockSpec((1, seq, M_QK_DIM), lambda b, h: (b, 0, h)),
            pl.BlockSpec((1, seq, M_QK_DIM), lambda b, h: (b, 0, kb + h)),
            pl.BlockSpec((1, seq, M_V_DIM), lambda b, h: (b, 0, vb + h)),
            pl.BlockSpec((1, 1, nc, 8, M_CHUNK), lambda b, h: (b, h, 0, 0, 0)),
        ],
        out_specs=pl.BlockSpec((1, seq, M_V_DIM), lambda b, h: (b, 0, h)),
        out_shape=jax.ShapeDtypeStruct((bsz, seq, M_V_WIDTH), BF16),
        scratch_shapes=[pltpu.VMEM((seq, M_V_DIM), F32), pltpu.VMEM((M_QK_DIM, M_V_DIM), F32),
                        pltpu.VMEM((M_QK_DIM, M_V_DIM), F32)],
        compiler_params=pltpu.CompilerParams(
            dimension_semantics=("arbitrary", "arbitrary"),
            vmem_limit_bytes=_vmem_limit(blocks, scratch_bytes, 8 * 1024 * 1024),
        ),
        name="mlstm",
    )(p3, p3, p3, prep)


def _gate_weights(w_in1, b_gate):
    wg = w_in1[:, M_MAIN_COLS:].reshape(D_MODEL, 4, M_HEADS).transpose(0, 2, 1)
    wg = jnp.pad(wg, ((0, 0), (0, 0), (0, 4))).reshape(D_MODEL, M_HEADS * 8)
    wg = jnp.pad(wg, ((0, 0), (0, GATE_LANES - M_HEADS * 8)))
    bg = jnp.pad(b_gate.astype(F32).reshape(4, M_HEADS).T, ((0, 0), (0, 4))).reshape(1, M_HEADS * 8)
    bg = jnp.pad(bg, ((0, 0), (0, GATE_LANES - M_HEADS * 8)))
    return wg.astype(BF16), bg


def _trunk(x, p):
    bsz, seq, d = x.shape
    t = bsz * seq
    assert seq % (2 * PERM_BLOCK) == 0
    x2d = x.reshape(t, d)
    h0 = _rmsnorm_orders(x2d, p["l0_norm_pre"])
    proj0 = _inproj0(h0, p["l0_w_in"], p["rope"], seq)
    outs, lses = zip(*[_attention_group(proj0, g, bsz, seq) for g in range(N_GROUPS)])
    x1, h1 = _outproj0(outs, lses, proj0, p["l0_w_out"], x2d, p["l0_norm_post"], p["l1_norm_pre"])
    proj1, gates = _inproj1(h1, p["l1_w_main"], p["l1_w_gate"])
    prep = _gate_prep(gates, p["l1_b_gate"], bsz, seq)
    hs = _mlstm(proj1, prep, bsz, seq)
    y = _outproj1(hs.reshape(t, M_V_WIDTH), proj1, p["l1_head_norm"], p["l1_w_out"], x1, p["l1_norm_post"])
    return y.reshape(bsz, seq, d)


def kernel(x_prompt, x_sample, l0_norm_pre, l0_w_in, l0_w_out, l0_norm_post,
           l1_norm_pre, l1_w_in, l1_b_gate, l1_head_norm, l1_w_out, l1_norm_post):
    assert x_prompt.shape[1] == x_sample.shape[1]
    w_gate, b_gate = _gate_weights(l1_w_in, l1_b_gate)
    p = {
        "l0_norm_pre": l0_norm_pre, "l0_w_in": l0_w_in.astype(BF16), "l0_w_out": l0_w_out.astype(BF16),
        "l0_norm_post": l0_norm_post, "l1_norm_pre": l1_norm_pre,
        "l1_w_main": l1_w_in.astype(BF16), "l1_w_gate": w_gate, "l1_b_gate": b_gate,
        "l1_head_norm": l1_head_norm, "l1_w_out": l1_w_out.astype(BF16), "l1_norm_post": l1_norm_post,
        "rope": _rope_tables(x_prompt.shape[1]),
    }
    return (_trunk(x_prompt, p), _trunk(x_sample, p))
```

```python
import functools

import jax
import jax.numpy as jnp
from jax import lax
from jax.experimental import pallas as pl
from jax.experimental.pallas import tpu as pltpu

F32 = jnp.float32
BF16 = jnp.bfloat16

D_MODEL = 2048
ATT_GROUPS = ((128, 1), (512, 4), (2048, 16))
N_GROUPS = len(ATT_GROUPS)
ATT_HEADS = 16
ATT_HEAD_DIM = 128
ATT_WIDTH = ATT_HEADS * ATT_HEAD_DIM
ATT_IN_COLS = 3 * N_GROUPS * ATT_WIDTH + ATT_WIDTH
ATT_HALF = 64
ROPE_THETA = 500000.0
ROPE_DIMS = ATT_HEAD_DIM // 4
ROPE_HALF = ROPE_DIMS // 2
LOG2E = 1.4426950408889634
ATT_Q_SCALE = ATT_HEAD_DIM ** -0.5 * LOG2E

M_HEADS = 8
M_QK_DIM = 256
M_V_DIM = 512
M_QK_WIDTH = M_HEADS * M_QK_DIM
M_V_WIDTH = M_HEADS * M_V_DIM
M_MAIN_COLS = 2 * M_QK_WIDTH + 3 * M_V_WIDTH
M_CHUNK = 128
GATE_LANES = 128

NORM_EPS = 1e-6
NEG_INF = -1e30

LANES = 128
VMEM_CAP_BYTES = 56 * 1024 * 1024


def _vmem_limit(block_bytes, scratch_bytes=0, temp_bytes=0):
    need = 2 * block_bytes + scratch_bytes + temp_bytes
    return int(min(max(need, 16 * 1024 * 1024), VMEM_CAP_BYTES))


def _nbytes(shape, dtype):
    n = 1
    for s in shape:
        n *= s
    return n * jnp.dtype(dtype).itemsize


PERM_BLOCK = 1024


def _rmsnorm_orders_kernel(x_ref, g_ref, o_ref, slab_scr):
    x = x_ref[...]
    inv = lax.rsqrt(jnp.mean(x * x, axis=-1, keepdims=True) + NORM_EPS)
    nslab = x_ref.shape[1] // LANES
    for c in range(nslab):
        ls = slice(c * LANES, (c + 1) * LANES)
        y = x_ref[:, ls] * inv * g_ref[:, ls]
        slab_scr[c] = y
        for g, (_, dil) in enumerate(ATT_GROUPS):
            if dil == 1:
                o_ref[g, :, ls] = y.astype(o_ref.dtype)
    for g, (_, dil) in enumerate(ATT_GROUPS):
        if dil == 1:
            continue
        n = PERM_BLOCK // dil
        for r in range(dil):
            for c in range(nslab):
                o_ref[g, r * n:(r + 1) * n, c * LANES:(c + 1) * LANES] = (
                    slab_scr[c, pl.ds(r, n, stride=dil), :].astype(o_ref.dtype))


def _rmsnorm_orders(x2d, gain):
    t, d = x2d.shape
    tm = PERM_BLOCK
    return pl.pallas_call(
        _rmsnorm_orders_kernel,
        grid=(t // tm,),
        in_specs=[pl.BlockSpec((tm, d), lambda i: (i, 0)), pl.BlockSpec((1, d), lambda i: (0, 0))],
        out_specs=pl.BlockSpec((N_GROUPS, tm, d), lambda i: (0, i, 0)),
        out_shape=jax.ShapeDtypeStruct((N_GROUPS, t, d), BF16),
        scratch_shapes=[pltpu.VMEM((d // LANES, tm, LANES), F32)],
        compiler_params=pltpu.CompilerParams(
            dimension_semantics=("arbitrary",),
            vmem_limit_bytes=_vmem_limit(_nbytes((tm, d), F32) + _nbytes((N_GROUPS, tm, d), BF16),
                                         _nbytes((tm, d), F32), _nbytes((tm, d), F32)),
        ),
        name="rmsnorm_orders",
    )(x2d, gain.reshape(1, d).astype(F32))


def _order_index(seq, dil):
    n = PERM_BLOCK // dil
    return jnp.arange(seq, dtype=jnp.int32).reshape(seq // PERM_BLOCK, n, dil).transpose(0, 2, 1).reshape(seq)


INPROJ0_TN = 2048
INPROJ0_CHUNK = 256
INPROJ0_TAIL_ROWS = ((0, 512), (512, 768), (768, 1024))
_GROUP_TILES = 3 * ATT_WIDTH // INPROJ0_TN
_Z_TILES = ATT_WIDTH // INPROJ0_TN


def _inproj0_col_tile(jj):
    return jnp.where(jj < _GROUP_TILES, jj,
                     jnp.where(jj < _GROUP_TILES + _Z_TILES, jj + (N_GROUPS - 1) * _GROUP_TILES, jj - _Z_TILES))


def _inproj0_order(jj):
    return jnp.where(jj < _GROUP_TILES + _Z_TILES, 0, (jj - _Z_TILES) // _GROUP_TILES)


def _inproj0_kernel(h_ref, w_ref, c_ref, s1_ref, s2_ref, o_ref):
    col = _inproj0_col_tile(pl.program_id(1))
    seg = (col * INPROJ0_TN) // ATT_WIDTH
    kind = seg % 3
    is_rope = jnp.logical_and(seg < 3 * N_GROUPS, kind < 2)
    nchunk = INPROJ0_TN // INPROJ0_CHUNK
    tm = o_ref.shape[0]

    def pieces():
        for cc in range(nchunk):
            cols = slice(cc * INPROJ0_CHUNK, (cc + 1) * INPROJ0_CHUNK)
            if cc < nchunk - 1:
                yield slice(0, tm), cols
            else:
                for lo, hi in INPROJ0_TAIL_ROWS:
                    yield slice(lo, hi), cols

    @pl.when(is_rope)
    def _():
        scale = jnp.where(kind == 0, ATT_Q_SCALE, 1.0).astype(F32)
        c_all = c_ref[...] * scale
        s1_all = s1_ref[...] * scale
        s2_all = s2_ref[...] * scale
        for rows, cols in pieces():
            c, s1, s2 = c_all[rows], s1_all[rows], s2_all[rows]
            acc = jnp.dot(h_ref[rows, :], w_ref[:, cols], preferred_element_type=F32)
            for t in range(INPROJ0_CHUNK // LANES):
                a = acc[:, t * LANES:(t + 1) * LANES]
                r = a * c + pltpu.roll(a, LANES - ROPE_HALF, 1) * s1 + pltpu.roll(a, ROPE_HALF, 1) * s2
                lo = cols.start + t * LANES
                o_ref[rows, lo:lo + LANES] = r.astype(o_ref.dtype)

    @pl.when(jnp.logical_not(is_rope))
    def _():
        for rows, cols in pieces():
            o_ref[rows, cols] = jnp.dot(h_ref[rows, :], w_ref[:, cols], preferred_element_type=F32).astype(o_ref.dtype)


def _rope_tables(seq):
    inv = jnp.power(ROPE_THETA, -jnp.arange(ROPE_HALF, dtype=F32) / ROPE_HALF)
    ang = jnp.arange(seq, dtype=F32)[:, None] * inv[None, :]
    cos, sin = jnp.cos(ang), jnp.sin(ang)
    zeros = jnp.zeros((seq, LANES - ROPE_DIMS), F32)
    zh = jnp.zeros((seq, ROPE_HALF), F32)
    c = jnp.concatenate([cos, cos, jnp.ones((seq, LANES - ROPE_DIMS), F32)], axis=1)
    s1 = jnp.concatenate([-sin, zh, zeros], axis=1)
    s2 = jnp.concatenate([zh, sin, zeros], axis=1)
    orders = [_order_index(seq, dil) for _, dil in ATT_GROUPS]
    return tuple(jnp.stack([tab[idx] for idx in orders]) for tab in (c, s1, s2))


def _inproj0(h_orders, w, tables, seq):
    _, t, k = h_orders.shape
    n = w.shape[1]
    tm, tn = PERM_BLOCK, INPROJ0_TN
    pos_blocks = seq // tm
    tab_spec = pl.BlockSpec((None, tm, LANES), lambda i, j: (_inproj0_order(j), i % pos_blocks, 0))
    blocks = _nbytes((tm, k), BF16) + _nbytes((k, tn), BF16) + _nbytes((tm, tn), BF16) + 3 * _nbytes((tm, LANES), F32)
    return pl.pallas_call(
        _inproj0_kernel,
        grid=(t // tm, n // tn),
        in_specs=[
            pl.BlockSpec((None, tm, k), lambda i, j: (_inproj0_order(j), i, 0)),
            pl.BlockSpec((k, tn), lambda i, j: (0, _inproj0_col_tile(j))),
            tab_spec, tab_spec, tab_spec,
        ],
        out_specs=pl.BlockSpec((tm, tn), lambda i, j: (i, _inproj0_col_tile(j))),
        out_shape=jax.ShapeDtypeStruct((t, n), BF16),
        compiler_params=pltpu.CompilerParams(
            dimension_semantics=("arbitrary", "arbitrary"),
            vmem_limit_bytes=_vmem_limit(blocks, 0, _nbytes((tm, k), BF16) + 3 * _nbytes((tm, LANES), F32)
                                         + 4 * _nbytes((tm, INPROJ0_CHUNK), F32)),
        ),
        name="inproj0",
    )(h_orders, w, *tables)


ATT_TQ = 512


def _attn_kernel(q_ref, kp_ref, kc_ref, kn_ref, vp_ref, vc_ref, vn_ref, o_ref, l_ref, qx, kx, vx, ox, lx, *, tq, ls):
    hb = ATT_HALF
    sub = 2 * hb
    w = qx.shape[2]
    rps = qx.shape[0]
    i = pl.program_id(2)
    bpt, n = q_ref.shape[1], q_ref.shape[3]

    row = lax.broadcasted_iota(jnp.int32, (sub, 2 * sub), 0)
    col = lax.broadcasted_iota(jnp.int32, (sub, 2 * sub), 1)
    band = jnp.abs(col - hb - row) <= hb
    lane = lax.broadcasted_iota(jnp.int32, (sub, LANES), 1)

    for rr in range(rps):
        qx[rr] = q_ref[0, :, rr].reshape(tq, w)
        kx[rr, 0:hb, :] = kp_ref[0, 0, rr]
        kx[rr, hb:hb + tq, :] = kc_ref[0, :, rr].reshape(tq, w)
        kx[rr, hb + tq:, :] = kn_ref[0, 0, rr]
        vx[rr, 0:hb, :] = vp_ref[0, 0, rr]
        vx[rr, hb:hb + tq, :] = vc_ref[0, :, rr].reshape(tq, w)
        vx[rr, hb + tq:, :] = vn_ref[0, 0, rr]

    for rr in range(rps):
        for a in range(tq // sub):
            r0 = a * sub
            kidx = i * tq + r0 + col - hb
            valid = band & (kidx >= 0) & (kidx < ls)
            lse_tile = jnp.zeros((sub, LANES), F32)
            for h in range(ATT_HEADS):
                hs = slice(h * ATT_HEAD_DIM, (h + 1) * ATT_HEAD_DIM)
                qh = qx[rr, r0:r0 + sub, hs]
                kh = kx[rr, r0:r0 + 2 * sub, hs]
                vh = vx[rr, r0:r0 + 2 * sub, hs]
                s = lax.dot_general(qh, kh, (((1,), (1,)), ((), ())), preferred_element_type=F32)
                s = jnp.where(valid, s, NEG_INF)
                m = jnp.max(s, axis=-1, keepdims=True)
                p = jnp.exp2(s - m)
                den = jnp.sum(p, axis=-1, keepdims=True)
                o = jnp.dot(p.astype(BF16), vh, preferred_element_type=F32) * (1.0 / den)
                lse_tile = jnp.where(lane == h, m + jnp.log2(den), lse_tile)
                ox[rr, r0:r0 + sub, hs] = o.astype(ox.dtype)
            lx[rr, r0:r0 + sub, :] = lse_tile
        o_ref[0, :, rr] = ox[rr].reshape(bpt, n, w)
        l_ref[0, :, rr] = lx[rr].reshape(bpt, n, LANES)


def _attention_group(proj, g, bsz, seq):
    _, dil = ATT_GROUPS[g]
    w = ATT_WIDTH
    hb = ATT_HALF
    ls = seq // dil
    if dil == 1:
        n = min(ATT_TQ, 256)
        nb = seq // n
    else:
        nb, n = seq // PERM_BLOCK, PERM_BLOCK // dil
    tq = min(ATT_TQ, ls)
    bpt = tq // n
    rps = min(ATT_TQ // tq, dil)
    assert tq % (2 * hb) == 0 and n % hb == 0 and tq % n == 0 and nb % bpt == 0 and dil % rps == 0
    pv = proj.reshape(bsz, nb, dil, n, ATT_IN_COLS)
    ncb = 3 * g

    grid = (bsz, dil // rps, nb // bpt)
    main = lambda cb: pl.BlockSpec((1, bpt, rps, n, w), lambda b, r, i: (b, i, r, 0, cb))
    prev_h = lambda cb: pl.BlockSpec((1, 1, rps, hb, w),
                                     lambda b, r, i: (b, jnp.maximum(i * bpt - 1, 0), r, n // hb - 1, cb))
    next_h = lambda cb: pl.BlockSpec((1, 1, rps, hb, w),
                                     lambda b, r, i: (b, jnp.minimum((i + 1) * bpt, nb - 1), r, 0, cb))
    out_block = lambda width: pl.BlockSpec((1, bpt, rps, n, width), lambda b, r, i: (b, i, r, 0, 0))

    in_specs = [main(ncb), prev_h(ncb + 1), main(ncb + 1), next_h(ncb + 1),
                prev_h(ncb + 2), main(ncb + 2), next_h(ncb + 2)]
    blocks = rps * (4 * _nbytes((tq, w), BF16) + 4 * _nbytes((hb, w), BF16) + _nbytes((tq, LANES), F32))
    scratch = [pltpu.VMEM((rps, tq, w), BF16), pltpu.VMEM((rps, tq + 2 * hb, w), BF16),
               pltpu.VMEM((rps, tq + 2 * hb, w), BF16), pltpu.VMEM((rps, tq, w), BF16),
               pltpu.VMEM((rps, tq, LANES), F32)]
    scratch_bytes = rps * (4 * _nbytes((tq + 2 * hb, w), BF16) + _nbytes((tq, LANES), F32))
    o, l = pl.pallas_call(
        functools.partial(_attn_kernel, tq=tq, ls=ls),
        grid=grid,
        in_specs=in_specs,
        out_specs=(out_block(w), out_block(LANES)),
        out_shape=(jax.ShapeDtypeStruct((bsz, nb, dil, n, w), BF16),
                   jax.ShapeDtypeStruct((bsz, nb, dil, n, LANES), F32)),
        scratch_shapes=scratch,
        compiler_params=pltpu.CompilerParams(
            dimension_semantics=("arbitrary", "arbitrary", "arbitrary"),
            vmem_limit_bytes=_vmem_limit(blocks, scratch_bytes, 8 * 1024 * 1024),
        ),
        name=f"attn_g{g}",
    )(*([pv] * 7))
    if dil > 1:
        o = o.transpose(0, 1, 3, 2, 4)
        l = l.transpose(0, 1, 3, 2, 4)
    return o.reshape(bsz * seq, w), l.reshape(bsz * seq, LANES)


def _post_norm_residual(out, x_ref, gpost_ref, o_ref, gnext_ref=None, hnext_ref=None):
    ms = jnp.mean(out * out, axis=-1, keepdims=True)
    x1 = x_ref[...] + out * lax.rsqrt(ms + NORM_EPS) * gpost_ref[...]
    o_ref[...] = x1
    if hnext_ref is not None:
        ms1 = jnp.mean(x1 * x1, axis=-1, keepdims=True)
        hnext_ref[...] = (x1 * lax.rsqrt(ms1 + NORM_EPS) * gnext_ref[...]).astype(hnext_ref.dtype)


OUTPROJ_KCHUNK = 512


def _sigmoid(x):
    return 0.5 * jnp.tanh(0.5 * x) + 0.5


def _silu(x):
    xh = 0.5 * x
    return xh * jnp.tanh(xh) + xh


def _outproj0_kernel(o0_ref, o1_ref, o2_ref, l0_ref, l1_ref, l2_ref, z_ref, w_ref, x_ref, gpost_ref, gnext_ref,
                     x1_ref, hnext_ref):
    o_refs = (o0_ref, o1_ref, o2_ref)
    lses = [r[...] for r in (l0_ref, l1_ref, l2_ref)]
    lmax = jnp.maximum(jnp.maximum(lses[0], lses[1]), lses[2])
    es = [jnp.exp2(l - lmax) for l in lses]
    tot = es[0] + es[1] + es[2]
    wts = [e / tot for e in es]
    heads_per_chunk = OUTPROJ_KCHUNK // ATT_HEAD_DIM
    out = None
    for c in range(ATT_HEADS // heads_per_chunk):
        ys = []
        for h in range(c * heads_per_chunk, (c + 1) * heads_per_chunk):
            hs = slice(h * ATT_HEAD_DIM, (h + 1) * ATT_HEAD_DIM)
            o = wts[0][:, h:h + 1] * o_refs[0][:, hs].astype(F32)
            for g in range(1, N_GROUPS):
                o = o + wts[g][:, h:h + 1] * o_refs[g][:, hs].astype(F32)
            ys.append((o * _silu(z_ref[:, hs].astype(F32))).astype(BF16))
        part = jnp.dot(jnp.concatenate(ys, axis=1), w_ref[c * OUTPROJ_KCHUNK:(c + 1) * OUTPROJ_KCHUNK, :],
                       preferred_element_type=F32)
        out = part if out is None else out + part
    _post_norm_residual(out, x_ref, gpost_ref, x1_ref, gnext_ref, hnext_ref)


def _outproj0(outs, lses, proj0, w, x2d, g_post, g_next, tm=256):
    t, d = x2d.shape
    k = w.shape[0]
    row = lambda i: (i, 0)
    vec = pl.BlockSpec((1, d), lambda i: (0, 0))
    zb = 3 * N_GROUPS
    in_specs = ([pl.BlockSpec((tm, k), row)] * N_GROUPS + [pl.BlockSpec((tm, LANES), row)] * N_GROUPS
                + [pl.BlockSpec((tm, k), lambda i: (i, zb)),
                   pl.BlockSpec((k, d), lambda i: (0, 0), pipeline_mode=pl.Buffered(1)),
                   pl.BlockSpec((tm, d), row), vec, vec])
    blocks = ((N_GROUPS + 1) * _nbytes((tm, k), BF16) + N_GROUPS * _nbytes((tm, LANES), F32)
              + 2 * _nbytes((tm, d), F32) + _nbytes((tm, d), BF16))
    return pl.pallas_call(
        _outproj0_kernel,
        grid=(t // tm,),
        in_specs=in_specs,
        out_specs=(pl.BlockSpec((tm, d), row), pl.BlockSpec((tm, d), row)),
        out_shape=(jax.ShapeDtypeStruct((t, d), F32), jax.ShapeDtypeStruct((t, d), BF16)),
        compiler_params=pltpu.CompilerParams(
            dimension_semantics=("arbitrary",),
            vmem_limit_bytes=_vmem_limit(blocks, _nbytes((k, d), BF16), 4 * _nbytes((tm, d), F32)),
        ),
        name="outproj0",
    )(*outs, *lses, proj0, w, x2d, g_post.reshape(1, d).astype(F32), g_next.reshape(1, d).astype(F32))


def _outproj1_kernel(hs_ref, og_ref, z_ref, hgain_ref, w_ref, x_ref, gpost_ref, o_ref):
    assert OUTPROJ_KCHUNK == M_V_DIM
    out = None
    for c in range(hs_ref.shape[1] // OUTPROJ_KCHUNK):
        ks = slice(c * OUTPROJ_KCHUNK, (c + 1) * OUTPROJ_KCHUNK)
        hs = hs_ref[:, ks].astype(F32)
        hn = hs * lax.rsqrt(jnp.mean(hs * hs, axis=-1, keepdims=True) + NORM_EPS) * hgain_ref[:, ks]
        y = (og_ref[:, ks].astype(F32) * hn) * z_ref[:, ks].astype(F32)
        part = jnp.dot(y.astype(BF16), w_ref[ks, :], preferred_element_type=F32)
        out = part if out is None else out + part
    _post_norm_residual(out, x_ref, gpost_ref, o_ref)


def _outproj1(hs, proj1, head_gain, w, x2d, g_post, tm=256):
    t, d = x2d.shape
    k = w.shape[0]
    row = lambda i: (i, 0)
    ob = M_OGATE_COL // k
    blocks = 3 * _nbytes((tm, k), BF16) + _nbytes((1, k), F32) + 2 * _nbytes((tm, d), F32)
    return pl.pallas_call(
        _outproj1_kernel,
        grid=(t // tm,),
        in_specs=[pl.BlockSpec((tm, k), row), pl.BlockSpec((tm, k), lambda i: (i, ob)),
                  pl.BlockSpec((tm, k), lambda i: (i, ob + 1)),
                  pl.BlockSpec((1, k), lambda i: (0, 0)),
                  pl.BlockSpec((k, d), lambda i: (0, 0), pipeline_mode=pl.Buffered(1)),
                  pl.BlockSpec((tm, d), row), pl.BlockSpec((1, d), lambda i: (0, 0))],
        out_specs=pl.BlockSpec((tm, d), row),
        out_shape=jax.ShapeDtypeStruct((t, d), F32),
        compiler_params=pltpu.CompilerParams(
            dimension_semantics=("arbitrary",),
            vmem_limit_bytes=_vmem_limit(blocks, _nbytes((k, d), BF16), 4 * _nbytes((tm, k), F32)),
        ),
        name="outproj1",
    )(hs, proj1, proj1, head_gain.reshape(1, k).astype(F32), w, x2d, g_post.reshape(1, d).astype(F32))


INPROJ1_CHUNK = 512


M_OGATE_COL = 2 * M_QK_WIDTH + M_V_WIDTH
M_ZGATE_COL = M_OGATE_COL + M_V_WIDTH


def _inproj1_kernel(h_ref, w_ref, wg_ref, o_ref, g_ref):
    tn = o_ref.shape[1]
    col0 = pl.program_id(1) * tn

    def tile(act):
        for cc in range(tn // INPROJ1_CHUNK):
            cs = slice(cc * INPROJ1_CHUNK, (cc + 1) * INPROJ1_CHUNK)
            acc = jnp.dot(h_ref[...], w_ref[:, cs], preferred_element_type=F32)
            o_ref[:, cs] = (acc if act is None else act(acc)).astype(o_ref.dtype)

    pl.when(col0 < M_OGATE_COL)(lambda: tile(None))
    pl.when(jnp.logical_and(col0 >= M_OGATE_COL, col0 < M_ZGATE_COL))(lambda: tile(_sigmoid))
    pl.when(col0 >= M_ZGATE_COL)(lambda: tile(_silu))

    @pl.when(pl.program_id(1) == 0)
    def _():
        g_ref[...] = jnp.dot(h_ref[...], wg_ref[...], preferred_element_type=F32)


def _inproj1(h, w, wg, tm=1024, tn=2048):
    t, k = h.shape
    n = M_MAIN_COLS
    tm = min(tm, t)
    blocks = (_nbytes((tm, k), BF16) + _nbytes((k, tn), BF16) + _nbytes((k, GATE_LANES), BF16)
              + _nbytes((tm, tn), BF16) + _nbytes((tm, GATE_LANES), F32))
    return pl.pallas_call(
        _inproj1_kernel,
        grid=(t // tm, n // tn),
        in_specs=[
            pl.BlockSpec((tm, k), lambda i, j: (i, 0)),
            pl.BlockSpec((k, tn), lambda i, j: (0, j)),
            pl.BlockSpec((k, GATE_LANES), lambda i, j: (0, 0)),
        ],
        out_specs=(pl.BlockSpec((tm, tn), lambda i, j: (i, j)), pl.BlockSpec((tm, GATE_LANES), lambda i, j: (i, 0))),
        out_shape=(jax.ShapeDtypeStruct((t, n), BF16), jax.ShapeDtypeStruct((t, GATE_LANES), F32)),
        compiler_params=pltpu.CompilerParams(
            dimension_semantics=("arbitrary", "arbitrary"),
            vmem_limit_bytes=_vmem_limit(blocks, 0, 4 * _nbytes((tm, INPROJ1_CHUNK), F32)),
        ),
        name="inproj1",
    )(h, w, wg)


def _scan_rows(x, rows, op, fill, reverse):
    n = x.shape[0]
    k = 1
    while k < n:
        if reverse:
            shifted = jnp.where(rows < n - k, pltpu.roll(x, n - k, 0), fill)
        else:
            shifted = jnp.where(rows >= k, pltpu.roll(x, k, 0), fill)
        x = op(x, shifted)
        k *= 2
    return x


GATE_PREP_CHUNKS = 4


def _gate_prep_kernel(g_ref, b_ref, o_ref):
    for ci in range(GATE_PREP_CHUNKS):
        _gate_prep_chunk(g_ref[0, ci * M_CHUNK:(ci + 1) * M_CHUNK, :] + b_ref[...], o_ref, ci)


def _gate_prep_chunk(g, o_ref, ci):
    rows = lax.broadcasted_iota(jnp.int32, g.shape, 0)
    kind = lax.broadcasted_iota(jnp.int32, g.shape, 1) % 8
    lf = jnp.minimum(g, 0.0) - jnp.log1p(jnp.exp(-jnp.abs(g)))
    csum = _scan_rows(lf, rows, jnp.add, 0.0, reverse=False)
    rsum = _scan_rows(lf, rows, jnp.add, 0.0, reverse=True)
    b_at_i = pltpu.roll(jnp.where(kind == 1, csum, rsum), LANES - 1, 1)
    grow = g - b_at_i
    cm_f = _scan_rows(grow, rows, jnp.maximum, NEG_INF, reverse=False)
    cm_b = _scan_rows(grow, rows, jnp.maximum, NEG_INF, reverse=True)
    y = jnp.where(kind == 0, b_at_i, 0.0)
    y = jnp.where(kind == 1, pltpu.roll(grow, 1, 1), y)
    y = jnp.where(kind == 2, pltpu.roll(cm_f, 2, 1), y)
    y = jnp.where(kind == 3, rsum, y)
    y = jnp.where(kind == 4, pltpu.roll(grow, 2, 1), y)
    y = jnp.where(kind == 5, pltpu.roll(cm_b, 3, 1), y)
    yt = y.T
    o_ref[0, :, ci] = yt[:M_HEADS * 8].reshape(M_HEADS, 8, g.shape[0])


def _gate_prep(gates, bias, bsz, seq):
    nc = seq // M_CHUNK
    per = GATE_PREP_CHUNKS
    return pl.pallas_call(
        _gate_prep_kernel,
        grid=(bsz, nc // per),
        in_specs=[pl.BlockSpec((1, per * M_CHUNK, GATE_LANES), lambda b, c: (b, c, 0)),
                  pl.BlockSpec((1, GATE_LANES), lambda b, c: (0, 0))],
        out_specs=pl.BlockSpec((1, M_HEADS, per, 8, M_CHUNK), lambda b, c: (b, 0, c, 0, 0)),
        out_shape=jax.ShapeDtypeStruct((bsz, M_HEADS, nc, 8, M_CHUNK), F32),
        compiler_params=pltpu.CompilerParams(dimension_semantics=("arbitrary", "arbitrary")),
        name="gate_prep",
    )(gates.reshape(bsz, seq, GATE_LANES), bias)


def _mlstm_kernel(q_ref, k_ref, v_ref, r_ref, o_ref, hs_scr, cf_scr, cb_scr, *, nc):
    L = M_CHUNK
    rows = lax.broadcasted_iota(jnp.int32, (L, L), 0)
    cols = lax.broadcasted_iota(jnp.int32, (L, L), 1)
    eye = rows == cols
    causal = cols <= rows
    anti = cols >= rows

    def to_col(r):
        return jnp.sum(jnp.where(eye, r, 0.0), axis=1, keepdims=True)

    def chunk(c, c_scr, n, m, fwd):
        st = pl.multiple_of(c * L, L)
        qc = q_ref[0, pl.ds(st, L), :] * (M_QK_DIM ** -0.5)
        kc = k_ref[0, pl.ds(st, L), :]
        vc = v_ref[0, pl.ds(st, L), :]
        r = r_ref[0, 0, pl.ds(c, 1)].reshape(8, L)
        base = 0 if fwd else 3
        grow = r[base + 1:base + 2]
        bcol = to_col(r[base:base + 1])
        gcol = to_col(grow)
        mm = jnp.maximum(to_col(r[base + 2:base + 3]), m)
        e = L - 1 if fwd else 0
        mm_l = mm[e:e + 1]
        s = lax.dot_general(qc, kc, (((1,), (1,)), ((), ())), preferred_element_type=F32)
        a = jnp.exp(jnp.where(causal if fwd else anti, grow - mm, NEG_INF)) * s
        gint = jnp.exp(m - mm)
        cmat = c_scr[...]
        num = (jnp.dot(a.astype(BF16), vc, preferred_element_type=F32)
               + gint * jnp.dot(qc, cmat.astype(BF16), preferred_element_type=F32))
        qn = jnp.sum(qc.astype(F32) * n, axis=1, keepdims=True)
        den = jnp.sum(a, axis=1, keepdims=True) + gint * qn
        hc = num * (1.0 / jnp.maximum(jnp.abs(den), jnp.exp(-(bcol + mm))))
        decay = jnp.exp(m - mm_l)
        wk = jnp.exp(gcol - mm_l) * kc.astype(F32)
        c_scr[...] = decay * cmat + lax.dot_general(wk.astype(BF16), vc, (((0,), (0,)), ((), ())),
                                                    preferred_element_type=F32)
        n_new = decay * n + jnp.sum(wk, axis=0, keepdims=True)
        m_new = bcol[e:e + 1] + mm_l
        return hc, n_new, m_new

    def finish(c, hsum):
        st = pl.multiple_of(c * L, L)
        o_ref[0, pl.ds(st, L), :] = hsum.astype(o_ref.dtype)

    def step(j, carry, second_half):
        nf, mf, nb, mb = carry
        cf, cb = j, nc - 1 - j
        hf, nf, mf = chunk(cf, cf_scr, nf, mf, True)
        hb, nb, mb = chunk(cb, cb_scr, nb, mb, False)
        sf = pl.multiple_of(cf * L, L)
        sb = pl.multiple_of(cb * L, L)
        if second_half:
            finish(cf, hs_scr[pl.ds(sf, L), :] + hf)
            finish(cb, hs_scr[pl.ds(sb, L), :] + hb)
        else:
            hs_scr[pl.ds(sf, L), :] = hf
            hs_scr[pl.ds(sb, L), :] = hb
        return nf, mf, nb, mb

    cf_scr[...] = jnp.zeros_like(cf_scr)
    cb_scr[...] = jnp.zeros_like(cb_scr)
    n0 = jnp.zeros((1, M_QK_DIM), F32)
    m0 = jnp.zeros((1, 1), F32)
    carry = lax.fori_loop(0, nc // 2, functools.partial(step, second_half=False), (n0, m0, n0, m0), unroll=4)
    lax.fori_loop(nc // 2, nc, functools.partial(step, second_half=True), carry, unroll=4)


def _mlstm(proj1, prep, bsz, seq):
    nc = seq // M_CHUNK
    assert nc % 2 == 0
    p3 = proj1.reshape(bsz, seq, M_MAIN_COLS)
    kb = M_QK_WIDTH // M_QK_DIM
    vb = 2 * M_QK_WIDTH // M_V_DIM
    blocks = (2 * _nbytes((seq, M_QK_DIM), BF16) + 2 * _nbytes((seq, M_V_DIM), BF16)
              + _nbytes((nc, 8, M_CHUNK), F32))
    scratch_bytes = _nbytes((seq, M_V_DIM), F32) + 2 * _nbytes((M_QK_DIM, M_V_DIM), F32)
    return pl.pallas_call(
        functools.partial(_mlstm_kernel, nc=nc),
        grid=(bsz, M_HEADS),
        in_specs=[
            pl.BlockSpec((1, seq, M_QK_DIM), lambda b, h: (b, 0, h)),
            pl.BlockSpec((1, seq, M_QK_DIM), lambda b, h: (b, 0, kb + h)),
            pl.BlockSpec((1, seq, M_V_DIM), lambda b, h: (b, 0, vb + h)),
            pl.BlockSpec((1, 1, nc, 8, M_CHUNK), lambda b, h: (b, h, 0, 0, 0)),
        ],
        out_specs=pl.BlockSpec((1, seq, M_V_DIM), lambda b, h: (b, 0, h)),
        out_shape=jax.ShapeDtypeStruct((bsz, seq, M_V_WIDTH), BF16),
        scratch_shapes=[pltpu.VMEM((seq, M_V_DIM), F32), pltpu.VMEM((M_QK_DIM, M_V_DIM), F32),
                        pltpu.VMEM((M_QK_DIM, M_V_DIM), F32)],
        compiler_params=pltpu.CompilerParams(
            dimension_semantics=("arbitrary", "arbitrary"),
            vmem_limit_bytes=_vmem_limit(blocks, scratch_bytes, 8 * 1024 * 1024),
        ),
        name="mlstm",
    )(p3, p3, p3, prep)


def _gate_weights(w_in1, b_gate):
    wg = w_in1[:, M_MAIN_COLS:].reshape(D_MODEL, 4, M_HEADS).transpose(0, 2, 1)
    wg = jnp.pad(wg, ((0, 0), (0, 0), (0, 4))).reshape(D_MODEL, M_HEADS * 8)
    wg = jnp.pad(wg, ((0, 0), (0, GATE_LANES - M_HEADS * 8)))
    bg = jnp.pad(b_gate.astype(F32).reshape(4, M_HEADS).T, ((0, 0), (0, 4))).reshape(1, M_HEADS * 8)
    bg = jnp.pad(bg, ((0, 0), (0, GATE_LANES - M_HEADS * 8)))
    return wg.astype(BF16), bg


def _trunk(x, p):
    bsz, seq, d = x.shape
    t = bsz * seq
    assert seq % (2 * PERM_BLOCK) == 0
    x2d = x.reshape(t, d)
    h0 = _rmsnorm_orders(x2d, p["l0_norm_pre"])
    proj0 = _inproj0(h0, p["l0_w_in"], p["rope"], seq)
    outs, lses = zip(*[_attention_group(proj0, g, bsz, seq) for g in range(N_GROUPS)])
    x1, h1 = _outproj0(outs, lses, proj0, p["l0_w_out"], x2d, p["l0_norm_post"], p["l1_norm_pre"])
    proj1, gates = _inproj1(h1, p["l1_w_main"], p["l1_w_gate"])
    prep = _gate_prep(gates, p["l1_b_gate"], bsz, seq)
    hs = _mlstm(proj1, prep, bsz, seq)
    y = _outproj1(hs.reshape(t, M_V_WIDTH), proj1, p["l1_head_norm"], p["l1_w_out"], x1, p["l1_norm_post"])
    return y.reshape(bsz, seq, d)


def kernel(x_prompt, x_sample, l0_norm_pre, l0_w_in, l0_w_out, l0_norm_post,
           l1_norm_pre, l1_w_in, l1_b_gate, l1_head_norm, l1_w_out, l1_norm_post):
    assert x_prompt.shape[1] == x_sample.shape[1]
    w_gate, b_gate = _gate_weights(l1_w_in, l1_b_gate)
    p = {
        "l0_norm_pre": l0_norm_pre, "l0_w_in": l0_w_in.astype(BF16), "l0_w_out": l0_w_out.astype(BF16),
        "l0_norm_post": l0_norm_post, "l1_norm_pre": l1_norm_pre,
        "l1_w_main": l1_w_in.astype(BF16), "l1_w_gate": w_gate, "l1_b_gate": b_gate,
        "l1_head_norm": l1_head_norm, "l1_w_out": l1_w_out.astype(BF16), "l1_norm_post": l1_norm_post,
        "rope": _rope_tables(x_prompt.shape[1]),
    }
    return (_trunk(x_prompt, p), _trunk(x_sample, p))
```

```python
import functools

import jax
import jax.numpy as jnp
from jax import lax
from jax.experimental import pallas as pl
from jax.experimental.pallas import tpu as pltpu

F32 = jnp.float32
BF16 = jnp.bfloat16

D_MODEL = 2048
ATT_GROUPS = ((128, 1), (512, 4), (2048, 16))
N_GROUPS = len(ATT_GROUPS)
ATT_HEADS = 16
ATT_HEAD_DIM = 128
ATT_WIDTH = ATT_HEADS * ATT_HEAD_DIM
ATT_IN_COLS = 3 * N_GROUPS * ATT_WIDTH + ATT_WIDTH
ATT_HALF = 64
ROPE_THETA = 500000.0
ROPE_DIMS = ATT_HEAD_DIM // 4
ROPE_HALF = ROPE_DIMS // 2
LOG2E = 1.4426950408889634
ATT_Q_SCALE = ATT_HEAD_DIM ** -0.5 * LOG2E

M_HEADS = 8
M_QK_DIM = 256
M_V_DIM = 512
M_QK_WIDTH = M_HEADS * M_QK_DIM
M_V_WIDTH = M_HEADS * M_V_DIM
M_MAIN_COLS = 2 * M_QK_WIDTH + 3 * M_V_WIDTH
M_CHUNK = 256
GATE_LANES = 128

NORM_EPS = 1e-6
NEG_INF = -1e30

LANES = 128
VMEM_CAP_BYTES = 56 * 1024 * 1024


def _vmem_limit(block_bytes, scratch_bytes=0, temp_bytes=0):
    need = 2 * block_bytes + scratch_bytes + temp_bytes
    return int(min(max(need, 16 * 1024 * 1024), VMEM_CAP_BYTES))


def _nbytes(shape, dtype):
    n = 1
    for s in shape:
        n *= s
    return n * jnp.dtype(dtype).itemsize


PERM_BLOCK = 1024


def _rmsnorm_orders_kernel(x_ref, g_ref, o_ref, slab_scr):
    x = x_ref[...]
    inv = lax.rsqrt(jnp.mean(x * x, axis=-1, keepdims=True) + NORM_EPS)
    nslab = x_ref.shape[1] // LANES
    for c in range(nslab):
        ls = slice(c * LANES, (c + 1) * LANES)
        y = x_ref[:, ls] * inv * g_ref[:, ls]
        slab_scr[c] = y
        for g, (_, dil) in enumerate(ATT_GROUPS):
            if dil == 1:
                o_ref[g, :, ls] = y.astype(o_ref.dtype)
    for g, (_, dil) in enumerate(ATT_GROUPS):
        if dil == 1:
            continue
        n = PERM_BLOCK // dil
        for r in range(dil):
            for c in range(nslab):
                o_ref[g, r * n:(r + 1) * n, c * LANES:(c + 1) * LANES] = (
                    slab_scr[c, pl.ds(r, n, stride=dil), :].astype(o_ref.dtype))


def _rmsnorm_orders(x2d, gain):
    t, d = x2d.shape
    tm = PERM_BLOCK
    return pl.pallas_call(
        _rmsnorm_orders_kernel,
        grid=(t // tm,),
        in_specs=[pl.BlockSpec((tm, d), lambda i: (i, 0)), pl.BlockSpec((1, d), lambda i: (0, 0))],
        out_specs=pl.BlockSpec((N_GROUPS, tm, d), lambda i: (0, i, 0)),
        out_shape=jax.ShapeDtypeStruct((N_GROUPS, t, d), BF16),
        scratch_shapes=[pltpu.VMEM((d // LANES, tm, LANES), F32)],
        compiler_params=pltpu.CompilerParams(
            dimension_semantics=("arbitrary",),
            vmem_limit_bytes=_vmem_limit(_nbytes((tm, d), F32) + _nbytes((N_GROUPS, tm, d), BF16),
                                         _nbytes((tm, d), F32), _nbytes((tm, d), F32)),
        ),
        name="rmsnorm_orders",
    )(x2d, gain.reshape(1, d).astype(F32))


def _order_index(seq, dil):
    n = PERM_BLOCK // dil
    return jnp.arange(seq, dtype=jnp.int32).reshape(seq // PERM_BLOCK, n, dil).transpose(0, 2, 1).reshape(seq)


INPROJ0_TN = 2048
INPROJ0_CHUNK = 256
INPROJ0_TAIL_ROWS = ((0, 512), (512, 768), (768, 1024))
_GROUP_TILES = 3 * ATT_WIDTH // INPROJ0_TN
_Z_TILES = ATT_WIDTH // INPROJ0_TN


def _inproj0_col_tile(jj):
    return jnp.where(jj < _GROUP_TILES, jj,
                     jnp.where(jj < _GROUP_TILES + _Z_TILES, jj + (N_GROUPS - 1) * _GROUP_TILES, jj - _Z_TILES))


def _inproj0_order(jj):
    return jnp.where(jj < _GROUP_TILES + _Z_TILES, 0, (jj - _Z_TILES) // _GROUP_TILES)


def _inproj0_kernel(h_ref, w_ref, c_ref, s1_ref, s2_ref, o_ref):
    col = _inproj0_col_tile(pl.program_id(1))
    seg = (col * INPROJ0_TN) // ATT_WIDTH
    kind = seg % 3
    is_rope = jnp.logical_and(seg < 3 * N_GROUPS, kind < 2)
    nchunk = INPROJ0_TN // INPROJ0_CHUNK
    tm = o_ref.shape[0]

    def pieces():
        for cc in range(nchunk):
            cols = slice(cc * INPROJ0_CHUNK, (cc + 1) * INPROJ0_CHUNK)
            if cc < nchunk - 1:
                yield slice(0, tm), cols
            else:
                for lo, hi in INPROJ0_TAIL_ROWS:
                    yield slice(lo, hi), cols

    @pl.when(is_rope)
    def _():
        scale = jnp.where(kind == 0, ATT_Q_SCALE, 1.0).astype(F32)
        c_all = c_ref[...] * scale
        s1_all = s1_ref[...] * scale
        s2_all = s2_ref[...] * scale
        for rows, cols in pieces():
            c, s1, s2 = c_all[rows], s1_all[rows], s2_all[rows]
            acc = jnp.dot(h_ref[rows, :], w_ref[:, cols], preferred_element_type=F32)
            for t in range(INPROJ0_CHUNK // LANES):
                a = acc[:, t * LANES:(t + 1) * LANES]
                r = a * c + pltpu.roll(a, LANES - ROPE_HALF, 1) * s1 + pltpu.roll(a, ROPE_HALF, 1) * s2
                lo = cols.start + t * LANES
                o_ref[rows, lo:lo + LANES] = r.astype(o_ref.dtype)

    @pl.when(jnp.logical_not(is_rope))
    def _():
        for rows, cols in pieces():
            o_ref[rows, cols] = jnp.dot(h_ref[rows, :], w_ref[:, cols], preferred_element_type=F32).astype(o_ref.dtype)


def _rope_tables(seq):
    inv = jnp.power(ROPE_THETA, -jnp.arange(ROPE_HALF, dtype=F32) / ROPE_HALF)
    ang = jnp.arange(seq, dtype=F32)[:, None] * inv[None, :]
    cos, sin = jnp.cos(ang), jnp.sin(ang)
    zeros = jnp.zeros((seq, LANES - ROPE_DIMS), F32)
    zh = jnp.zeros((seq, ROPE_HALF), F32)
    c = jnp.concatenate([cos, cos, jnp.ones((seq, LANES - ROPE_DIMS), F32)], axis=1)
    s1 = jnp.concatenate([-sin, zh, zeros], axis=1)
    s2 = jnp.concatenate([zh, sin, zeros], axis=1)
    orders = [_order_index(seq, dil) for _, dil in ATT_GROUPS]
    return tuple(jnp.stack([tab[idx] for idx in orders]) for tab in (c, s1, s2))


def _inproj0(h_orders, w, tables, seq):
    _, t, k = h_orders.shape
    n = w.shape[1]
    tm, tn = PERM_BLOCK, INPROJ0_TN
    pos_blocks = seq // tm
    tab_spec = pl.BlockSpec((None, tm, LANES), lambda i, j: (_inproj0_order(j), i % pos_blocks, 0))
    blocks = _nbytes((tm, k), BF16) + _nbytes((k, tn), BF16) + _nbytes((tm, tn), BF16) + 3 * _nbytes((tm, LANES), F32)
    return pl.pallas_call(
        _inproj0_kernel,
        grid=(t // tm, n // tn),
        in_specs=[
            pl.BlockSpec((None, tm, k), lambda i, j: (_inproj0_order(j), i, 0)),
            pl.BlockSpec((k, tn), lambda i, j: (0, _inproj0_col_tile(j))),
            tab_spec, tab_spec, tab_spec,
        ],
        out_specs=pl.BlockSpec((tm, tn), lambda i, j: (i, _inproj0_col_tile(j))),
        out_shape=jax.ShapeDtypeStruct((t, n), BF16),
        compiler_params=pltpu.CompilerParams(
            dimension_semantics=("arbitrary", "arbitrary"),
            vmem_limit_bytes=_vmem_limit(blocks, 0, _nbytes((tm, k), BF16) + 3 * _nbytes((tm, LANES), F32)
                                         + 4 * _nbytes((tm, INPROJ0_CHUNK), F32)),
        ),
        name="inproj0",
    )(h_orders, w, *tables)


ATT_TQ = 512


def _attn_kernel(q_ref, kp_ref, kc_ref, kn_ref, vp_ref, vc_ref, vn_ref, o_ref, l_ref, qx, kx, vx, ox, lx, *, tq, ls):
    hb = ATT_HALF
    sub = 2 * hb
    w = qx.shape[2]
    rps = qx.shape[0]
    i = pl.program_id(2)
    bpt, n = q_ref.shape[1], q_ref.shape[3]

    row = lax.broadcasted_iota(jnp.int32, (sub, 2 * sub), 0)
    col = lax.broadcasted_iota(jnp.int32, (sub, 2 * sub), 1)
    band = jnp.abs(col - hb - row) <= hb
    lane = lax.broadcasted_iota(jnp.int32, (sub, LANES), 1)

    for rr in range(rps):
        qx[rr] = q_ref[0, :, rr].reshape(tq, w)
        kx[rr, 0:hb, :] = kp_ref[0, 0, rr]
        kx[rr, hb:hb + tq, :] = kc_ref[0, :, rr].reshape(tq, w)
        kx[rr, hb + tq:, :] = kn_ref[0, 0, rr]
        vx[rr, 0:hb, :] = vp_ref[0, 0, rr]
        vx[rr, hb:hb + tq, :] = vc_ref[0, :, rr].reshape(tq, w)
        vx[rr, hb + tq:, :] = vn_ref[0, 0, rr]

    for rr in range(rps):
        for a in range(tq // sub):
            r0 = a * sub
            kidx = i * tq + r0 + col - hb
            valid = band & (kidx >= 0) & (kidx < ls)
            lse_tile = jnp.zeros((sub, LANES), F32)
            for h in range(ATT_HEADS):
                hs = slice(h * ATT_HEAD_DIM, (h + 1) * ATT_HEAD_DIM)
                qh = qx[rr, r0:r0 + sub, hs]
                kh = kx[rr, r0:r0 + 2 * sub, hs]
                vh = vx[rr, r0:r0 + 2 * sub, hs]
                s = lax.dot_general(qh, kh, (((1,), (1,)), ((), ())), preferred_element_type=F32)
                s = jnp.where(valid, s, NEG_INF)
                m = jnp.max(s, axis=-1, keepdims=True)
                p = jnp.exp2(s - m)
                den = jnp.sum(p, axis=-1, keepdims=True)
                o = jnp.dot(p.astype(BF16), vh, preferred_element_type=F32) * (1.0 / den)
                lse_tile = jnp.where(lane == h, m + jnp.log2(den), lse_tile)
                ox[rr, r0:r0 + sub, hs] = o.astype(ox.dtype)
            lx[rr, r0:r0 + sub, :] = lse_tile
        o_ref[0, :, rr] = ox[rr].reshape(bpt, n, w)
        l_ref[0, :, rr] = lx[rr].reshape(bpt, n, LANES)


def _attention_group(proj, g, bsz, seq):
    _, dil = ATT_GROUPS[g]
    w = ATT_WIDTH
    hb = ATT_HALF
    ls = seq // dil
    if dil == 1:
        n = min(ATT_TQ, 256)
        nb = seq // n
    else:
        nb, n = seq // PERM_BLOCK, PERM_BLOCK // dil
    tq = min(ATT_TQ, ls)
    bpt = tq // n
    rps = min(ATT_TQ // tq, dil)
    assert tq % (2 * hb) == 0 and n % hb == 0 and tq % n == 0 and nb % bpt == 0 and dil % rps == 0
    pv = proj.reshape(bsz, nb, dil, n, ATT_IN_COLS)
    ncb = 3 * g

    grid = (bsz, dil // rps, nb // bpt)
    main = lambda cb: pl.BlockSpec((1, bpt, rps, n, w), lambda b, r, i: (b, i, r, 0, cb))
    prev_h = lambda cb: pl.BlockSpec((1, 1, rps, hb, w),
                                     lambda b, r, i: (b, jnp.maximum(i * bpt - 1, 0), r, n // hb - 1, cb))
    next_h = lambda cb: pl.BlockSpec((1, 1, rps, hb, w),
                                     lambda b, r, i: (b, jnp.minimum((i + 1) * bpt, nb - 1), r, 0, cb))
    out_block = lambda width: pl.BlockSpec((1, bpt, rps, n, width), lambda b, r, i: (b, i, r, 0, 0))

    in_specs = [main(ncb), prev_h(ncb + 1), main(ncb + 1), next_h(ncb + 1),
                prev_h(ncb + 2), main(ncb + 2), next_h(ncb + 2)]
    blocks = rps * (4 * _nbytes((tq, w), BF16) + 4 * _nbytes((hb, w), BF16) + _nbytes((tq, LANES), F32))
    scratch = [pltpu.VMEM((rps, tq, w), BF16), pltpu.VMEM((rps, tq + 2 * hb, w), BF16),
               pltpu.VMEM((rps, tq + 2 * hb, w), BF16), pltpu.VMEM((rps, tq, w), BF16),
               pltpu.VMEM((rps, tq, LANES), F32)]
    scratch_bytes = rps * (4 * _nbytes((tq + 2 * hb, w), BF16) + _nbytes((tq, LANES), F32))
    o, l = pl.pallas_call(
        functools.partial(_attn_kernel, tq=tq, ls=ls),
        grid=grid,
        in_specs=in_specs,
        out_specs=(out_block(w), out_block(LANES)),
        out_shape=(jax.ShapeDtypeStruct((bsz, nb, dil, n, w), BF16),
                   jax.ShapeDtypeStruct((bsz, nb, dil, n, LANES), F32)),
        scratch_shapes=scratch,
        compiler_params=pltpu.CompilerParams(
            dimension_semantics=("arbitrary", "arbitrary", "arbitrary"),
            vmem_limit_bytes=_vmem_limit(blocks, scratch_bytes, 8 * 1024 * 1024),
        ),
        name=f"attn_g{g}",
    )(*([pv] * 7))
    if dil > 1:
        o = o.transpose(0, 1, 3, 2, 4)
        l = l.transpose(0, 1, 3, 2, 4)
    return o.reshape(bsz * seq, w), l.reshape(bsz * seq, LANES)


def _post_norm_residual(out, x_ref, gpost_ref, o_ref, gnext_ref=None, hnext_ref=None):
    ms = jnp.mean(out * out, axis=-1, keepdims=True)
    x1 = x_ref[...] + out * lax.rsqrt(ms + NORM_EPS) * gpost_ref[...]
    o_ref[...] = x1
    if hnext_ref is not None:
        ms1 = jnp.mean(x1 * x1, axis=-1, keepdims=True)
        hnext_ref[...] = (x1 * lax.rsqrt(ms1 + NORM_EPS) * gnext_ref[...]).astype(hnext_ref.dtype)


OUTPROJ_KCHUNK = 512


def _sigmoid(x):
    return 0.5 * jnp.tanh(0.5 * x) + 0.5


def _silu(x):
    xh = 0.5 * x
    return xh * jnp.tanh(xh) + xh


def _outproj0_kernel(o0_ref, o1_ref, o2_ref, l0_ref, l1_ref, l2_ref, z_ref, w_ref, x_ref, gpost_ref, gnext_ref,
                     x1_ref, hnext_ref):
    o_refs = (o0_ref, o1_ref, o2_ref)
    lses = [r[...] for r in (l0_ref, l1_ref, l2_ref)]
    lmax = jnp.maximum(jnp.maximum(lses[0], lses[1]), lses[2])
    es = [jnp.exp2(l - lmax) for l in lses]
    tot = es[0] + es[1] + es[2]
    wts = [e / tot for e in es]
    heads_per_chunk = OUTPROJ_KCHUNK // ATT_HEAD_DIM
    out = None
    for c in range(ATT_HEADS // heads_per_chunk):
        ys = []
        for h in range(c * heads_per_chunk, (c + 1) * heads_per_chunk):
            hs = slice(h * ATT_HEAD_DIM, (h + 1) * ATT_HEAD_DIM)
            o = wts[0][:, h:h + 1] * o_refs[0][:, hs].astype(F32)
            for g in range(1, N_GROUPS):
                o = o + wts[g][:, h:h + 1] * o_refs[g][:, hs].astype(F32)
            ys.append((o * _silu(z_ref[:, hs].astype(F32))).astype(BF16))
        part = jnp.dot(jnp.concatenate(ys, axis=1), w_ref[c * OUTPROJ_KCHUNK:(c + 1) * OUTPROJ_KCHUNK, :],
                       preferred_element_type=F32)
        out = part if out is None else out + part
    _post_norm_residual(out, x_ref, gpost_ref, x1_ref, gnext_ref, hnext_ref)


def _outproj0(outs, lses, proj0, w, x2d, g_post, g_next, tm=256):
    t, d = x2d.shape
    k = w.shape[0]
    row = lambda i: (i, 0)
    vec = pl.BlockSpec((1, d), lambda i: (0, 0))
    zb = 3 * N_GROUPS
    in_specs = ([pl.BlockSpec((tm, k), row)] * N_GROUPS + [pl.BlockSpec((tm, LANES), row)] * N_GROUPS
                + [pl.BlockSpec((tm, k), lambda i: (i, zb)),
                   pl.BlockSpec((k, d), lambda i: (0, 0), pipeline_mode=pl.Buffered(1)),
                   pl.BlockSpec((tm, d), row), vec, vec])
    blocks = ((N_GROUPS + 1) * _nbytes((tm, k), BF16) + N_GROUPS * _nbytes((tm, LANES), F32)
              + 2 * _nbytes((tm, d), F32) + _nbytes((tm, d), BF16))
    return pl.pallas_call(
        _outproj0_kernel,
        grid=(t // tm,),
        in_specs=in_specs,
        out_specs=(pl.BlockSpec((tm, d), row), pl.BlockSpec((tm, d), row)),
        out_shape=(jax.ShapeDtypeStruct((t, d), F32), jax.ShapeDtypeStruct((t, d), BF16)),
        compiler_params=pltpu.CompilerParams(
            dimension_semantics=("arbitrary",),
            vmem_limit_bytes=_vmem_limit(blocks, _nbytes((k, d), BF16), 4 * _nbytes((tm, d), F32)),
        ),
        name="outproj0",
    )(*outs, *lses, proj0, w, x2d, g_post.reshape(1, d).astype(F32), g_next.reshape(1, d).astype(F32))


def _outproj1_kernel(hs_ref, og_ref, z_ref, hgain_ref, w_ref, x_ref, gpost_ref, o_ref):
    assert OUTPROJ_KCHUNK == M_V_DIM
    out = None
    for c in range(hs_ref.shape[1] // OUTPROJ_KCHUNK):
        ks = slice(c * OUTPROJ_KCHUNK, (c + 1) * OUTPROJ_KCHUNK)
        hs = hs_ref[:, ks].astype(F32)
        hn = hs * lax.rsqrt(jnp.mean(hs * hs, axis=-1, keepdims=True) + NORM_EPS) * hgain_ref[:, ks]
        y = (og_ref[:, ks].astype(F32) * hn) * z_ref[:, ks].astype(F32)
        part = jnp.dot(y.astype(BF16), w_ref[ks, :], preferred_element_type=F32)
        out = part if out is None else out + part
    _post_norm_residual(out, x_ref, gpost_ref, o_ref)


def _outproj1(hs, proj1, head_gain, w, x2d, g_post, tm=256):
    t, d = x2d.shape
    k = w.shape[0]
    row = lambda i: (i, 0)
    ob = M_OGATE_COL // k
    blocks = 3 * _nbytes((tm, k), BF16) + _nbytes((1, k), F32) + 2 * _nbytes((tm, d), F32)
    return pl.pallas_call(
        _outproj1_kernel,
        grid=(t // tm,),
        in_specs=[pl.BlockSpec((tm, k), row), pl.BlockSpec((tm, k), lambda i: (i, ob)),
                  pl.BlockSpec((tm, k), lambda i: (i, ob + 1)),
                  pl.BlockSpec((1, k), lambda i: (0, 0)),
                  pl.BlockSpec((k, d), lambda i: (0, 0), pipeline_mode=pl.Buffered(1)),
                  pl.BlockSpec((tm, d), row), pl.BlockSpec((1, d), lambda i: (0, 0))],
        out_specs=pl.BlockSpec((tm, d), row),
        out_shape=jax.ShapeDtypeStruct((t, d), F32),
        compiler_params=pltpu.CompilerParams(
            dimension_semantics=("arbitrary",),
            vmem_limit_bytes=_vmem_limit(blocks, _nbytes((k, d), BF16), 4 * _nbytes((tm, k), F32)),
        ),
        name="outproj1",
    )(hs, proj1, proj1, head_gain.reshape(1, k).astype(F32), w, x2d, g_post.reshape(1, d).astype(F32))


INPROJ1_CHUNK = 512


M_OGATE_COL = 2 * M_QK_WIDTH + M_V_WIDTH
M_ZGATE_COL = M_OGATE_COL + M_V_WIDTH


def _inproj1_kernel(h_ref, w_ref, wg_ref, o_ref, g_ref):
    tn = o_ref.shape[1]
    col0 = pl.program_id(1) * tn

    def tile(act):
        for cc in range(tn // INPROJ1_CHUNK):
            cs = slice(cc * INPROJ1_CHUNK, (cc + 1) * INPROJ1_CHUNK)
            acc = jnp.dot(h_ref[...], w_ref[:, cs], preferred_element_type=F32)
            o_ref[:, cs] = (acc if act is None else act(acc)).astype(o_ref.dtype)

    pl.when(col0 < M_OGATE_COL)(lambda: tile(None))
    pl.when(jnp.logical_and(col0 >= M_OGATE_COL, col0 < M_ZGATE_COL))(lambda: tile(_sigmoid))
    pl.when(col0 >= M_ZGATE_COL)(lambda: tile(_silu))

    @pl.when(pl.program_id(1) == 0)
    def _():
        g_ref[...] = jnp.dot(h_ref[...], wg_ref[...], preferred_element_type=F32)


def _inproj1(h, w, wg, tm=1024, tn=2048):
    t, k = h.shape
    n = M_MAIN_COLS
    tm = min(tm, t)
    blocks = (_nbytes((tm, k), BF16) + _nbytes((k, tn), BF16) + _nbytes((k, GATE_LANES), BF16)
              + _nbytes((tm, tn), BF16) + _nbytes((tm, GATE_LANES), F32))
    return pl.pallas_call(
        _inproj1_kernel,
        grid=(t // tm, n // tn),
        in_specs=[
            pl.BlockSpec((tm, k), lambda i, j: (i, 0)),
            pl.BlockSpec((k, tn), lambda i, j: (0, j)),
            pl.BlockSpec((k, GATE_LANES), lambda i, j: (0, 0)),
        ],
        out_specs=(pl.BlockSpec((tm, tn), lambda i, j: (i, j)), pl.BlockSpec((tm, GATE_LANES), lambda i, j: (i, 0))),
        out_shape=(jax.ShapeDtypeStruct((t, n), BF16), jax.ShapeDtypeStruct((t, GATE_LANES), F32)),
        compiler_params=pltpu.CompilerParams(
            dimension_semantics=("arbitrary", "arbitrary"),
            vmem_limit_bytes=_vmem_limit(blocks, 0, 4 * _nbytes((tm, INPROJ1_CHUNK), F32)),
        ),
        name="inproj1",
    )(h, w, wg)


def _scan_rows(x, rows, op, fill, reverse):
    n = x.shape[0]
    k = 1
    while k < n:
        if reverse:
            shifted = jnp.where(rows < n - k, pltpu.roll(x, n - k, 0), fill)
        else:
            shifted = jnp.where(rows >= k, pltpu.roll(x, k, 0), fill)
        x = op(x, shifted)
        k *= 2
    return x


GATE_PREP_CHUNKS = 4


def _gate_prep_kernel(g_ref, b_ref, o_ref):
    for ci in range(GATE_PREP_CHUNKS):
        _gate_prep_chunk(g_ref[0, ci * M_CHUNK:(ci + 1) * M_CHUNK, :] + b_ref[...], o_ref, ci)


def _gate_prep_chunk(g, o_ref, ci):
    rows = lax.broadcasted_iota(jnp.int32, g.shape, 0)
    kind = lax.broadcasted_iota(jnp.int32, g.shape, 1) % 8
    lf = jnp.minimum(g, 0.0) - jnp.log1p(jnp.exp(-jnp.abs(g)))
    csum = _scan_rows(lf, rows, jnp.add, 0.0, reverse=False)
    rsum = _scan_rows(lf, rows, jnp.add, 0.0, reverse=True)
    b_at_i = pltpu.roll(jnp.where(kind == 1, csum, rsum), LANES - 1, 1)
    grow = g - b_at_i
    cm_f = _scan_rows(grow, rows, jnp.maximum, NEG_INF, reverse=False)
    cm_b = _scan_rows(grow, rows, jnp.maximum, NEG_INF, reverse=True)
    y = jnp.where(kind == 0, b_at_i, 0.0)
    y = jnp.where(kind == 1, pltpu.roll(grow, 1, 1), y)
    y = jnp.where(kind == 2, pltpu.roll(cm_f, 2, 1), y)
    y = jnp.where(kind == 3, rsum, y)
    y = jnp.where(kind == 4, pltpu.roll(grow, 2, 1), y)
    y = jnp.where(kind == 5, pltpu.roll(cm_b, 3, 1), y)
    yt = y.T
    o_ref[0, :, ci] = yt[:M_HEADS * 8].reshape(M_HEADS, 8, g.shape[0])


def _gate_prep(gates, bias, bsz, seq):
    nc = seq // M_CHUNK
    per = GATE_PREP_CHUNKS
    return pl.pallas_call(
        _gate_prep_kernel,
        grid=(bsz, nc // per),
        in_specs=[pl.BlockSpec((1, per * M_CHUNK, GATE_LANES), lambda b, c: (b, c, 0)),
                  pl.BlockSpec((1, GATE_LANES), lambda b, c: (0, 0))],
        out_specs=pl.BlockSpec((1, M_HEADS, per, 8, M_CHUNK), lambda b, c: (b, 0, c, 0, 0)),
        out_shape=jax.ShapeDtypeStruct((bsz, M_HEADS, nc, 8, M_CHUNK), F32),
        compiler_params=pltpu.CompilerParams(dimension_semantics=("arbitrary", "arbitrary")),
        name="gate_prep",
    )(gates.reshape(bsz, seq, GATE_LANES), bias)


def _mlstm_kernel(q_ref, k_ref, v_ref, r_ref, o_ref, hs_scr, cf_scr, cb_scr, *, nc):
    L = M_CHUNK
    rows = lax.broadcasted_iota(jnp.int32, (L, L), 0)
    cols = lax.broadcasted_iota(jnp.int32, (L, L), 1)
    eye = rows == cols
    causal = cols <= rows
    anti = cols >= rows

    def to_col(r):
        return jnp.sum(jnp.where(eye, r, 0.0), axis=1, keepdims=True)

    def chunk(c, c_scr, n, m, fwd):
        st = pl.multiple_of(c * L, L)
        qc = q_ref[0, pl.ds(st, L), :] * (M_QK_DIM ** -0.5)
        kc = k_ref[0, pl.ds(st, L), :]
        vc = v_ref[0, pl.ds(st, L), :]
        r = r_ref[0, 0, pl.ds(c, 1)].reshape(8, L)
        base = 0 if fwd else 3
        grow = r[base + 1:base + 2]
        bcol = to_col(r[base:base + 1])
        gcol = to_col(grow)
        mm = jnp.maximum(to_col(r[base + 2:base + 3]), m)
        e = L - 1 if fwd else 0
        mm_l = mm[e:e + 1]
        s = lax.dot_general(qc, kc, (((1,), (1,)), ((), ())), preferred_element_type=F32)
        a = jnp.exp(jnp.where(causal if fwd else anti, grow - mm, NEG_INF)) * s
        gint = jnp.exp(m - mm)
        cmat = c_scr[...]
        num = (jnp.dot(a.astype(BF16), vc, preferred_element_type=F32)
               + gint * jnp.dot(qc, cmat.astype(BF16), preferred_element_type=F32))
        qn = jnp.sum(qc.astype(F32) * n, axis=1, keepdims=True)
        den = jnp.sum(a, axis=1, keepdims=True) + gint * qn
        hc = num * (1.0 / jnp.maximum(jnp.abs(den), jnp.exp(-(bcol + mm))))
        decay = jnp.exp(m - mm_l)
        wk = jnp.exp(gcol - mm_l) * kc.astype(F32)
        c_scr[...] = decay * cmat + lax.dot_general(wk.astype(BF16), vc, (((0,), (0,)), ((), ())),
                                                    preferred_element_type=F32)
        n_new = decay * n + jnp.sum(wk, axis=0, keepdims=True)
        m_new = bcol[e:e + 1] + mm_l
        return hc, n_new, m_new

    def finish(c, hsum):
        st = pl.multiple_of(c * L, L)
        o_ref[0, pl.ds(st, L), :] = hsum.astype(o_ref.dtype)

    def step(j, carry, second_half):
        nf, mf, nb, mb = carry
        cf, cb = j, nc - 1 - j
        hf, nf, mf = chunk(cf, cf_scr, nf, mf, True)
        hb, nb, mb = chunk(cb, cb_scr, nb, mb, False)
        sf = pl.multiple_of(cf * L, L)
        sb = pl.multiple_of(cb * L, L)
        if second_half:
            finish(cf, hs_scr[pl.ds(sf, L), :] + hf)
            finish(cb, hs_scr[pl.ds(sb, L), :] + hb)
        else:
            hs_scr[pl.ds(sf, L), :] = hf
            hs_scr[pl.ds(sb, L), :] = hb
        return nf, mf, nb, mb

    cf_scr[...] = jnp.zeros_like(cf_scr)
    cb_scr[...] = jnp.zeros_like(cb_scr)
    n0 = jnp.zeros((1, M_QK_DIM), F32)
    m0 = jnp.zeros((1, 1), F32)
    carry = lax.fori_loop(0, nc // 2, functools.partial(step, second_half=False), (n0, m0, n0, m0), unroll=4)
    lax.fori_loop(nc // 2, nc, functools.partial(step, second_half=True), carry, unroll=4)


def _mlstm(proj1, prep, bsz, seq):
    nc = seq // M_CHUNK
    assert nc % 2 == 0
    p3 = proj1.reshape(bsz, seq, M_MAIN_COLS)
    kb = M_QK_WIDTH // M_QK_DIM
    vb = 2 * M_QK_WIDTH // M_V_DIM
    blocks = (2 * _nbytes((seq, M_QK_DIM), BF16) + 2 * _nbytes((seq, M_V_DIM), BF16)
              + _nbytes((nc, 8, M_CHUNK), F32))
    scratch_bytes = _nbytes((seq, M_V_DIM), F32) + 2 * _nbytes((M_QK_DIM, M_V_DIM), F32)
    return pl.pallas_call(
        functools.partial(_mlstm_kernel, nc=nc),
        grid=(bsz, M_HEADS),
        in_specs=[
            pl.BlockSpec((1, seq, M_QK_DIM), lambda b, h: (b, 0, h)),
            pl.BlockSpec((1, seq, M_QK_DIM), lambda b, h: (b, 0, kb + h)),
            pl.BlockSpec((1, seq, M_V_DIM), lambda b, h: (b, 0, vb + h)),
            pl.BlockSpec((1, 1, nc, 8, M_CHUNK), lambda b, h: (b, h, 0, 0, 0)),
        ],
        out_specs=pl.BlockSpec((1, seq, M_V_DIM), lambda b, h: (b, 0, h)),
        out_shape=jax.ShapeDtypeStruct((bsz, seq, M_V_WIDTH), BF16),
        scratch_shapes=[pltpu.VMEM((seq, M_V_DIM), F32), pltpu.VMEM((M_QK_DIM, M_V_DIM), F32),
                        pltpu.VMEM((M_QK_DIM, M_V_DIM), F32)],
        compiler_params=pltpu.CompilerParams(
            dimension_semantics=("arbitrary", "arbitrary"),
            vmem_limit_bytes=_vmem_limit(blocks, scratch_bytes, 8 * 1024 * 1024),
        ),
        name="mlstm",
    )(p3, p3, p3, prep)


def _gate_weights(w_in1, b_gate):
    wg = w_in1[:, M_MAIN_COLS:].reshape(D_MODEL, 4, M_HEADS).transpose(0, 2, 1)
    wg = jnp.pad(wg, ((0, 0), (0, 0), (0, 4))).reshape(D_MODEL, M_HEADS * 8)
    wg = jnp.pad(wg, ((0, 0), (0, GATE_LANES - M_HEADS * 8)))
    bg = jnp.pad(b_gate.astype(F32).reshape(4, M_HEADS).T, ((0, 0), (0, 4))).reshape(1, M_HEADS * 8)
    bg = jnp.pad(bg, ((0, 0), (0, GATE_LANES - M_HEADS * 8)))
    return wg.astype(BF16), bg


def _trunk(x, p):
    bsz, seq, d = x.shape
    t = bsz * seq
    assert seq % (2 * PERM_BLOCK) == 0
    x2d = x.reshape(t, d)
    h0 = _rmsnorm_orders(x2d, p["l0_norm_pre"])
    proj0 = _inproj0(h0, p["l0_w_in"], p["rope"], seq)
    outs, lses = zip(*[_attention_group(proj0, g, bsz, seq) for g in range(N_GROUPS)])
    x1, h1 = _outproj0(outs, lses, proj0, p["l0_w_out"], x2d, p["l0_norm_post"], p["l1_norm_pre"])
    proj1, gates = _inproj1(h1, p["l1_w_main"], p["l1_w_gate"])
    prep = _gate_prep(gates, p["l1_b_gate"], bsz, seq)
    hs = _mlstm(proj1, prep, bsz, seq)
    y = _outproj1(hs.reshape(t, M_V_WIDTH), proj1, p["l1_head_norm"], p["l1_w_out"], x1, p["l1_norm_post"])
    return y.reshape(bsz, seq, d)


def kernel(x_prompt, x_sample, l0_norm_pre, l0_w_in, l0_w_out, l0_norm_post,
           l1_norm_pre, l1_w_in, l1_b_gate, l1_head_norm, l1_w_out, l1_norm_post):
    assert x_prompt.shape[1] == x_sample.shape[1]
    w_gate, b_gate = _gate_weights(l1_w_in, l1_b_gate)
    p = {
        "l0_norm_pre": l0_norm_pre, "l0_w_in": l0_w_in.astype(BF16), "l0_w_out": l0_w_out.astype(BF16),
        "l0_norm_post": l0_norm_post, "l1_norm_pre": l1_norm_pre,
        "l1_w_main": l1_w_in.astype(BF16), "l1_w_gate": w_gate, "l1_b_gate": b_gate,
        "l1_head_norm": l1_head_norm, "l1_w_out": l1_w_out.astype(BF16), "l1_norm_post": l1_norm_post,
        "rope": _rope_tables(x_prompt.shape[1]),
    }
    return (_trunk(x_prompt, p), _trunk(x_sample, p))
```

```python
import functools

import jax
import jax.numpy as jnp
from jax import lax
from jax.experimental import pallas as pl
from jax.experimental.pallas import tpu as pltpu

F32 = jnp.float32
BF16 = jnp.bfloat16

D_MODEL = 2048
ATT_GROUPS = ((128, 1), (512, 4), (2048, 16))
N_GROUPS = len(ATT_GROUPS)
ATT_HEADS = 16
ATT_HEAD_DIM = 128
ATT_WIDTH = ATT_HEADS * ATT_HEAD_DIM
ATT_IN_COLS = 3 * N_GROUPS * ATT_WIDTH + ATT_WIDTH
ATT_HALF = 64
ROPE_THETA = 500000.0
ROPE_DIMS = ATT_HEAD_DIM // 4
ROPE_HALF = ROPE_DIMS // 2
LOG2E = 1.4426950408889634
ATT_Q_SCALE = ATT_HEAD_DIM ** -0.5 * LOG2E

M_HEADS = 8
M_QK_DIM = 256
M_V_DIM = 512
M_QK_WIDTH = M_HEADS * M_QK_DIM
M_V_WIDTH = M_HEADS * M_V_DIM
M_MAIN_COLS = 2 * M_QK_WIDTH + 3 * M_V_WIDTH
M_CHUNK = 256
GATE_LANES = 128

NORM_EPS = 1e-6
NEG_INF = -1e30

LANES = 128
VMEM_CAP_BYTES = 56 * 1024 * 1024


def _vmem_limit(block_bytes, scratch_bytes=0, temp_bytes=0):
    need = 2 * block_bytes + scratch_bytes + temp_bytes
    return int(min(max(need, 16 * 1024 * 1024), VMEM_CAP_BYTES))


def _nbytes(shape, dtype):
    n = 1
    for s in shape:
        n *= s
    return n * jnp.dtype(dtype).itemsize


PERM_BLOCK = 1024


RMS_PIECE = 256


def _rmsnorm_orders_kernel(x_ref, g_ref, o_ref, slab_scr):
    nslab = x_ref.shape[1] // LANES
    pr = RMS_PIECE

    def piece(pi, _):
        r0 = pl.multiple_of(pi * pr, pr)
        x = x_ref[pl.ds(r0, pr), :]
        inv = lax.rsqrt(jnp.mean(x * x, axis=-1, keepdims=True) + NORM_EPS)
        for c in range(nslab):
            ls = slice(c * LANES, (c + 1) * LANES)
            y = x[:, ls] * inv * g_ref[:, ls]
            slab_scr[c] = y
            for g, (_, dil) in enumerate(ATT_GROUPS):
                if dil == 1:
                    o_ref[g, pl.ds(r0, pr), ls] = y.astype(o_ref.dtype)
        for g, (_, dil) in enumerate(ATT_GROUPS):
            if dil == 1:
                continue
            n = PERM_BLOCK // dil
            npc = pr // dil
            for r in range(dil):
                dst = pl.multiple_of(r * n + pi * npc, npc)
                for c in range(nslab):
                    o_ref[g, pl.ds(dst, npc), c * LANES:(c + 1) * LANES] = (
                        slab_scr[c, pl.ds(r, npc, stride=dil), :].astype(o_ref.dtype))
        return 0

    lax.fori_loop(0, x_ref.shape[0] // pr, piece, 0)


def _rmsnorm_orders(x2d, gain):
    t, d = x2d.shape
    tm = PERM_BLOCK
    return pl.pallas_call(
        _rmsnorm_orders_kernel,
        grid=(t // tm,),
        in_specs=[pl.BlockSpec((tm, d), lambda i: (i, 0)), pl.BlockSpec((1, d), lambda i: (0, 0))],
        out_specs=pl.BlockSpec((N_GROUPS, tm, d), lambda i: (0, i, 0)),
        out_shape=jax.ShapeDtypeStruct((N_GROUPS, t, d), BF16),
        scratch_shapes=[pltpu.VMEM((d // LANES, RMS_PIECE, LANES), F32)],
        compiler_params=pltpu.CompilerParams(
            dimension_semantics=("arbitrary",),
            vmem_limit_bytes=_vmem_limit(_nbytes((tm, d), F32) + _nbytes((N_GROUPS, tm, d), BF16),
                                         _nbytes((RMS_PIECE, d), F32), 2 * _nbytes((RMS_PIECE, d), F32)),
        ),
        name="rmsnorm_orders",
    )(x2d, gain.reshape(1, d).astype(F32))


def _order_index(seq, dil):
    n = PERM_BLOCK // dil
    return jnp.arange(seq, dtype=jnp.int32).reshape(seq // PERM_BLOCK, n, dil).transpose(0, 2, 1).reshape(seq)


INPROJ0_TN = 2048
INPROJ0_CHUNK = 256
INPROJ0_TAIL_ROWS = ((0, 512), (512, 768), (768, 1024))
_GROUP_TILES = 3 * ATT_WIDTH // INPROJ0_TN
_Z_TILES = ATT_WIDTH // INPROJ0_TN


def _inproj0_col_tile(jj):
    return jnp.where(jj < _GROUP_TILES, jj,
                     jnp.where(jj < _GROUP_TILES + _Z_TILES, jj + (N_GROUPS - 1) * _GROUP_TILES, jj - _Z_TILES))


def _inproj0_order(jj):
    return jnp.where(jj < _GROUP_TILES + _Z_TILES, 0, (jj - _Z_TILES) // _GROUP_TILES)


def _inproj0_kernel(h_ref, w_ref, c_ref, s1_ref, s2_ref, o_ref):
    col = _inproj0_col_tile(pl.program_id(1))
    seg = (col * INPROJ0_TN) // ATT_WIDTH
    kind = seg % 3
    is_rope = jnp.logical_and(seg < 3 * N_GROUPS, kind < 2)
    nchunk = INPROJ0_TN // INPROJ0_CHUNK
    tm = o_ref.shape[0]

    def pieces():
        for cc in range(nchunk):
            cols = slice(cc * INPROJ0_CHUNK, (cc + 1) * INPROJ0_CHUNK)
            if cc < nchunk - 1:
                yield slice(0, tm), cols
            else:
                for lo, hi in INPROJ0_TAIL_ROWS:
                    yield slice(lo, hi), cols

    @pl.when(is_rope)
    def _():
        scale = jnp.where(kind == 0, ATT_Q_SCALE, 1.0).astype(F32)
        c_all = c_ref[...] * scale
        s1_all = s1_ref[...] * scale
        s2_all = s2_ref[...] * scale
        for rows, cols in pieces():
            c, s1, s2 = c_all[rows], s1_all[rows], s2_all[rows]
            acc = jnp.dot(h_ref[rows, :], w_ref[:, cols], preferred_element_type=F32)
            for t in range(INPROJ0_CHUNK // LANES):
                a = acc[:, t * LANES:(t + 1) * LANES]
                r = a * c + pltpu.roll(a, LANES - ROPE_HALF, 1) * s1 + pltpu.roll(a, ROPE_HALF, 1) * s2
                lo = cols.start + t * LANES
                o_ref[rows, lo:lo + LANES] = r.astype(o_ref.dtype)

    @pl.when(jnp.logical_not(is_rope))
    def _():
        for rows, cols in pieces():
            o_ref[rows, cols] = jnp.dot(h_ref[rows, :], w_ref[:, cols], preferred_element_type=F32).astype(o_ref.dtype)


def _rope_tables(seq):
    inv = jnp.power(ROPE_THETA, -jnp.arange(ROPE_HALF, dtype=F32) / ROPE_HALF)
    ang = jnp.arange(seq, dtype=F32)[:, None] * inv[None, :]
    cos, sin = jnp.cos(ang), jnp.sin(ang)
    zeros = jnp.zeros((seq, LANES - ROPE_DIMS), F32)
    zh = jnp.zeros((seq, ROPE_HALF), F32)
    c = jnp.concatenate([cos, cos, jnp.ones((seq, LANES - ROPE_DIMS), F32)], axis=1)
    s1 = jnp.concatenate([-sin, zh, zeros], axis=1)
    s2 = jnp.concatenate([zh, sin, zeros], axis=1)
    orders = [_order_index(seq, dil) for _, dil in ATT_GROUPS]
    return tuple(jnp.stack([tab[idx] for idx in orders]) for tab in (c, s1, s2))


def _inproj0(h_orders, w, tables, seq):
    _, t, k = h_orders.shape
    n = w.shape[1]
    tm, tn = PERM_BLOCK, INPROJ0_TN
    pos_blocks = seq // tm
    tab_spec = pl.BlockSpec((None, tm, LANES), lambda i, j: (_inproj0_order(j), i % pos_blocks, 0))
    blocks = _nbytes((tm, k), BF16) + _nbytes((k, tn), BF16) + _nbytes((tm, tn), BF16) + 3 * _nbytes((tm, LANES), F32)
    return pl.pallas_call(
        _inproj0_kernel,
        grid=(t // tm, n // tn),
        in_specs=[
            pl.BlockSpec((None, tm, k), lambda i, j: (_inproj0_order(j), i, 0)),
            pl.BlockSpec((k, tn), lambda i, j: (0, _inproj0_col_tile(j))),
            tab_spec, tab_spec, tab_spec,
        ],
        out_specs=pl.BlockSpec((tm, tn), lambda i, j: (i, _inproj0_col_tile(j))),
        out_shape=jax.ShapeDtypeStruct((t, n), BF16),
        compiler_params=pltpu.CompilerParams(
            dimension_semantics=("arbitrary", "arbitrary"),
            vmem_limit_bytes=_vmem_limit(blocks, 0, _nbytes((tm, k), BF16) + 3 * _nbytes((tm, LANES), F32)
                                         + 4 * _nbytes((tm, INPROJ0_CHUNK), F32)),
        ),
        name="inproj0",
    )(h_orders, w, *tables)


ATT_TQ = 512


def _attn_kernel(q_ref, kp_ref, kc_ref, kn_ref, vp_ref, vc_ref, vn_ref, o_ref, l_ref, qx, kx, vx, ox, lx, *, tq, ls):
    hb = ATT_HALF
    sub = 2 * hb
    w = qx.shape[2]
    rps = qx.shape[0]
    i = pl.program_id(2)
    bpt, n = q_ref.shape[1], q_ref.shape[3]

    row = lax.broadcasted_iota(jnp.int32, (sub, 2 * sub), 0)
    col = lax.broadcasted_iota(jnp.int32, (sub, 2 * sub), 1)
    band = jnp.abs(col - hb - row) <= hb
    lane = lax.broadcasted_iota(jnp.int32, (sub, LANES), 1)

    for rr in range(rps):
        qx[rr] = q_ref[0, :, rr].reshape(tq, w)
        kx[rr, 0:hb, :] = kp_ref[0, 0, rr]
        kx[rr, hb:hb + tq, :] = kc_ref[0, :, rr].reshape(tq, w)
        kx[rr, hb + tq:, :] = kn_ref[0, 0, rr]
        vx[rr, 0:hb, :] = vp_ref[0, 0, rr]
        vx[rr, hb:hb + tq, :] = vc_ref[0, :, rr].reshape(tq, w)
        vx[rr, hb + tq:, :] = vn_ref[0, 0, rr]

    for rr in range(rps):
        for a in range(tq // sub):
            r0 = a * sub
            kidx = i * tq + r0 + col - hb
            valid = band & (kidx >= 0) & (kidx < ls)
            lse_tile = jnp.zeros((sub, LANES), F32)
            for h in range(ATT_HEADS):
                hs = slice(h * ATT_HEAD_DIM, (h + 1) * ATT_HEAD_DIM)
                qh = qx[rr, r0:r0 + sub, hs]
                kh = kx[rr, r0:r0 + 2 * sub, hs]
                vh = vx[rr, r0:r0 + 2 * sub, hs]
                s = lax.dot_general(qh, kh, (((1,), (1,)), ((), ())), preferred_element_type=F32)
                s = jnp.where(valid, s, NEG_INF)
                m = jnp.max(s, axis=-1, keepdims=True)
                p = jnp.exp2(s - m)
                den = jnp.sum(p, axis=-1, keepdims=True)
                o = jnp.dot(p.astype(BF16), vh, preferred_element_type=F32) * (1.0 / den)
                lse_tile = jnp.where(lane == h, m + jnp.log2(den), lse_tile)
                ox[rr, r0:r0 + sub, hs] = o.astype(ox.dtype)
            lx[rr, r0:r0 + sub, :] = lse_tile
        o_ref[0, :, rr] = ox[rr].reshape(bpt, n, w)
        l_ref[0, :, rr] = lx[rr].reshape(bpt, n, LANES)


def _attention_group(proj, g, bsz, seq):
    _, dil = ATT_GROUPS[g]
    w = ATT_WIDTH
    hb = ATT_HALF
    ls = seq // dil
    if dil == 1:
        n = min(ATT_TQ, 256)
        nb = seq // n
    else:
        nb, n = seq // PERM_BLOCK, PERM_BLOCK // dil
    tq = min(ATT_TQ, ls)
    bpt = tq // n
    rps = min(ATT_TQ // tq, dil)
    assert tq % (2 * hb) == 0 and n % hb == 0 and tq % n == 0 and nb % bpt == 0 and dil % rps == 0
    pv = proj.reshape(bsz, nb, dil, n, ATT_IN_COLS)
    ncb = 3 * g

    grid = (bsz, dil // rps, nb // bpt)
    main = lambda cb: pl.BlockSpec((1, bpt, rps, n, w), lambda b, r, i: (b, i, r, 0, cb))
    prev_h = lambda cb: pl.BlockSpec((1, 1, rps, hb, w),
                                     lambda b, r, i: (b, jnp.maximum(i * bpt - 1, 0), r, n // hb - 1, cb))
    next_h = lambda cb: pl.BlockSpec((1, 1, rps, hb, w),
                                     lambda b, r, i: (b, jnp.minimum((i + 1) * bpt, nb - 1), r, 0, cb))
    out_block = lambda width: pl.BlockSpec((1, bpt, rps, n, width), lambda b, r, i: (b, i, r, 0, 0))

    in_specs = [main(ncb), prev_h(ncb + 1), main(ncb + 1), next_h(ncb + 1),
                prev_h(ncb + 2), main(ncb + 2), next_h(ncb + 2)]
    blocks = rps * (4 * _nbytes((tq, w), BF16) + 4 * _nbytes((hb, w), BF16) + _nbytes((tq, LANES), F32))
    scratch = [pltpu.VMEM((rps, tq, w), BF16), pltpu.VMEM((rps, tq + 2 * hb, w), BF16),
               pltpu.VMEM((rps, tq + 2 * hb, w), BF16), pltpu.VMEM((rps, tq, w), BF16),
               pltpu.VMEM((rps, tq, LANES), F32)]
    scratch_bytes = rps * (4 * _nbytes((tq + 2 * hb, w), BF16) + _nbytes((tq, LANES), F32))
    o, l = pl.pallas_call(
        functools.partial(_attn_kernel, tq=tq, ls=ls),
        grid=grid,
        in_specs=in_specs,
        out_specs=(out_block(w), out_block(LANES)),
        out_shape=(jax.ShapeDtypeStruct((bsz, nb, dil, n, w), BF16),
                   jax.ShapeDtypeStruct((bsz, nb, dil, n, LANES), F32)),
        scratch_shapes=scratch,
        compiler_params=pltpu.CompilerParams(
            dimension_semantics=("arbitrary", "arbitrary", "arbitrary"),
            vmem_limit_bytes=_vmem_limit(blocks, scratch_bytes, 8 * 1024 * 1024),
        ),
        name=f"attn_g{g}",
    )(*([pv] * 7))
    if dil > 1:
        o = o.transpose(0, 1, 3, 2, 4)
        l = l.transpose(0, 1, 3, 2, 4)
    return o.reshape(bsz * seq, w), l.reshape(bsz * seq, LANES)


def _post_norm_residual(out, x_ref, gpost_ref, o_ref, gnext_ref=None, hnext_ref=None):
    ms = jnp.mean(out * out, axis=-1, keepdims=True)
    x1 = x_ref[...] + out * lax.rsqrt(ms + NORM_EPS) * gpost_ref[...]
    o_ref[...] = x1
    if hnext_ref is not None:
        ms1 = jnp.mean(x1 * x1, axis=-1, keepdims=True)
        hnext_ref[...] = (x1 * lax.rsqrt(ms1 + NORM_EPS) * gnext_ref[...]).astype(hnext_ref.dtype)


OUTPROJ_KCHUNK = 512


def _sigmoid(x):
    return 0.5 * jnp.tanh(0.5 * x) + 0.5


def _silu(x):
    xh = 0.5 * x
    return xh * jnp.tanh(xh) + xh


def _outproj0_kernel(o0_ref, o1_ref, o2_ref, l0_ref, l1_ref, l2_ref, z_ref, w_ref, x_ref, gpost_ref, gnext_ref,
                     x1_ref, hnext_ref):
    o_refs = (o0_ref, o1_ref, o2_ref)
    lses = [r[...] for r in (l0_ref, l1_ref, l2_ref)]
    lmax = jnp.maximum(jnp.maximum(lses[0], lses[1]), lses[2])
    es = [jnp.exp2(l - lmax) for l in lses]
    tot = es[0] + es[1] + es[2]
    wts = [e / tot for e in es]
    heads_per_chunk = OUTPROJ_KCHUNK // ATT_HEAD_DIM
    out = None
    for c in range(ATT_HEADS // heads_per_chunk):
        ys = []
        for h in range(c * heads_per_chunk, (c + 1) * heads_per_chunk):
            hs = slice(h * ATT_HEAD_DIM, (h + 1) * ATT_HEAD_DIM)
            o = wts[0][:, h:h + 1] * o_refs[0][:, hs].astype(F32)
            for g in range(1, N_GROUPS):
                o = o + wts[g][:, h:h + 1] * o_refs[g][:, hs].astype(F32)
            ys.append((o * _silu(z_ref[:, hs].astype(F32))).astype(BF16))
        part = jnp.dot(jnp.concatenate(ys, axis=1), w_ref[c * OUTPROJ_KCHUNK:(c + 1) * OUTPROJ_KCHUNK, :],
                       preferred_element_type=F32)
        out = part if out is None else out + part
    _post_norm_residual(out, x_ref, gpost_ref, x1_ref, gnext_ref, hnext_ref)


def _outproj0(outs, lses, proj0, w, x2d, g_post, g_next, tm=256):
    t, d = x2d.shape
    k = w.shape[0]
    row = lambda i: (i, 0)
    vec = pl.BlockSpec((1, d), lambda i: (0, 0))
    zb = 3 * N_GROUPS
    in_specs = ([pl.BlockSpec((tm, k), row)] * N_GROUPS + [pl.BlockSpec((tm, LANES), row)] * N_GROUPS
                + [pl.BlockSpec((tm, k), lambda i: (i, zb)),
                   pl.BlockSpec((k, d), lambda i: (0, 0), pipeline_mode=pl.Buffered(1)),
                   pl.BlockSpec((tm, d), row), vec, vec])
    blocks = ((N_GROUPS + 1) * _nbytes((tm, k), BF16) + N_GROUPS * _nbytes((tm, LANES), F32)
              + 2 * _nbytes((tm, d), F32) + _nbytes((tm, d), BF16))
    return pl.pallas_call(
        _outproj0_kernel,
        grid=(t // tm,),
        in_specs=in_specs,
        out_specs=(pl.BlockSpec((tm, d), row), pl.BlockSpec((tm, d), row)),
        out_shape=(jax.ShapeDtypeStruct((t, d), F32), jax.ShapeDtypeStruct((t, d), BF16)),
        compiler_params=pltpu.CompilerParams(
            dimension_semantics=("arbitrary",),
            vmem_limit_bytes=_vmem_limit(blocks, _nbytes((k, d), BF16), 4 * _nbytes((tm, d), F32)),
        ),
        name="outproj0",
    )(*outs, *lses, proj0, w, x2d, g_post.reshape(1, d).astype(F32), g_next.reshape(1, d).astype(F32))


def _outproj1_kernel(hs_ref, og_ref, z_ref, hgain_ref, w_ref, x_ref, gpost_ref, o_ref):
    assert OUTPROJ_KCHUNK == M_V_DIM
    out = None
    for c in range(hs_ref.shape[1] // OUTPROJ_KCHUNK):
        ks = slice(c * OUTPROJ_KCHUNK, (c + 1) * OUTPROJ_KCHUNK)
        hs = hs_ref[:, ks].astype(F32)
        hn = hs * lax.rsqrt(jnp.mean(hs * hs, axis=-1, keepdims=True) + NORM_EPS) * hgain_ref[:, ks]
        y = (og_ref[:, ks].astype(F32) * hn) * z_ref[:, ks].astype(F32)
        part = jnp.dot(y.astype(BF16), w_ref[ks, :], preferred_element_type=F32)
        out = part if out is None else out + part
    _post_norm_residual(out, x_ref, gpost_ref, o_ref)


def _outproj1(hs, proj1, head_gain, w, x2d, g_post, tm=256):
    t, d = x2d.shape
    k = w.shape[0]
    row = lambda i: (i, 0)
    ob = M_OGATE_COL // k
    blocks = 3 * _nbytes((tm, k), BF16) + _nbytes((1, k), F32) + 2 * _nbytes((tm, d), F32)
    return pl.pallas_call(
        _outproj1_kernel,
        grid=(t // tm,),
        in_specs=[pl.BlockSpec((tm, k), row), pl.BlockSpec((tm, k), lambda i: (i, ob)),
                  pl.BlockSpec((tm, k), lambda i: (i, ob + 1)),
                  pl.BlockSpec((1, k), lambda i: (0, 0)),
                  pl.BlockSpec((k, d), lambda i: (0, 0), pipeline_mode=pl.Buffered(1)),
                  pl.BlockSpec((tm, d), row), pl.BlockSpec((1, d), lambda i: (0, 0))],
        out_specs=pl.BlockSpec((tm, d), row),
        out_shape=jax.ShapeDtypeStruct((t, d), F32),
        compiler_params=pltpu.CompilerParams(
            dimension_semantics=("arbitrary",),
            vmem_limit_bytes=_vmem_limit(blocks, _nbytes((k, d), BF16), 4 * _nbytes((tm, k), F32)),
        ),
        name="outproj1",
    )(hs, proj1, proj1, head_gain.reshape(1, k).astype(F32), w, x2d, g_post.reshape(1, d).astype(F32))


INPROJ1_CHUNK = 512


M_OGATE_COL = 2 * M_QK_WIDTH + M_V_WIDTH
M_ZGATE_COL = M_OGATE_COL + M_V_WIDTH


def _inproj1_kernel(h_ref, w_ref, wg_ref, o_ref, g_ref):
    tn = o_ref.shape[1]
    col0 = pl.program_id(1) * tn

    def tile(act):
        for cc in range(tn // INPROJ1_CHUNK):
            cs = slice(cc * INPROJ1_CHUNK, (cc + 1) * INPROJ1_CHUNK)
            acc = jnp.dot(h_ref[...], w_ref[:, cs], preferred_element_type=F32)
            o_ref[:, cs] = (acc if act is None else act(acc)).astype(o_ref.dtype)

    pl.when(col0 < M_OGATE_COL)(lambda: tile(None))
    pl.when(jnp.logical_and(col0 >= M_OGATE_COL, col0 < M_ZGATE_COL))(lambda: tile(_sigmoid))
    pl.when(col0 >= M_ZGATE_COL)(lambda: tile(_silu))

    @pl.when(pl.program_id(1) == 0)
    def _():
        g_ref[...] = jnp.dot(h_ref[...], wg_ref[...], preferred_element_type=F32)


def _inproj1(h, w, wg, tm=1024, tn=2048):
    t, k = h.shape
    n = M_MAIN_COLS
    tm = min(tm, t)
    blocks = (_nbytes((tm, k), BF16) + _nbytes((k, tn), BF16) + _nbytes((k, GATE_LANES), BF16)
              + _nbytes((tm, tn), BF16) + _nbytes((tm, GATE_LANES), F32))
    return pl.pallas_call(
        _inproj1_kernel,
        grid=(t // tm, n // tn),
        in_specs=[
            pl.BlockSpec((tm, k), lambda i, j: (i, 0)),
            pl.BlockSpec((k, tn), lambda i, j: (0, j)),
            pl.BlockSpec((k, GATE_LANES), lambda i, j: (0, 0)),
        ],
        out_specs=(pl.BlockSpec((tm, tn), lambda i, j: (i, j)), pl.BlockSpec((tm, GATE_LANES), lambda i, j: (i, 0))),
        out_shape=(jax.ShapeDtypeStruct((t, n), BF16), jax.ShapeDtypeStruct((t, GATE_LANES), F32)),
        compiler_params=pltpu.CompilerParams(
            dimension_semantics=("arbitrary", "arbitrary"),
            vmem_limit_bytes=_vmem_limit(blocks, 0, 4 * _nbytes((tm, INPROJ1_CHUNK), F32)),
        ),
        name="inproj1",
    )(h, w, wg)


def _scan_rows(x, rows, op, fill, reverse):
    n = x.shape[0]
    k = 1
    while k < n:
        if reverse:
            shifted = jnp.where(rows < n - k, pltpu.roll(x, n - k, 0), fill)
        else:
            shifted = jnp.where(rows >= k, pltpu.roll(x, k, 0), fill)
        x = op(x, shifted)
        k *= 2
    return x


GATE_PREP_CHUNKS = 4


def _gate_prep_kernel(g_ref, b_ref, o_ref):
    for ci in range(GATE_PREP_CHUNKS):
        _gate_prep_chunk(g_ref[0, ci * M_CHUNK:(ci + 1) * M_CHUNK, :] + b_ref[...], o_ref, ci)


def _gate_prep_chunk(g, o_ref, ci):
    rows = lax.broadcasted_iota(jnp.int32, g.shape, 0)
    kind = lax.broadcasted_iota(jnp.int32, g.shape, 1) % 8
    lf = jnp.minimum(g, 0.0) - jnp.log1p(jnp.exp(-jnp.abs(g)))
    csum = _scan_rows(lf, rows, jnp.add, 0.0, reverse=False)
    rsum = _scan_rows(lf, rows, jnp.add, 0.0, reverse=True)
    b_at_i = pltpu.roll(jnp.where(kind == 1, csum, rsum), LANES - 1, 1)
    grow = g - b_at_i
    cm_f = _scan_rows(grow, rows, jnp.maximum, NEG_INF, reverse=False)
    cm_b = _scan_rows(grow, rows, jnp.maximum, NEG_INF, reverse=True)
    y = jnp.where(kind == 0, b_at_i, 0.0)
    y = jnp.where(kind == 1, pltpu.roll(grow, 1, 1), y)
    y = jnp.where(kind == 2, pltpu.roll(cm_f, 2, 1), y)
    y = jnp.where(kind == 3, rsum, y)
    y = jnp.where(kind == 4, pltpu.roll(grow, 2, 1), y)
    y = jnp.where(kind == 5, pltpu.roll(cm_b, 3, 1), y)
    yt = y.T
    o_ref[0, :, ci] = yt[:M_HEADS * 8].reshape(M_HEADS, 8, g.shape[0])


def _gate_prep(gates, bias, bsz, seq):
    nc = seq // M_CHUNK
    per = GATE_PREP_CHUNKS
    return pl.pallas_call(
        _gate_prep_kernel,
        grid=(bsz, nc // per),
        in_specs=[pl.BlockSpec((1, per * M_CHUNK, GATE_LANES), lambda b, c: (b, c, 0)),
                  pl.BlockSpec((1, GATE_LANES), lambda b, c: (0, 0))],
        out_specs=pl.BlockSpec((1, M_HEADS, per, 8, M_CHUNK), lambda b, c: (b, 0, c, 0, 0)),
        out_shape=jax.ShapeDtypeStruct((bsz, M_HEADS, nc, 8, M_CHUNK), F32),
        compiler_params=pltpu.CompilerParams(dimension_semantics=("arbitrary", "arbitrary")),
        name="gate_prep",
    )(gates.reshape(bsz, seq, GATE_LANES), bias)


def _mlstm_kernel(q_ref, k_ref, v_ref, r_ref, o_ref, hs_scr, cf_scr, cb_scr, *, nc):
    L = M_CHUNK
    rows = lax.broadcasted_iota(jnp.int32, (L, L), 0)
    cols = lax.broadcasted_iota(jnp.int32, (L, L), 1)
    eye = rows == cols
    causal = cols <= rows
    anti = cols >= rows

    def to_col(r):
        return jnp.sum(jnp.where(eye, r, 0.0), axis=1, keepdims=True)

    def chunk(c, c_scr, n, m, fwd):
        st = pl.multiple_of(c * L, L)
        qc = q_ref[0, pl.ds(st, L), :] * (M_QK_DIM ** -0.5)
        kc = k_ref[0, pl.ds(st, L), :]
        vc = v_ref[0, pl.ds(st, L), :]
        r = r_ref[0, 0, pl.ds(c, 1)].reshape(8, L)
        base = 0 if fwd else 3
        grow = r[base + 1:base + 2]
        bcol = to_col(r[base:base + 1])
        gcol = to_col(grow)
        mm = jnp.maximum(to_col(r[base + 2:base + 3]), m)
        e = L - 1 if fwd else 0
        mm_l = mm[e:e + 1]
        s = lax.dot_general(qc, kc, (((1,), (1,)), ((), ())), preferred_element_type=F32)
        a = jnp.exp(jnp.where(causal if fwd else anti, grow - mm, NEG_INF)) * s
        gint = jnp.exp(m - mm)
        cmat = c_scr[...]
        num = (jnp.dot(a.astype(BF16), vc, preferred_element_type=F32)
               + gint * jnp.dot(qc, cmat.astype(BF16), preferred_element_type=F32))
        qn = jnp.sum(qc.astype(F32) * n, axis=1, keepdims=True)
        den = jnp.sum(a, axis=1, keepdims=True) + gint * qn
        hc = num * (1.0 / jnp.maximum(jnp.abs(den), jnp.exp(-(bcol + mm))))
        decay = jnp.exp(m - mm_l)
        wk = jnp.exp(gcol - mm_l) * kc.astype(F32)
        c_scr[...] = decay * cmat + lax.dot_general(wk.astype(BF16), vc, (((0,), (0,)), ((), ())),
                                                    preferred_element_type=F32)
        n_new = decay * n + jnp.sum(wk, axis=0, keepdims=True)
        m_new = bcol[e:e + 1] + mm_l
        return hc, n_new, m_new

    def finish(c, hsum):
        st = pl.multiple_of(c * L, L)
        o_ref[0, pl.ds(st, L), :] = hsum.astype(o_ref.dtype)

    def step(j, carry, second_half):
        nf, mf, nb, mb = carry
        cf, cb = j, nc - 1 - j
        hf, nf, mf = chunk(cf, cf_scr, nf, mf, True)
        hb, nb, mb = chunk(cb, cb_scr, nb, mb, False)
        sf = pl.multiple_of(cf * L, L)
        sb = pl.multiple_of(cb * L, L)
        if second_half:
            finish(cf, hs_scr[pl.ds(sf, L), :] + hf)
            finish(cb, hs_scr[pl.ds(sb, L), :] + hb)
        else:
            hs_scr[pl.ds(sf, L), :] = hf
            hs_scr[pl.ds(sb, L), :] = hb
        return nf, mf, nb, mb

    cf_scr[...] = jnp.zeros_like(cf_scr)
    cb_scr[...] = jnp.zeros_like(cb_scr)
    n0 = jnp.zeros((1, M_QK_DIM), F32)
    m0 = jnp.zeros((1, 1), F32)
    carry = lax.fori_loop(0, nc // 2, functools.partial(step, second_half=False), (n0, m0, n0, m0), unroll=4)
    lax.fori_loop(nc // 2, nc, functools.partial(step, second_half=True), carry, unroll=4)


def _mlstm(proj1, prep, bsz, seq):
    nc = seq // M_CHUNK
    assert nc % 2 == 0
    p3 = proj1.reshape(bsz, seq, M_MAIN_COLS)
    kb = M_QK_WIDTH // M_QK_DIM
    vb = 2 * M_QK_WIDTH // M_V_DIM
    blocks = (2 * _nbytes((seq, M_QK_DIM), BF16) + 2 * _nbytes((seq, M_V_DIM), BF16)
              + _nbytes((nc, 8, M_CHUNK), F32))
    scratch_bytes = _nbytes((seq, M_V_DIM), F32) + 2 * _nbytes((M_QK_DIM, M_V_DIM), F32)
    return pl.pallas_call(
        functools.partial(_mlstm_kernel, nc=nc),
        grid=(bsz, M_HEADS),
        in_specs=[
            pl.BlockSpec((1, seq, M_QK_DIM), lambda b, h: (b, 0, h)),
            pl.BlockSpec((1, seq, M_QK_DIM), lambda b, h: (b, 0, kb + h)),
            pl.BlockSpec((1, seq, M_V_DIM), lambda b, h: (b, 0, vb + h)),
            pl.BlockSpec((1, 1, nc, 8, M_CHUNK), lambda b, h: (b, h, 0, 0, 0)),
        ],
        out_specs=pl.BlockSpec((1, seq, M_V_DIM), lambda b, h: (b, 0, h)),
        out_shape=jax.ShapeDtypeStruct((bsz, seq, M_V_WIDTH), BF16),
        scratch_shapes=[pltpu.VMEM((seq, M_V_DIM), F32), pltpu.VMEM((M_QK_DIM, M_V_DIM), F32),
                        pltpu.VMEM((M_QK_DIM, M_V_DIM), F32)],
        compiler_params=pltpu.CompilerParams(
            dimension_semantics=("arbitrary", "arbitrary"),
            vmem_limit_bytes=_vmem_limit(blocks, scratch_bytes, 8 * 1024 * 1024),
        ),
        name="mlstm",
    )(p3, p3, p3, prep)


def _gate_weights(w_in1, b_gate):
    wg = w_in1[:, M_MAIN_COLS:].reshape(D_MODEL, 4, M_HEADS).transpose(0, 2, 1)
    wg = jnp.pad(wg, ((0, 0), (0, 0), (0, 4))).reshape(D_MODEL, M_HEADS * 8)
    wg = jnp.pad(wg, ((0, 0), (0, GATE_LANES - M_HEADS * 8)))
    bg = jnp.pad(b_gate.astype(F32).reshape(4, M_HEADS).T, ((0, 0), (0, 4))).reshape(1, M_HEADS * 8)
    bg = jnp.pad(bg, ((0, 0), (0, GATE_LANES - M_HEADS * 8)))
    return wg.astype(BF16), bg


def _trunk(x, p):
    bsz, seq, d = x.shape
    t = bsz * seq
    assert seq % (2 * PERM_BLOCK) == 0
    x2d = x.reshape(t, d)
    h0 = _rmsnorm_orders(x2d, p["l0_norm_pre"])
    proj0 = _inproj0(h0, p["l0_w_in"], p["rope"], seq)
    outs, lses = zip(*[_attention_group(proj0, g, bsz, seq) for g in range(N_GROUPS)])
    x1, h1 = _outproj0(outs, lses, proj0, p["l0_w_out"], x2d, p["l0_norm_post"], p["l1_norm_pre"])
    proj1, gates = _inproj1(h1, p["l1_w_main"], p["l1_w_gate"])
    prep = _gate_prep(gates, p["l1_b_gate"], bsz, seq)
    hs = _mlstm(proj1, prep, bsz, seq)
    y = _outproj1(hs.reshape(t, M_V_WIDTH), proj1, p["l1_head_norm"], p["l1_w_out"], x1, p["l1_norm_post"])
    return y.reshape(bsz, seq, d)


def kernel(x_prompt, x_sample, l0_norm_pre, l0_w_in, l0_w_out, l0_norm_post,
           l1_norm_pre, l1_w_in, l1_b_gate, l1_head_norm, l1_w_out, l1_norm_post):
    assert x_prompt.shape[1] == x_sample.shape[1]
    w_gate, b_gate = _gate_weights(l1_w_in, l1_b_gate)
    p = {
        "l0_norm_pre": l0_norm_pre, "l0_w_in": l0_w_in.astype(BF16), "l0_w_out": l0_w_out.astype(BF16),
        "l0_norm_post": l0_norm_post, "l1_norm_pre": l1_norm_pre,
        "l1_w_main": l1_w_in.astype(BF16), "l1_w_gate": w_gate, "l1_b_gate": b_gate,
        "l1_head_norm": l1_head_norm, "l1_w_out": l1_w_out.astype(BF16), "l1_norm_post": l1_norm_post,
        "rope": _rope_tables(x_prompt.shape[1]),
    }
    return (_trunk(x_prompt, p), _trunk(x_sample, p))
```

```python
import functools

import jax
import jax.numpy as jnp
from jax import lax
from jax.experimental import pallas as pl
from jax.experimental.pallas import tpu as pltpu

F32 = jnp.float32
BF16 = jnp.bfloat16

D_MODEL = 2048
ATT_GROUPS = ((128, 1), (512, 4), (2048, 16))
N_GROUPS = len(ATT_GROUPS)
ATT_HEADS = 16
ATT_HEAD_DIM = 128
ATT_WIDTH = ATT_HEADS * ATT_HEAD_DIM
ATT_IN_COLS = 3 * N_GROUPS * ATT_WIDTH + ATT_WIDTH
ATT_HALF = 64
ROPE_THETA = 500000.0
ROPE_DIMS = ATT_HEAD_DIM // 4
ROPE_HALF = ROPE_DIMS // 2
LOG2E = 1.4426950408889634
ATT_Q_SCALE = ATT_HEAD_DIM ** -0.5 * LOG2E

M_HEADS = 8
M_QK_DIM = 256
M_V_DIM = 512
M_QK_WIDTH = M_HEADS * M_QK_DIM
M_V_WIDTH = M_HEADS * M_V_DIM
M_MAIN_COLS = 2 * M_QK_WIDTH + 3 * M_V_WIDTH
M_CHUNK = 256
GATE_LANES = 128

NORM_EPS = 1e-6
NEG_INF = -1e30

LANES = 128
VMEM_CAP_BYTES = 56 * 1024 * 1024


def _vmem_limit(block_bytes, scratch_bytes=0, temp_bytes=0):
    need = 2 * block_bytes + scratch_bytes + temp_bytes
    return int(min(max(need, 16 * 1024 * 1024), VMEM_CAP_BYTES))


def _nbytes(shape, dtype):
    n = 1
    for s in shape:
        n *= s
    return n * jnp.dtype(dtype).itemsize


PERM_BLOCK = 1024


RMS_PIECE = 256


def _rmsnorm_orders_kernel(x_ref, g_ref, o_ref, slab_scr):
    nslab = x_ref.shape[1] // LANES
    pr = RMS_PIECE

    def piece(pi, _):
        r0 = pl.multiple_of(pi * pr, pr)
        x = x_ref[pl.ds(r0, pr), :]
        inv = lax.rsqrt(jnp.mean(x * x, axis=-1, keepdims=True) + NORM_EPS)
        for c in range(nslab):
            ls = slice(c * LANES, (c + 1) * LANES)
            y = x[:, ls] * inv * g_ref[:, ls]
            slab_scr[c] = y
            for g, (_, dil) in enumerate(ATT_GROUPS):
                if dil == 1:
                    o_ref[g, pl.ds(r0, pr), ls] = y.astype(o_ref.dtype)
        for g, (_, dil) in enumerate(ATT_GROUPS):
            if dil == 1:
                continue
            n = PERM_BLOCK // dil
            npc = pr // dil
            for r in range(dil):
                dst = pl.multiple_of(r * n + pi * npc, npc)
                for c in range(nslab):
                    o_ref[g, pl.ds(dst, npc), c * LANES:(c + 1) * LANES] = (
                        slab_scr[c, pl.ds(r, npc, stride=dil), :].astype(o_ref.dtype))
        return 0

    lax.fori_loop(0, x_ref.shape[0] // pr, piece, 0)


def _rmsnorm_orders(x2d, gain):
    t, d = x2d.shape
    tm = PERM_BLOCK
    return pl.pallas_call(
        _rmsnorm_orders_kernel,
        grid=(t // tm,),
        in_specs=[pl.BlockSpec((tm, d), lambda i: (i, 0)), pl.BlockSpec((1, d), lambda i: (0, 0))],
        out_specs=pl.BlockSpec((N_GROUPS, tm, d), lambda i: (0, i, 0)),
        out_shape=jax.ShapeDtypeStruct((N_GROUPS, t, d), BF16),
        scratch_shapes=[pltpu.VMEM((d // LANES, RMS_PIECE, LANES), F32)],
        compiler_params=pltpu.CompilerParams(
            dimension_semantics=("arbitrary",),
            vmem_limit_bytes=_vmem_limit(_nbytes((tm, d), F32) + _nbytes((N_GROUPS, tm, d), BF16),
                                         _nbytes((RMS_PIECE, d), F32), 2 * _nbytes((RMS_PIECE, d), F32)),
        ),
        name="rmsnorm_orders",
    )(x2d, gain.reshape(1, d).astype(F32))


def _order_index(seq, dil):
    n = PERM_BLOCK // dil
    return jnp.arange(seq, dtype=jnp.int32).reshape(seq // PERM_BLOCK, n, dil).transpose(0, 2, 1).reshape(seq)


INPROJ0_TN = 2048
INPROJ0_CHUNK = 256
INPROJ0_TAIL_ROWS = ((0, PERM_BLOCK // 2), (PERM_BLOCK // 2, 3 * PERM_BLOCK // 4), (3 * PERM_BLOCK // 4, PERM_BLOCK))
_GROUP_TILES = 3 * ATT_WIDTH // INPROJ0_TN
_Z_TILES = ATT_WIDTH // INPROJ0_TN


def _inproj0_col_tile(jj):
    return jnp.where(jj < _GROUP_TILES, jj,
                     jnp.where(jj < _GROUP_TILES + _Z_TILES, jj + (N_GROUPS - 1) * _GROUP_TILES, jj - _Z_TILES))


def _inproj0_order(jj):
    return jnp.where(jj < _GROUP_TILES + _Z_TILES, 0, (jj - _Z_TILES) // _GROUP_TILES)


def _inproj0_kernel(h_ref, w_ref, c_ref, s1_ref, s2_ref, o_ref):
    col = _inproj0_col_tile(pl.program_id(1))
    seg = (col * INPROJ0_TN) // ATT_WIDTH
    kind = seg % 3
    is_rope = jnp.logical_and(seg < 3 * N_GROUPS, kind < 2)
    nchunk = INPROJ0_TN // INPROJ0_CHUNK
    tm = o_ref.shape[0]

    def pieces():
        for cc in range(nchunk):
            cols = slice(cc * INPROJ0_CHUNK, (cc + 1) * INPROJ0_CHUNK)
            if cc < nchunk - 1:
                yield slice(0, tm), cols
            else:
                for lo, hi in INPROJ0_TAIL_ROWS:
                    yield slice(lo, hi), cols

    @pl.when(is_rope)
    def _():
        scale = jnp.where(kind == 0, ATT_Q_SCALE, 1.0).astype(F32)
        c_all = c_ref[...] * scale
        s1_all = s1_ref[...] * scale
        s2_all = s2_ref[...] * scale
        for rows, cols in pieces():
            c, s1, s2 = c_all[rows], s1_all[rows], s2_all[rows]
            acc = jnp.dot(h_ref[rows, :], w_ref[:, cols], preferred_element_type=F32)
            for t in range(INPROJ0_CHUNK // LANES):
                a = acc[:, t * LANES:(t + 1) * LANES]
                r = a * c + pltpu.roll(a, LANES - ROPE_HALF, 1) * s1 + pltpu.roll(a, ROPE_HALF, 1) * s2
                lo = cols.start + t * LANES
                o_ref[rows, lo:lo + LANES] = r.astype(o_ref.dtype)

    @pl.when(jnp.logical_not(is_rope))
    def _():
        for rows, cols in pieces():
            o_ref[rows, cols] = jnp.dot(h_ref[rows, :], w_ref[:, cols], preferred_element_type=F32).astype(o_ref.dtype)


def _rope_tables(seq):
    inv = jnp.power(ROPE_THETA, -jnp.arange(ROPE_HALF, dtype=F32) / ROPE_HALF)
    pos = jnp.stack([_order_index(seq, dil) for _, dil in ATT_GROUPS]).astype(F32)
    ang = pos[:, :, None] * inv[None, None, :]
    cos, sin = jnp.cos(ang), jnp.sin(ang)
    zeros = jnp.zeros((N_GROUPS, seq, LANES - ROPE_DIMS), F32)
    zh = jnp.zeros((N_GROUPS, seq, ROPE_HALF), F32)
    c = jnp.concatenate([cos, cos, jnp.ones((N_GROUPS, seq, LANES - ROPE_DIMS), F32)], axis=2)
    s1 = jnp.concatenate([-sin, zh, zeros], axis=2)
    s2 = jnp.concatenate([zh, sin, zeros], axis=2)
    return c, s1, s2


def _inproj0(h_orders, w, tables, seq):
    _, t, k = h_orders.shape
    n = w.shape[1]
    tm, tn = PERM_BLOCK, INPROJ0_TN
    pos_blocks = seq // tm
    tab_spec = pl.BlockSpec((None, tm, LANES), lambda i, j: (_inproj0_order(j), i % pos_blocks, 0))
    blocks = _nbytes((tm, k), BF16) + _nbytes((k, tn), BF16) + _nbytes((tm, tn), BF16) + 3 * _nbytes((tm, LANES), F32)
    return pl.pallas_call(
        _inproj0_kernel,
        grid=(t // tm, n // tn),
        in_specs=[
            pl.BlockSpec((None, tm, k), lambda i, j: (_inproj0_order(j), i, 0)),
            pl.BlockSpec((k, tn), lambda i, j: (0, _inproj0_col_tile(j))),
            tab_spec, tab_spec, tab_spec,
        ],
        out_specs=pl.BlockSpec((tm, tn), lambda i, j: (i, _inproj0_col_tile(j))),
        out_shape=jax.ShapeDtypeStruct((t, n), BF16),
        compiler_params=pltpu.CompilerParams(
            dimension_semantics=("arbitrary", "arbitrary"),
            vmem_limit_bytes=_vmem_limit(blocks, 0, _nbytes((tm, k), BF16) + 3 * _nbytes((tm, LANES), F32)
                                         + 4 * _nbytes((tm, INPROJ0_CHUNK), F32)),
        ),
        name="inproj0",
    )(h_orders, w, *tables)


ATT_TQ = 512


def _attn_kernel(q_ref, kp_ref, kc_ref, kn_ref, vp_ref, vc_ref, vn_ref, o_ref, l_ref, qx, kx, vx, ox, lx, *, tq, ls):
    hb = ATT_HALF
    sub = 2 * hb
    w = qx.shape[2]
    rps = qx.shape[0]
    i = pl.program_id(2)
    bpt, n = q_ref.shape[1], q_ref.shape[3]

    row = lax.broadcasted_iota(jnp.int32, (sub, 2 * sub), 0)
    col = lax.broadcasted_iota(jnp.int32, (sub, 2 * sub), 1)
    band = jnp.abs(col - hb - row) <= hb
    lane = lax.broadcasted_iota(jnp.int32, (sub, LANES), 1)

    for rr in range(rps):
        qx[rr] = q_ref[0, :, rr].reshape(tq, w)
        kx[rr, 0:hb, :] = kp_ref[0, 0, rr]
        kx[rr, hb:hb + tq, :] = kc_ref[0, :, rr].reshape(tq, w)
        kx[rr, hb + tq:, :] = kn_ref[0, 0, rr]
        vx[rr, 0:hb, :] = vp_ref[0, 0, rr]
        vx[rr, hb:hb + tq, :] = vc_ref[0, :, rr].reshape(tq, w)
        vx[rr, hb + tq:, :] = vn_ref[0, 0, rr]

    for rr in range(rps):
        for a in range(tq // sub):
            r0 = a * sub
            kidx = i * tq + r0 + col - hb
            valid = band & (kidx >= 0) & (kidx < ls)
            lse_tile = jnp.zeros((sub, LANES), F32)
            for h in range(ATT_HEADS):
                hs = slice(h * ATT_HEAD_DIM, (h + 1) * ATT_HEAD_DIM)
                qh = qx[rr, r0:r0 + sub, hs]
                kh = kx[rr, r0:r0 + 2 * sub, hs]
                vh = vx[rr, r0:r0 + 2 * sub, hs]
                s = lax.dot_general(qh, kh, (((1,), (1,)), ((), ())), preferred_element_type=F32)
                s = jnp.where(valid, s, NEG_INF)
                m = jnp.max(s, axis=-1, keepdims=True)
                p = jnp.exp2(s - m)
                den = jnp.sum(p, axis=-1, keepdims=True)
                o = jnp.dot(p.astype(BF16), vh, preferred_element_type=F32) * (1.0 / den)
                lse_tile = jnp.where(lane == h, m + jnp.log2(den), lse_tile)
                ox[rr, r0:r0 + sub, hs] = o.astype(ox.dtype)
            lx[rr, r0:r0 + sub, :] = lse_tile
        o_ref[0, :, rr] = ox[rr].reshape(bpt, n, w)
        l_ref[0, :, rr] = lx[rr].reshape(bpt, n, LANES)


def _attention_group(proj, g, bsz, seq):
    _, dil = ATT_GROUPS[g]
    w = ATT_WIDTH
    hb = ATT_HALF
    ls = seq // dil
    if dil == 1:
        n = ATT_TQ // 2
        nb = seq // n
    else:
        nb, n = seq // PERM_BLOCK, PERM_BLOCK // dil
    tq = min(ATT_TQ, ls)
    bpt = tq // n
    rps = min(ATT_TQ // tq, dil)
    assert tq % (2 * hb) == 0 and n % hb == 0 and tq % n == 0 and nb % bpt == 0 and dil % rps == 0
    pv = proj.reshape(bsz, nb, dil, n, ATT_IN_COLS)
    ncb = 3 * g

    grid = (bsz, dil // rps, nb // bpt)
    main = lambda cb: pl.BlockSpec((1, bpt, rps, n, w), lambda b, r, i: (b, i, r, 0, cb))
    prev_h = lambda cb: pl.BlockSpec((1, 1, rps, hb, w),
                                     lambda b, r, i: (b, jnp.maximum(i * bpt - 1, 0), r, n // hb - 1, cb))
    next_h = lambda cb: pl.BlockSpec((1, 1, rps, hb, w),
                                     lambda b, r, i: (b, jnp.minimum((i + 1) * bpt, nb - 1), r, 0, cb))
    out_block = lambda width: pl.BlockSpec((1, bpt, rps, n, width), lambda b, r, i: (b, i, r, 0, 0))

    in_specs = [main(ncb), prev_h(ncb + 1), main(ncb + 1), next_h(ncb + 1),
                prev_h(ncb + 2), main(ncb + 2), next_h(ncb + 2)]
    blocks = rps * (4 * _nbytes((tq, w), BF16) + 4 * _nbytes((hb, w), BF16) + _nbytes((tq, LANES), F32))
    scratch = [pltpu.VMEM((rps, tq, w), BF16), pltpu.VMEM((rps, tq + 2 * hb, w), BF16),
               pltpu.VMEM((rps, tq + 2 * hb, w), BF16), pltpu.VMEM((rps, tq, w), BF16),
               pltpu.VMEM((rps, tq, LANES), F32)]
    scratch_bytes = rps * (4 * _nbytes((tq + 2 * hb, w), BF16) + _nbytes((tq, LANES), F32))
    o, l = pl.pallas_call(
        functools.partial(_attn_kernel, tq=tq, ls=ls),
        grid=grid,
        in_specs=in_specs,
        out_specs=(out_block(w), out_block(LANES)),
        out_shape=(jax.ShapeDtypeStruct((bsz, nb, dil, n, w), BF16),
                   jax.ShapeDtypeStruct((bsz, nb, dil, n, LANES), F32)),
        scratch_shapes=scratch,
        compiler_params=pltpu.CompilerParams(
            dimension_semantics=("arbitrary", "arbitrary", "arbitrary"),
            vmem_limit_bytes=_vmem_limit(blocks, scratch_bytes, 8 * 1024 * 1024),
        ),
        name=f"attn_g{g}",
    )(*([pv] * 7))
    if dil > 1:
        o = o.transpose(0, 1, 3, 2, 4)
        l = l.transpose(0, 1, 3, 2, 4)
    return o.reshape(bsz * seq, w), l.reshape(bsz * seq, LANES)


def _post_norm_residual(out, x_ref, gpost_ref, o_ref, gnext_ref=None, hnext_ref=None):
    ms = jnp.mean(out * out, axis=-1, keepdims=True)
    x1 = x_ref[...] + out * lax.rsqrt(ms + NORM_EPS) * gpost_ref[...]
    o_ref[...] = x1
    if hnext_ref is not None:
        ms1 = jnp.mean(x1 * x1, axis=-1, keepdims=True)
        hnext_ref[...] = (x1 * lax.rsqrt(ms1 + NORM_EPS) * gnext_ref[...]).astype(hnext_ref.dtype)


OUTPROJ_KCHUNK = 512


def _sigmoid(x):
    return 0.5 * jnp.tanh(0.5 * x) + 0.5


def _silu(x):
    xh = 0.5 * x
    return xh * jnp.tanh(xh) + xh


def _outproj0_kernel(o0_ref, o1_ref, o2_ref, l0_ref, l1_ref, l2_ref, z_ref, w_ref, x_ref, gpost_ref, gnext_ref,
                     x1_ref, hnext_ref):
    o_refs = (o0_ref, o1_ref, o2_ref)
    lses = [r[...] for r in (l0_ref, l1_ref, l2_ref)]
    lmax = jnp.maximum(jnp.maximum(lses[0], lses[1]), lses[2])
    es = [jnp.exp2(l - lmax) for l in lses]
    tot = es[0] + es[1] + es[2]
    wts = [e / tot for e in es]
    heads_per_chunk = OUTPROJ_KCHUNK // ATT_HEAD_DIM
    out = None
    for c in range(ATT_HEADS // heads_per_chunk):
        ys = []
        for h in range(c * heads_per_chunk, (c + 1) * heads_per_chunk):
            hs = slice(h * ATT_HEAD_DIM, (h + 1) * ATT_HEAD_DIM)
            o = wts[0][:, h:h + 1] * o_refs[0][:, hs].astype(F32)
            for g in range(1, N_GROUPS):
                o = o + wts[g][:, h:h + 1] * o_refs[g][:, hs].astype(F32)
            ys.append((o * _silu(z_ref[:, hs].astype(F32))).astype(BF16))
        part = jnp.dot(jnp.concatenate(ys, axis=1), w_ref[c * OUTPROJ_KCHUNK:(c + 1) * OUTPROJ_KCHUNK, :],
                       preferred_element_type=F32)
        out = part if out is None else out + part
    _post_norm_residual(out, x_ref, gpost_ref, x1_ref, gnext_ref, hnext_ref)


def _outproj0(outs, lses, proj0, w, x2d, g_post, g_next, tm=256):
    t, d = x2d.shape
    k = w.shape[0]
    row = lambda i: (i, 0)
    vec = pl.BlockSpec((1, d), lambda i: (0, 0))
    zb = 3 * N_GROUPS
    in_specs = ([pl.BlockSpec((tm, k), row)] * N_GROUPS + [pl.BlockSpec((tm, LANES), row)] * N_GROUPS
                + [pl.BlockSpec((tm, k), lambda i: (i, zb)),
                   pl.BlockSpec((k, d), lambda i: (0, 0), pipeline_mode=pl.Buffered(1)),
                   pl.BlockSpec((tm, d), row), vec, vec])
    blocks = ((N_GROUPS + 1) * _nbytes((tm, k), BF16) + N_GROUPS * _nbytes((tm, LANES), F32)
              + 2 * _nbytes((tm, d), F32) + _nbytes((tm, d), BF16))
    return pl.pallas_call(
        _outproj0_kernel,
        grid=(t // tm,),
        in_specs=in_specs,
        out_specs=(pl.BlockSpec((tm, d), row), pl.BlockSpec((tm, d), row)),
        out_shape=(jax.ShapeDtypeStruct((t, d), F32), jax.ShapeDtypeStruct((t, d), BF16)),
        compiler_params=pltpu.CompilerParams(
            dimension_semantics=("arbitrary",),
            vmem_limit_bytes=_vmem_limit(blocks, _nbytes((k, d), BF16), 4 * _nbytes((tm, d), F32)),
        ),
        name="outproj0",
    )(*outs, *lses, proj0, w, x2d, g_post.reshape(1, d).astype(F32), g_next.reshape(1, d).astype(F32))


def _outproj1_kernel(hs_ref, og_ref, z_ref, hgain_ref, w_ref, x_ref, gpost_ref, o_ref):
    assert OUTPROJ_KCHUNK == M_V_DIM
    out = None
    for c in range(hs_ref.shape[1] // OUTPROJ_KCHUNK):
        ks = slice(c * OUTPROJ_KCHUNK, (c + 1) * OUTPROJ_KCHUNK)
        hs = hs_ref[:, ks].astype(F32)
        hn = hs * lax.rsqrt(jnp.mean(hs * hs, axis=-1, keepdims=True) + NORM_EPS) * hgain_ref[:, ks]
        y = (og_ref[:, ks].astype(F32) * hn) * z_ref[:, ks].astype(F32)
        part = jnp.dot(y.astype(BF16), w_ref[ks, :], preferred_element_type=F32)
        out = part if out is None else out + part
    _post_norm_residual(out, x_ref, gpost_ref, o_ref)


def _outproj1(hs, proj1, head_gain, w, x2d, g_post, tm=256):
    t, d = x2d.shape
    k = w.shape[0]
    row = lambda i: (i, 0)
    ob = M_OGATE_COL // k
    blocks = 3 * _nbytes((tm, k), BF16) + _nbytes((1, k), F32) + 2 * _nbytes((tm, d), F32)
    return pl.pallas_call(
        _outproj1_kernel,
        grid=(t // tm,),
        in_specs=[pl.BlockSpec((tm, k), row), pl.BlockSpec((tm, k), lambda i: (i, ob)),
                  pl.BlockSpec((tm, k), lambda i: (i, ob + 1)),
                  pl.BlockSpec((1, k), lambda i: (0, 0)),
                  pl.BlockSpec((k, d), lambda i: (0, 0), pipeline_mode=pl.Buffered(1)),
                  pl.BlockSpec((tm, d), row), pl.BlockSpec((1, d), lambda i: (0, 0))],
        out_specs=pl.BlockSpec((tm, d), row),
        out_shape=jax.ShapeDtypeStruct((t, d), F32),
        compiler_params=pltpu.CompilerParams(
            dimension_semantics=("arbitrary",),
            vmem_limit_bytes=_vmem_limit(blocks, _nbytes((k, d), BF16), 4 * _nbytes((tm, k), F32)),
        ),
        name="outproj1",
    )(hs, proj1, proj1, head_gain.reshape(1, k).astype(F32), w, x2d, g_post.reshape(1, d).astype(F32))


INPROJ1_CHUNK = 512


M_OGATE_COL = 2 * M_QK_WIDTH + M_V_WIDTH
M_ZGATE_COL = M_OGATE_COL + M_V_WIDTH


def _inproj1_kernel(h_ref, w_ref, wg_ref, o_ref, g_ref):
    tn = o_ref.shape[1]
    col0 = pl.program_id(1) * tn

    def tile(act):
        for cc in range(tn // INPROJ1_CHUNK):
            cs = slice(cc * INPROJ1_CHUNK, (cc + 1) * INPROJ1_CHUNK)
            acc = jnp.dot(h_ref[...], w_ref[:, cs], preferred_element_type=F32)
            o_ref[:, cs] = (acc if act is None else act(acc)).astype(o_ref.dtype)

    pl.when(col0 < M_OGATE_COL)(lambda: tile(None))
    pl.when(jnp.logical_and(col0 >= M_OGATE_COL, col0 < M_ZGATE_COL))(lambda: tile(_sigmoid))
    pl.when(col0 >= M_ZGATE_COL)(lambda: tile(_silu))

    @pl.when(pl.program_id(1) == 0)
    def _():
        g_ref[...] = jnp.dot(h_ref[...], wg_ref[...], preferred_element_type=F32)


def _inproj1(h, w, wg, tm=1024, tn=2048):
    t, k = h.shape
    n = M_MAIN_COLS
    tm = min(tm, t)
    blocks = (_nbytes((tm, k), BF16) + _nbytes((k, tn), BF16) + _nbytes((k, GATE_LANES), BF16)
              + _nbytes((tm, tn), BF16) + _nbytes((tm, GATE_LANES), F32))
    return pl.pallas_call(
        _inproj1_kernel,
        grid=(t // tm, n // tn),
        in_specs=[
            pl.BlockSpec((tm, k), lambda i, j: (i, 0)),
            pl.BlockSpec((k, tn), lambda i, j: (0, j)),
            pl.BlockSpec((k, GATE_LANES), lambda i, j: (0, 0)),
        ],
        out_specs=(pl.BlockSpec((tm, tn), lambda i, j: (i, j)), pl.BlockSpec((tm, GATE_LANES), lambda i, j: (i, 0))),
        out_shape=(jax.ShapeDtypeStruct((t, n), BF16), jax.ShapeDtypeStruct((t, GATE_LANES), F32)),
        compiler_params=pltpu.CompilerParams(
            dimension_semantics=("arbitrary", "arbitrary"),
            vmem_limit_bytes=_vmem_limit(blocks, 0, 4 * _nbytes((tm, INPROJ1_CHUNK), F32)),
        ),
        name="inproj1",
    )(h, w, wg)


def _scan_rows(x, rows, op, fill, reverse):
    n = x.shape[0]
    k = 1
    while k < n:
        if reverse:
            shifted = jnp.where(rows < n - k, pltpu.roll(x, n - k, 0), fill)
        else:
            shifted = jnp.where(rows >= k, pltpu.roll(x, k, 0), fill)
        x = op(x, shifted)
        k *= 2
    return x


GATE_PREP_CHUNKS = 4


def _gate_prep_kernel(g_ref, b_ref, o_ref):
    for ci in range(GATE_PREP_CHUNKS):
        _gate_prep_chunk(g_ref[0, ci * M_CHUNK:(ci + 1) * M_CHUNK, :] + b_ref[...], o_ref, ci)


def _gate_prep_chunk(g, o_ref, ci):
    rows = lax.broadcasted_iota(jnp.int32, g.shape, 0)
    kind = lax.broadcasted_iota(jnp.int32, g.shape, 1) % 8
    lf = jnp.minimum(g, 0.0) - jnp.log1p(jnp.exp(-jnp.abs(g)))
    csum = _scan_rows(lf, rows, jnp.add, 0.0, reverse=False)
    rsum = _scan_rows(lf, rows, jnp.add, 0.0, reverse=True)
    b_at_i = pltpu.roll(jnp.where(kind == 1, csum, rsum), LANES - 1, 1)
    grow = g - b_at_i
    cm_f = _scan_rows(grow, rows, jnp.maximum, NEG_INF, reverse=False)
    cm_b = _scan_rows(grow, rows, jnp.maximum, NEG_INF, reverse=True)
    y = jnp.where(kind == 0, b_at_i, 0.0)
    y = jnp.where(kind == 1, pltpu.roll(grow, 1, 1), y)
    y = jnp.where(kind == 2, pltpu.roll(cm_f, 2, 1), y)
    y = jnp.where(kind == 3, rsum, y)
    y = jnp.where(kind == 4, pltpu.roll(grow, 2, 1), y)
    y = jnp.where(kind == 5, pltpu.roll(cm_b, 3, 1), y)
    yt = y.T
    o_ref[0, :, ci] = yt[:M_HEADS * 8].reshape(M_HEADS, 8, g.shape[0])


def _gate_prep(gates, bias, bsz, seq):
    nc = seq // M_CHUNK
    per = GATE_PREP_CHUNKS
    return pl.pallas_call(
        _gate_prep_kernel,
        grid=(bsz, nc // per),
        in_specs=[pl.BlockSpec((1, per * M_CHUNK, GATE_LANES), lambda b, c: (b, c, 0)),
                  pl.BlockSpec((1, GATE_LANES), lambda b, c: (0, 0))],
        out_specs=pl.BlockSpec((1, M_HEADS, per, 8, M_CHUNK), lambda b, c: (b, 0, c, 0, 0)),
        out_shape=jax.ShapeDtypeStruct((bsz, M_HEADS, nc, 8, M_CHUNK), F32),
        compiler_params=pltpu.CompilerParams(dimension_semantics=("arbitrary", "arbitrary")),
        name="gate_prep",
    )(gates.reshape(bsz, seq, GATE_LANES), bias)


def _mlstm_kernel(q_ref, k_ref, v_ref, r_ref, o_ref, hs_scr, cf_scr, cb_scr, *, nc):
    L = M_CHUNK
    rows = lax.broadcasted_iota(jnp.int32, (L, L), 0)
    cols = lax.broadcasted_iota(jnp.int32, (L, L), 1)
    eye = rows == cols
    causal = cols <= rows
    anti = cols >= rows

    def to_col(r):
        return jnp.sum(jnp.where(eye, r, 0.0), axis=1, keepdims=True)

    def chunk(c, c_scr, n, m, fwd):
        st = pl.multiple_of(c * L, L)
        qc = q_ref[0, pl.ds(st, L), :] * (M_QK_DIM ** -0.5)
        kc = k_ref[0, pl.ds(st, L), :]
        vc = v_ref[0, pl.ds(st, L), :]
        r = r_ref[0, 0, pl.ds(c, 1)].reshape(8, L)
        base = 0 if fwd else 3
        grow = r[base + 1:base + 2]
        bcol = to_col(r[base:base + 1])
        gcol = to_col(grow)
        mm = jnp.maximum(to_col(r[base + 2:base + 3]), m)
        e = L - 1 if fwd else 0
        mm_l = mm[e:e + 1]
        s = lax.dot_general(qc, kc, (((1,), (1,)), ((), ())), preferred_element_type=F32)
        a = jnp.exp(jnp.where(causal if fwd else anti, grow - mm, NEG_INF)) * s
        gint = jnp.exp(m - mm)
        cmat = c_scr[...]
        num = (jnp.dot(a.astype(BF16), vc, preferred_element_type=F32)
               + gint * jnp.dot(qc, cmat.astype(BF16), preferred_element_type=F32))
        qn = jnp.sum(qc.astype(F32) * n, axis=1, keepdims=True)
        den = jnp.sum(a, axis=1, keepdims=True) + gint * qn
        hc = num * (1.0 / jnp.maximum(jnp.abs(den), jnp.exp(-(bcol + mm))))
        decay = jnp.exp(m - mm_l)
        wk = jnp.exp(gcol - mm_l) * kc.astype(F32)
        c_scr[...] = decay * cmat + lax.dot_general(wk.astype(BF16), vc, (((0,), (0,)), ((), ())),
                                                    preferred_element_type=F32)
        n_new = decay * n + jnp.sum(wk, axis=0, keepdims=True)
        m_new = bcol[e:e + 1] + mm_l
        return hc, n_new, m_new

    def finish(c, hsum):
        st = pl.multiple_of(c * L, L)
        o_ref[0, pl.ds(st, L), :] = hsum.astype(o_ref.dtype)

    def step(j, carry, second_half):
        nf, mf, nb, mb = carry
        cf, cb = j, nc - 1 - j
        hf, nf, mf = chunk(cf, cf_scr, nf, mf, True)
        hb, nb, mb = chunk(cb, cb_scr, nb, mb, False)
        sf = pl.multiple_of(cf * L, L)
        sb = pl.multiple_of(cb * L, L)
        if second_half:
            finish(cf, hs_scr[pl.ds(sf, L), :] + hf)
            finish(cb, hs_scr[pl.ds(sb, L), :] + hb)
        else:
            hs_scr[pl.ds(sf, L), :] = hf
            hs_scr[pl.ds(sb, L), :] = hb
        return nf, mf, nb, mb

    cf_scr[...] = jnp.zeros_like(cf_scr)
    cb_scr[...] = jnp.zeros_like(cb_scr)
    n0 = jnp.zeros((1, M_QK_DIM), F32)
    m0 = jnp.zeros((1, 1), F32)
    carry = lax.fori_loop(0, nc // 2, functools.partial(step, second_half=False), (n0, m0, n0, m0), unroll=4)
    lax.fori_loop(nc // 2, nc, functools.partial(step, second_half=True), carry, unroll=4)


def _mlstm(proj1, prep, bsz, seq):
    nc = seq // M_CHUNK
    assert nc % 2 == 0
    p3 = proj1.reshape(bsz, seq, M_MAIN_COLS)
    kb = M_QK_WIDTH // M_QK_DIM
    vb = 2 * M_QK_WIDTH // M_V_DIM
    blocks = (2 * _nbytes((seq, M_QK_DIM), BF16) + 2 * _nbytes((seq, M_V_DIM), BF16)
              + _nbytes((nc, 8, M_CHUNK), F32))
    scratch_bytes = _nbytes((seq, M_V_DIM), F32) + 2 * _nbytes((M_QK_DIM, M_V_DIM), F32)
    return pl.pallas_call(
        functools.partial(_mlstm_kernel, nc=nc),
        grid=(bsz, M_HEADS),
        in_specs=[
            pl.BlockSpec((1, seq, M_QK_DIM), lambda b, h: (b, 0, h)),
            pl.BlockSpec((1, seq, M_QK_DIM), lambda b, h: (b, 0, kb + h)),
            pl.BlockSpec((1, seq, M_V_DIM), lambda b, h: (b, 0, vb + h)),
            pl.BlockSpec((1, 1, nc, 8, M_CHUNK), lambda b, h: (b, h, 0, 0, 0)),
        ],
        out_specs=pl.BlockSpec((1, seq, M_V_DIM), lambda b, h: (b, 0, h)),
        out_shape=jax.ShapeDtypeStruct((bsz, seq, M_V_WIDTH), BF16),
        scratch_shapes=[pltpu.VMEM((seq, M_V_DIM), F32), pltpu.VMEM((M_QK_DIM, M_V_DIM), F32),
                        pltpu.VMEM((M_QK_DIM, M_V_DIM), F32)],
        compiler_params=pltpu.CompilerParams(
            dimension_semantics=("arbitrary", "arbitrary"),
            vmem_limit_bytes=_vmem_limit(blocks, scratch_bytes, 8 * 1024 * 1024),
        ),
        name="mlstm",
    )(p3, p3, p3, prep)


def _gate_weights(w_in1, b_gate):
    wg = w_in1[:, M_MAIN_COLS:].reshape(D_MODEL, 4, M_HEADS).transpose(0, 2, 1)
    wg = jnp.pad(wg, ((0, 0), (0, 0), (0, 4))).reshape(D_MODEL, M_HEADS * 8)
    wg = jnp.pad(wg, ((0, 0), (0, GATE_LANES - M_HEADS * 8)))
    bg = jnp.pad(b_gate.astype(F32).reshape(4, M_HEADS).T, ((0, 0), (0, 4))).reshape(1, M_HEADS * 8)
    bg = jnp.pad(bg, ((0, 0), (0, GATE_LANES - M_HEADS * 8)))
    return wg.astype(BF16), bg


def _trunk(x, p):
    bsz, seq, d = x.shape
    t = bsz * seq
    assert seq % (2 * PERM_BLOCK) == 0
    x2d = x.reshape(t, d)
    h0 = _rmsnorm_orders(x2d, p["l0_norm_pre"])
    proj0 = _inproj0(h0, p["l0_w_in"], p["rope"], seq)
    outs, lses = zip(*[_attention_group(proj0, g, bsz, seq) for g in range(N_GROUPS)])
    x1, h1 = _outproj0(outs, lses, proj0, p["l0_w_out"], x2d, p["l0_norm_post"], p["l1_norm_pre"])
    proj1, gates = _inproj1(h1, p["l1_w_main"], p["l1_w_gate"])
    prep = _gate_prep(gates, p["l1_b_gate"], bsz, seq)
    hs = _mlstm(proj1, prep, bsz, seq)
    y = _outproj1(hs.reshape(t, M_V_WIDTH), proj1, p["l1_head_norm"], p["l1_w_out"], x1, p["l1_norm_post"])
    return y.reshape(bsz, seq, d)


def kernel(x_prompt, x_sample, l0_norm_pre, l0_w_in, l0_w_out, l0_norm_post,
           l1_norm_pre, l1_w_in, l1_b_gate, l1_head_norm, l1_w_out, l1_norm_post):
    assert x_prompt.shape[1] == x_sample.shape[1]
    w_gate, b_gate = _gate_weights(l1_w_in, l1_b_gate)
    p = {
        "l0_norm_pre": l0_norm_pre, "l0_w_in": l0_w_in.astype(BF16), "l0_w_out": l0_w_out.astype(BF16),
        "l0_norm_post": l0_norm_post, "l1_norm_pre": l1_norm_pre,
        "l1_w_main": l1_w_in.astype(BF16), "l1_w_gate": w_gate, "l1_b_gate": b_gate,
        "l1_head_norm": l1_head_norm, "l1_w_out": l1_w_out.astype(BF16), "l1_norm_post": l1_norm_post,
        "rope": _rope_tables(x_prompt.shape[1]),
    }
    return (_trunk(x_prompt, p), _trunk(x_sample, p))
```

```python
import functools

import jax
import jax.numpy as jnp
from jax import lax
from jax.experimental import pallas as pl
from jax.experimental.pallas import tpu as pltpu

F32 = jnp.float32
BF16 = jnp.bfloat16

D_MODEL = 2048
ATT_GROUPS = ((128, 1), (512, 4), (2048, 16))
N_GROUPS = len(ATT_GROUPS)
ATT_HEADS = 16
ATT_HEAD_DIM = 128
ATT_WIDTH = ATT_HEADS * ATT_HEAD_DIM
ATT_IN_COLS = 3 * N_GROUPS * ATT_WIDTH + ATT_WIDTH
ATT_HALF = 64
ROPE_THETA = 500000.0
ROPE_DIMS = ATT_HEAD_DIM // 4
ROPE_HALF = ROPE_DIMS // 2
LOG2E = 1.4426950408889634
ATT_Q_SCALE = ATT_HEAD_DIM ** -0.5 * LOG2E

M_HEADS = 8
M_QK_DIM = 256
M_V_DIM = 512
M_QK_WIDTH = M_HEADS * M_QK_DIM
M_V_WIDTH = M_HEADS * M_V_DIM
M_MAIN_COLS = 2 * M_QK_WIDTH + 3 * M_V_WIDTH
M_CHUNK = 256
GATE_LANES = 128

NORM_EPS = 1e-6
NEG_INF = -1e30

LANES = 128
VMEM_CAP_BYTES = 56 * 1024 * 1024


def _vmem_limit(block_bytes, scratch_bytes=0, temp_bytes=0):
    need = 2 * block_bytes + scratch_bytes + temp_bytes
    return int(min(max(need, 16 * 1024 * 1024), VMEM_CAP_BYTES))


def _nbytes(shape, dtype):
    n = 1
    for s in shape:
        n *= s
    return n * jnp.dtype(dtype).itemsize


PERM_BLOCK = 1024


RMS_PIECE = 128


def _rmsnorm_orders_kernel(x_ref, g_ref, o_ref, slab_scr):
    nslab = x_ref.shape[1] // LANES
    pr = RMS_PIECE

    def piece(pi, _):
        r0 = pl.multiple_of(pi * pr, pr)
        x = x_ref[pl.ds(r0, pr), :]
        inv = lax.rsqrt(jnp.mean(x * x, axis=-1, keepdims=True) + NORM_EPS)
        for c in range(nslab):
            ls = slice(c * LANES, (c + 1) * LANES)
            y = x[:, ls] * inv * g_ref[:, ls]
            slab_scr[c] = y
            for g, (_, dil) in enumerate(ATT_GROUPS):
                if dil == 1:
                    o_ref[g, pl.ds(r0, pr), ls] = y.astype(o_ref.dtype)
        for g, (_, dil) in enumerate(ATT_GROUPS):
            if dil == 1:
                continue
            n = PERM_BLOCK // dil
            npc = pr // dil
            for r in range(dil):
                dst = pl.multiple_of(r * n + pi * npc, npc)
                for c in range(nslab):
                    o_ref[g, pl.ds(dst, npc), c * LANES:(c + 1) * LANES] = (
                        slab_scr[c, pl.ds(r, npc, stride=dil), :].astype(o_ref.dtype))
        return 0

    lax.fori_loop(0, x_ref.shape[0] // pr, piece, 0)


def _rmsnorm_orders(x2d, gain):
    t, d = x2d.shape
    tm = PERM_BLOCK
    return pl.pallas_call(
        _rmsnorm_orders_kernel,
        grid=(t // tm,),
        in_specs=[pl.BlockSpec((tm, d), lambda i: (i, 0)), pl.BlockSpec((1, d), lambda i: (0, 0))],
        out_specs=pl.BlockSpec((N_GROUPS, tm, d), lambda i: (0, i, 0)),
        out_shape=jax.ShapeDtypeStruct((N_GROUPS, t, d), BF16),
        scratch_shapes=[pltpu.VMEM((d // LANES, RMS_PIECE, LANES), F32)],
        compiler_params=pltpu.CompilerParams(
            dimension_semantics=("arbitrary",),
            vmem_limit_bytes=_vmem_limit(_nbytes((tm, d), F32) + _nbytes((N_GROUPS, tm, d), BF16),
                                         _nbytes((RMS_PIECE, d), F32), 2 * _nbytes((RMS_PIECE, d), F32)),
        ),
        name="rmsnorm_orders",
    )(x2d, gain.reshape(1, d).astype(F32))


def _order_index(seq, dil):
    n = PERM_BLOCK // dil
    return jnp.arange(seq, dtype=jnp.int32).reshape(seq // PERM_BLOCK, n, dil).transpose(0, 2, 1).reshape(seq)


INPROJ0_TN = 2048
INPROJ0_CHUNK = 256
INPROJ0_TAIL_ROWS = ((0, PERM_BLOCK // 2), (PERM_BLOCK // 2, 3 * PERM_BLOCK // 4), (3 * PERM_BLOCK // 4, PERM_BLOCK))
_GROUP_TILES = 3 * ATT_WIDTH // INPROJ0_TN
_Z_TILES = ATT_WIDTH // INPROJ0_TN


def _inproj0_col_tile(jj):
    return jnp.where(jj < _GROUP_TILES, jj,
                     jnp.where(jj < _GROUP_TILES + _Z_TILES, jj + (N_GROUPS - 1) * _GROUP_TILES, jj - _Z_TILES))


def _inproj0_order(jj):
    return jnp.where(jj < _GROUP_TILES + _Z_TILES, 0, (jj - _Z_TILES) // _GROUP_TILES)


def _inproj0_kernel(h_ref, w_ref, c_ref, s1_ref, s2_ref, o_ref):
    col = _inproj0_col_tile(pl.program_id(1))
    seg = (col * INPROJ0_TN) // ATT_WIDTH
    kind = seg % 3
    is_rope = jnp.logical_and(seg < 3 * N_GROUPS, kind < 2)
    nchunk = INPROJ0_TN // INPROJ0_CHUNK
    tm = o_ref.shape[0]

    def pieces():
        for cc in range(nchunk):
            cols = slice(cc * INPROJ0_CHUNK, (cc + 1) * INPROJ0_CHUNK)
            if cc < nchunk - 1:
                yield slice(0, tm), cols
            else:
                for lo, hi in INPROJ0_TAIL_ROWS:
                    yield slice(lo, hi), cols

    @pl.when(is_rope)
    def _():
        scale = jnp.where(kind == 0, ATT_Q_SCALE, 1.0).astype(F32)
        c_all = c_ref[...] * scale
        s1_all = s1_ref[...] * scale
        s2_all = s2_ref[...] * scale
        for rows, cols in pieces():
            c, s1, s2 = c_all[rows], s1_all[rows], s2_all[rows]
            acc = jnp.dot(h_ref[rows, :], w_ref[:, cols], preferred_element_type=F32)
            for t in range(INPROJ0_CHUNK // LANES):
                a = acc[:, t * LANES:(t + 1) * LANES]
                r = a * c + pltpu.roll(a, LANES - ROPE_HALF, 1) * s1 + pltpu.roll(a, ROPE_HALF, 1) * s2
                lo = cols.start + t * LANES
                o_ref[rows, lo:lo + LANES] = r.astype(o_ref.dtype)

    @pl.when(jnp.logical_not(is_rope))
    def _():
        for rows, cols in pieces():
            o_ref[rows, cols] = jnp.dot(h_ref[rows, :], w_ref[:, cols], preferred_element_type=F32).astype(o_ref.dtype)


def _rope_tables(seq):
    inv = jnp.power(ROPE_THETA, -jnp.arange(ROPE_HALF, dtype=F32) / ROPE_HALF)
    pos = jnp.stack([_order_index(seq, dil) for _, dil in ATT_GROUPS]).astype(F32)
    ang = pos[:, :, None] * inv[None, None, :]
    cos, sin = jnp.cos(ang), jnp.sin(ang)
    zeros = jnp.zeros((N_GROUPS, seq, LANES - ROPE_DIMS), F32)
    zh = jnp.zeros((N_GROUPS, seq, ROPE_HALF), F32)
    c = jnp.concatenate([cos, cos, jnp.ones((N_GROUPS, seq, LANES - ROPE_DIMS), F32)], axis=2)
    s1 = jnp.concatenate([-sin, zh, zeros], axis=2)
    s2 = jnp.concatenate([zh, sin, zeros], axis=2)
    return c, s1, s2


def _inproj0(h_orders, w, tables, seq):
    _, t, k = h_orders.shape
    n = w.shape[1]
    tm, tn = PERM_BLOCK, INPROJ0_TN
    pos_blocks = seq // tm
    tab_spec = pl.BlockSpec((None, tm, LANES), lambda i, j: (_inproj0_order(j), i % pos_blocks, 0))
    blocks = _nbytes((tm, k), BF16) + _nbytes((k, tn), BF16) + _nbytes((tm, tn), BF16) + 3 * _nbytes((tm, LANES), F32)
    return pl.pallas_call(
        _inproj0_kernel,
        grid=(t // tm, n // tn),
        in_specs=[
            pl.BlockSpec((None, tm, k), lambda i, j: (_inproj0_order(j), i, 0)),
            pl.BlockSpec((k, tn), lambda i, j: (0, _inproj0_col_tile(j))),
            tab_spec, tab_spec, tab_spec,
        ],
        out_specs=pl.BlockSpec((tm, tn), lambda i, j: (i, _inproj0_col_tile(j))),
        out_shape=jax.ShapeDtypeStruct((t, n), BF16),
        compiler_params=pltpu.CompilerParams(
            dimension_semantics=("arbitrary", "arbitrary"),
            vmem_limit_bytes=_vmem_limit(blocks, 0, _nbytes((tm, k), BF16) + 3 * _nbytes((tm, LANES), F32)
                                         + 4 * _nbytes((tm, INPROJ0_CHUNK), F32)),
        ),
        name="inproj0",
    )(h_orders, w, *tables)


ATT_TQ = 512


def _attn_kernel(q_ref, kp_ref, kc_ref, kn_ref, vp_ref, vc_ref, vn_ref, o_ref, l_ref, qx, kx, vx, ox, lx, *, tq, ls):
    hb = ATT_HALF
    sub = 2 * hb
    w = qx.shape[2]
    rps = qx.shape[0]
    i = pl.program_id(2)
    bpt, n = q_ref.shape[1], q_ref.shape[3]

    row = lax.broadcasted_iota(jnp.int32, (sub, 2 * sub), 0)
    col = lax.broadcasted_iota(jnp.int32, (sub, 2 * sub), 1)
    band = jnp.abs(col - hb - row) <= hb
    lane = lax.broadcasted_iota(jnp.int32, (sub, LANES), 1)

    for rr in range(rps):
        qx[rr] = q_ref[0, :, rr].reshape(tq, w)
        kx[rr, 0:hb, :] = kp_ref[0, 0, rr]
        kx[rr, hb:hb + tq, :] = kc_ref[0, :, rr].reshape(tq, w)
        kx[rr, hb + tq:, :] = kn_ref[0, 0, rr]
        vx[rr, 0:hb, :] = vp_ref[0, 0, rr]
        vx[rr, hb:hb + tq, :] = vc_ref[0, :, rr].reshape(tq, w)
        vx[rr, hb + tq:, :] = vn_ref[0, 0, rr]

    for rr in range(rps):
        for a in range(tq // sub):
            r0 = a * sub
            kidx = i * tq + r0 + col - hb
            valid = band & (kidx >= 0) & (kidx < ls)
            lse_tile = jnp.zeros((sub, LANES), F32)
            for h in range(ATT_HEADS):
                hs = slice(h * ATT_HEAD_DIM, (h + 1) * ATT_HEAD_DIM)
                qh = qx[rr, r0:r0 + sub, hs]
                kh = kx[rr, r0:r0 + 2 * sub, hs]
                vh = vx[rr, r0:r0 + 2 * sub, hs]
                s = lax.dot_general(qh, kh, (((1,), (1,)), ((), ())), preferred_element_type=F32)
                s = jnp.where(valid, s, NEG_INF)
                m = jnp.max(s, axis=-1, keepdims=True)
                p = jnp.exp2(s - m)
                den = jnp.sum(p, axis=-1, keepdims=True)
                o = jnp.dot(p.astype(BF16), vh, preferred_element_type=F32) * (1.0 / den)
                lse_tile = jnp.where(lane == h, m + jnp.log2(den), lse_tile)
                ox[rr, r0:r0 + sub, hs] = o.astype(ox.dtype)
            lx[rr, r0:r0 + sub, :] = lse_tile
        o_ref[0, :, rr] = ox[rr].reshape(bpt, n, w)
        l_ref[0, :, rr] = lx[rr].reshape(bpt, n, LANES)


def _attention_group(proj, g, bsz, seq):
    _, dil = ATT_GROUPS[g]
    w = ATT_WIDTH
    hb = ATT_HALF
    ls = seq // dil
    if dil == 1:
        n = ATT_TQ // 2
        nb = seq // n
    else:
        nb, n = seq // PERM_BLOCK, PERM_BLOCK // dil
    tq = min(ATT_TQ, ls)
    bpt = tq // n
    rps = min(ATT_TQ // tq, dil)
    assert tq % (2 * hb) == 0 and n % hb == 0 and tq % n == 0 and nb % bpt == 0 and dil % rps == 0
    pv = proj.reshape(bsz, nb, dil, n, ATT_IN_COLS)
    ncb = 3 * g

    grid = (bsz, dil // rps, nb // bpt)
    main = lambda cb: pl.BlockSpec((1, bpt, rps, n, w), lambda b, r, i: (b, i, r, 0, cb))
    prev_h = lambda cb: pl.BlockSpec((1, 1, rps, hb, w),
                                     lambda b, r, i: (b, jnp.maximum(i * bpt - 1, 0), r, n // hb - 1, cb))
    next_h = lambda cb: pl.BlockSpec((1, 1, rps, hb, w),
                                     lambda b, r, i: (b, jnp.minimum((i + 1) * bpt, nb - 1), r, 0, cb))
    out_block = lambda width: pl.BlockSpec((1, bpt, rps, n, width), lambda b, r, i: (b, i, r, 0, 0))

    in_specs = [main(ncb), prev_h(ncb + 1), main(ncb + 1), next_h(ncb + 1),
                prev_h(ncb + 2), main(ncb + 2), next_h(ncb + 2)]
    blocks = rps * (4 * _nbytes((tq, w), BF16) + 4 * _nbytes((hb, w), BF16) + _nbytes((tq, LANES), F32))
    scratch = [pltpu.VMEM((rps, tq, w), BF16), pltpu.VMEM((rps, tq + 2 * hb, w), BF16),
               pltpu.VMEM((rps, tq + 2 * hb, w), BF16), pltpu.VMEM((rps, tq, w), BF16),
               pltpu.VMEM((rps, tq, LANES), F32)]
    scratch_bytes = rps * (4 * _nbytes((tq + 2 * hb, w), BF16) + _nbytes((tq, LANES), F32))
    o, l = pl.pallas_call(
        functools.partial(_attn_kernel, tq=tq, ls=ls),
        grid=grid,
        in_specs=in_specs,
        out_specs=(out_block(w), out_block(LANES)),
        out_shape=(jax.ShapeDtypeStruct((bsz, nb, dil, n, w), BF16),
                   jax.ShapeDtypeStruct((bsz, nb, dil, n, LANES), F32)),
        scratch_shapes=scratch,
        compiler_params=pltpu.CompilerParams(
            dimension_semantics=("arbitrary", "arbitrary", "arbitrary"),
            vmem_limit_bytes=_vmem_limit(blocks, scratch_bytes, 8 * 1024 * 1024),
        ),
        name=f"attn_g{g}",
    )(*([pv] * 7))
    if dil > 1:
        o = o.transpose(0, 1, 3, 2, 4)
        l = l.transpose(0, 1, 3, 2, 4)
    return o.reshape(bsz * seq, w), l.reshape(bsz * seq, LANES)


def _post_norm_residual(out, x_ref, gpost_ref, o_ref, gnext_ref=None, hnext_ref=None):
    ms = jnp.mean(out * out, axis=-1, keepdims=True)
    x1 = x_ref[...] + out * lax.rsqrt(ms + NORM_EPS) * gpost_ref[...]
    o_ref[...] = x1
    if hnext_ref is not None:
        ms1 = jnp.mean(x1 * x1, axis=-1, keepdims=True)
        hnext_ref[...] = (x1 * lax.rsqrt(ms1 + NORM_EPS) * gnext_ref[...]).astype(hnext_ref.dtype)


OUTPROJ_KCHUNK = 512


def _sigmoid(x):
    return 0.5 * jnp.tanh(0.5 * x) + 0.5


def _silu(x):
    xh = 0.5 * x
    return xh * jnp.tanh(xh) + xh


def _outproj0_kernel(o0_ref, o1_ref, o2_ref, l0_ref, l1_ref, l2_ref, z_ref, w_ref, x_ref, gpost_ref, gnext_ref,
                     x1_ref, hnext_ref):
    o_refs = (o0_ref, o1_ref, o2_ref)
    lses = [r[...] for r in (l0_ref, l1_ref, l2_ref)]
    lmax = jnp.maximum(jnp.maximum(lses[0], lses[1]), lses[2])
    es = [jnp.exp2(l - lmax) for l in lses]
    tot = es[0] + es[1] + es[2]
    wts = [e / tot for e in es]
    heads_per_chunk = OUTPROJ_KCHUNK // ATT_HEAD_DIM
    out = None
    for c in range(ATT_HEADS // heads_per_chunk):
        ys = []
        for h in range(c * heads_per_chunk, (c + 1) * heads_per_chunk):
            hs = slice(h * ATT_HEAD_DIM, (h + 1) * ATT_HEAD_DIM)
            o = wts[0][:, h:h + 1] * o_refs[0][:, hs].astype(F32)
            for g in range(1, N_GROUPS):
                o = o + wts[g][:, h:h + 1] * o_refs[g][:, hs].astype(F32)
            ys.append((o * _silu(z_ref[:, hs].astype(F32))).astype(BF16))
        part = jnp.dot(jnp.concatenate(ys, axis=1), w_ref[c * OUTPROJ_KCHUNK:(c + 1) * OUTPROJ_KCHUNK, :],
                       preferred_element_type=F32)
        out = part if out is None else out + part
    _post_norm_residual(out, x_ref, gpost_ref, x1_ref, gnext_ref, hnext_ref)


def _outproj0(outs, lses, proj0, w, x2d, g_post, g_next, tm=256):
    t, d = x2d.shape
    k = w.shape[0]
    row = lambda i: (i, 0)
    vec = pl.BlockSpec((1, d), lambda i: (0, 0))
    zb = 3 * N_GROUPS
    in_specs = ([pl.BlockSpec((tm, k), row)] * N_GROUPS + [pl.BlockSpec((tm, LANES), row)] * N_GROUPS
                + [pl.BlockSpec((tm, k), lambda i: (i, zb)),
                   pl.BlockSpec((k, d), lambda i: (0, 0), pipeline_mode=pl.Buffered(1)),
                   pl.BlockSpec((tm, d), row), vec, vec])
    blocks = ((N_GROUPS + 1) * _nbytes((tm, k), BF16) + N_GROUPS * _nbytes((tm, LANES), F32)
              + 2 * _nbytes((tm, d), F32) + _nbytes((tm, d), BF16))
    return pl.pallas_call(
        _outproj0_kernel,
        grid=(t // tm,),
        in_specs=in_specs,
        out_specs=(pl.BlockSpec((tm, d), row), pl.BlockSpec((tm, d), row)),
        out_shape=(jax.ShapeDtypeStruct((t, d), F32), jax.ShapeDtypeStruct((t, d), BF16)),
        compiler_params=pltpu.CompilerParams(
            dimension_semantics=("arbitrary",),
            vmem_limit_bytes=_vmem_limit(blocks, _nbytes((k, d), BF16), 4 * _nbytes((tm, d), F32)),
        ),
        name="outproj0",
    )(*outs, *lses, proj0, w, x2d, g_post.reshape(1, d).astype(F32), g_next.reshape(1, d).astype(F32))


def _outproj1_kernel(hs_ref, og_ref, z_ref, hgain_ref, w_ref, x_ref, gpost_ref, o_ref):
    assert OUTPROJ_KCHUNK == M_V_DIM
    out = None
    for c in range(hs_ref.shape[1] // OUTPROJ_KCHUNK):
        ks = slice(c * OUTPROJ_KCHUNK, (c + 1) * OUTPROJ_KCHUNK)
        hs = hs_ref[:, ks].astype(F32)
        hn = hs * lax.rsqrt(jnp.mean(hs * hs, axis=-1, keepdims=True) + NORM_EPS) * hgain_ref[:, ks]
        y = (og_ref[:, ks].astype(F32) * hn) * z_ref[:, ks].astype(F32)
        part = jnp.dot(y.astype(BF16), w_ref[ks, :], preferred_element_type=F32)
        out = part if out is None else out + part
    _post_norm_residual(out, x_ref, gpost_ref, o_ref)


def _outproj1(hs, proj1, head_gain, w, x2d, g_post, tm=256):
    t, d = x2d.shape
    k = w.shape[0]
    row = lambda i: (i, 0)
    ob = M_OGATE_COL // k
    blocks = 3 * _nbytes((tm, k), BF16) + _nbytes((1, k), F32) + 2 * _nbytes((tm, d), F32)
    return pl.pallas_call(
        _outproj1_kernel,
        grid=(t // tm,),
        in_specs=[pl.BlockSpec((tm, k), row), pl.BlockSpec((tm, k), lambda i: (i, ob)),
                  pl.BlockSpec((tm, k), lambda i: (i, ob + 1)),
                  pl.BlockSpec((1, k), lambda i: (0, 0)),
                  pl.BlockSpec((k, d), lambda i: (0, 0), pipeline_mode=pl.Buffered(1)),
                  pl.BlockSpec((tm, d), row), pl.BlockSpec((1, d), lambda i: (0, 0))],
        out_specs=pl.BlockSpec((tm, d), row),
        out_shape=jax.ShapeDtypeStruct((t, d), F32),
        compiler_params=pltpu.CompilerParams(
            dimension_semantics=("arbitrary",),
            vmem_limit_bytes=_vmem_limit(blocks, _nbytes((k, d), BF16), 4 * _nbytes((tm, k), F32)),
        ),
        name="outproj1",
    )(hs, proj1, proj1, head_gain.reshape(1, k).astype(F32), w, x2d, g_post.reshape(1, d).astype(F32))


INPROJ1_CHUNK = 512


M_OGATE_COL = 2 * M_QK_WIDTH + M_V_WIDTH
M_ZGATE_COL = M_OGATE_COL + M_V_WIDTH


def _inproj1_kernel(h_ref, w_ref, wg_ref, o_ref, g_ref):
    tn = o_ref.shape[1]
    col0 = pl.program_id(1) * tn

    def tile(act):
        for cc in range(tn // INPROJ1_CHUNK):
            cs = slice(cc * INPROJ1_CHUNK, (cc + 1) * INPROJ1_CHUNK)
            acc = jnp.dot(h_ref[...], w_ref[:, cs], preferred_element_type=F32)
            o_ref[:, cs] = (acc if act is None else act(acc)).astype(o_ref.dtype)

    pl.when(col0 < M_OGATE_COL)(lambda: tile(None))
    pl.when(jnp.logical_and(col0 >= M_OGATE_COL, col0 < M_ZGATE_COL))(lambda: tile(_sigmoid))
    pl.when(col0 >= M_ZGATE_COL)(lambda: tile(_silu))

    @pl.when(pl.program_id(1) == 0)
    def _():
        g_ref[...] = jnp.dot(h_ref[...], wg_ref[...], preferred_element_type=F32)


def _inproj1(h, w, wg, tm=1024, tn=2048):
    t, k = h.shape
    n = M_MAIN_COLS
    tm = min(tm, t)
    blocks = (_nbytes((tm, k), BF16) + _nbytes((k, tn), BF16) + _nbytes((k, GATE_LANES), BF16)
              + _nbytes((tm, tn), BF16) + _nbytes((tm, GATE_LANES), F32))
    return pl.pallas_call(
        _inproj1_kernel,
        grid=(t // tm, n // tn),
        in_specs=[
            pl.BlockSpec((tm, k), lambda i, j: (i, 0)),
            pl.BlockSpec((k, tn), lambda i, j: (0, j)),
            pl.BlockSpec((k, GATE_LANES), lambda i, j: (0, 0)),
        ],
        out_specs=(pl.BlockSpec((tm, tn), lambda i, j: (i, j)), pl.BlockSpec((tm, GATE_LANES), lambda i, j: (i, 0))),
        out_shape=(jax.ShapeDtypeStruct((t, n), BF16), jax.ShapeDtypeStruct((t, GATE_LANES), F32)),
        compiler_params=pltpu.CompilerParams(
            dimension_semantics=("arbitrary", "arbitrary"),
            vmem_limit_bytes=_vmem_limit(blocks, 0, 4 * _nbytes((tm, INPROJ1_CHUNK), F32)),
        ),
        name="inproj1",
    )(h, w, wg)


def _scan_rows(x, rows, op, fill, reverse):
    n = x.shape[0]
    k = 1
    while k < n:
        if reverse:
            shifted = jnp.where(rows < n - k, pltpu.roll(x, n - k, 0), fill)
        else:
            shifted = jnp.where(rows >= k, pltpu.roll(x, k, 0), fill)
        x = op(x, shifted)
        k *= 2
    return x


GATE_PREP_CHUNKS = 4


def _gate_prep_kernel(g_ref, b_ref, o_ref):
    for ci in range(GATE_PREP_CHUNKS):
        _gate_prep_chunk(g_ref[0, ci * M_CHUNK:(ci + 1) * M_CHUNK, :] + b_ref[...], o_ref, ci)


def _gate_prep_chunk(g, o_ref, ci):
    rows = lax.broadcasted_iota(jnp.int32, g.shape, 0)
    kind = lax.broadcasted_iota(jnp.int32, g.shape, 1) % 8
    lf = jnp.minimum(g, 0.0) - jnp.log1p(jnp.exp(-jnp.abs(g)))
    csum = _scan_rows(lf, rows, jnp.add, 0.0, reverse=False)
    rsum = _scan_rows(lf, rows, jnp.add, 0.0, reverse=True)
    b_at_i = pltpu.roll(jnp.where(kind == 1, csum, rsum), LANES - 1, 1)
    grow = g - b_at_i
    cm_f = _scan_rows(grow, rows, jnp.maximum, NEG_INF, reverse=False)
    cm_b = _scan_rows(grow, rows, jnp.maximum, NEG_INF, reverse=True)
    y = jnp.where(kind == 0, b_at_i, 0.0)
    y = jnp.where(kind == 1, pltpu.roll(grow, 1, 1), y)
    y = jnp.where(kind == 2, pltpu.roll(cm_f, 2, 1), y)
    y = jnp.where(kind == 3, rsum, y)
    y = jnp.where(kind == 4, pltpu.roll(grow, 2, 1), y)
    y = jnp.where(kind == 5, pltpu.roll(cm_b, 3, 1), y)
    yt = y.T
    o_ref[0, :, ci] = yt[:M_HEADS * 8].reshape(M_HEADS, 8, g.shape[0])


def _gate_prep(gates, bias, bsz, seq):
    nc = seq // M_CHUNK
    per = GATE_PREP_CHUNKS
    return pl.pallas_call(
        _gate_prep_kernel,
        grid=(bsz, nc // per),
        in_specs=[pl.BlockSpec((1, per * M_CHUNK, GATE_LANES), lambda b, c: (b, c, 0)),
                  pl.BlockSpec((1, GATE_LANES), lambda b, c: (0, 0))],
        out_specs=pl.BlockSpec((1, M_HEADS, per, 8, M_CHUNK), lambda b, c: (b, 0, c, 0, 0)),
        out_shape=jax.ShapeDtypeStruct((bsz, M_HEADS, nc, 8, M_CHUNK), F32),
        compiler_params=pltpu.CompilerParams(dimension_semantics=("arbitrary", "arbitrary")),
        name="gate_prep",
    )(gates.reshape(bsz, seq, GATE_LANES), bias)


def _mlstm_kernel(q_ref, k_ref, v_ref, r_ref, o_ref, hs_scr, cf_scr, cb_scr, *, nc):
    L = M_CHUNK
    rows = lax.broadcasted_iota(jnp.int32, (L, L), 0)
    cols = lax.broadcasted_iota(jnp.int32, (L, L), 1)
    eye = rows == cols
    causal = cols <= rows
    anti = cols >= rows

    def to_col(r):
        return jnp.sum(jnp.where(eye, r, 0.0), axis=1, keepdims=True)

    def chunk(c, c_scr, n, m, fwd):
        st = pl.multiple_of(c * L, L)
        qc = q_ref[0, pl.ds(st, L), :] * (M_QK_DIM ** -0.5)
        kc = k_ref[0, pl.ds(st, L), :]
        vc = v_ref[0, pl.ds(st, L), :]
        r = r_ref[0, 0, pl.ds(c, 1)].reshape(8, L)
        base = 0 if fwd else 3
        grow = r[base + 1:base + 2]
        bcol = to_col(r[base:base + 1])
        gcol = to_col(grow)
        mm = jnp.maximum(to_col(r[base + 2:base + 3]), m)
        e = L - 1 if fwd else 0
        mm_l = mm[e:e + 1]
        s = lax.dot_general(qc, kc, (((1,), (1,)), ((), ())), preferred_element_type=F32)
        a = jnp.exp(jnp.where(causal if fwd else anti, grow - mm, NEG_INF)) * s
        gint = jnp.exp(m - mm)
        cmat = c_scr[...]
        num = (jnp.dot(a.astype(BF16), vc, preferred_element_type=F32)
               + gint * jnp.dot(qc, cmat.astype(BF16), preferred_element_type=F32))
        qn = jnp.sum(qc.astype(F32) * n, axis=1, keepdims=True)
        den = jnp.sum(a, axis=1, keepdims=True) + gint * qn
        hc = num * (1.0 / jnp.maximum(jnp.abs(den), jnp.exp(-(bcol + mm))))
        decay = jnp.exp(m - mm_l)
        wk = jnp.exp(gcol - mm_l) * kc.astype(F32)
        c_scr[...] = decay * cmat + lax.dot_general(wk.astype(BF16), vc, (((0,), (0,)), ((), ())),
                                                    preferred_element_type=F32)
        n_new = decay * n + jnp.sum(wk, axis=0, keepdims=True)
        m_new = bcol[e:e + 1] + mm_l
        return hc, n_new, m_new

    def finish(c, hsum):
        st = pl.multiple_of(c * L, L)
        o_ref[0, pl.ds(st, L), :] = hsum.astype(o_ref.dtype)

    def step(j, carry, second_half):
        nf, mf, nb, mb = carry
        cf, cb = j, nc - 1 - j
        hf, nf, mf = chunk(cf, cf_scr, nf, mf, True)
        hb, nb, mb = chunk(cb, cb_scr, nb, mb, False)
        sf = pl.multiple_of(cf * L, L)
        sb = pl.multiple_of(cb * L, L)
        if second_half:
            finish(cf, hs_scr[pl.ds(sf, L), :] + hf)
            finish(cb, hs_scr[pl.ds(sb, L), :] + hb)
        else:
            hs_scr[pl.ds(sf, L), :] = hf
            hs_scr[pl.ds(sb, L), :] = hb
        return nf, mf, nb, mb

    cf_scr[...] = jnp.zeros_like(cf_scr)
    cb_scr[...] = jnp.zeros_like(cb_scr)
    n0 = jnp.zeros((1, M_QK_DIM), F32)
    m0 = jnp.zeros((1, 1), F32)
    carry = lax.fori_loop(0, nc // 2, functools.partial(step, second_half=False), (n0, m0, n0, m0), unroll=4)
    lax.fori_loop(nc // 2, nc, functools.partial(step, second_half=True), carry, unroll=4)


def _mlstm(proj1, prep, bsz, seq):
    nc = seq // M_CHUNK
    assert nc % 2 == 0
    p3 = proj1.reshape(bsz, seq, M_MAIN_COLS)
    kb = M_QK_WIDTH // M_QK_DIM
    vb = 2 * M_QK_WIDTH // M_V_DIM
    blocks = (2 * _nbytes((seq, M_QK_DIM), BF16) + 2 * _nbytes((seq, M_V_DIM), BF16)
              + _nbytes((nc, 8, M_CHUNK), F32))
    scratch_bytes = _nbytes((seq, M_V_DIM), F32) + 2 * _nbytes((M_QK_DIM, M_V_DIM), F32)
    return pl.pallas_call(
        functools.partial(_mlstm_kernel, nc=nc),
        grid=(bsz, M_HEADS),
        in_specs=[
            pl.BlockSpec((1, seq, M_QK_DIM), lambda b, h: (b, 0, h)),
            pl.BlockSpec((1, seq, M_QK_DIM), lambda b, h: (b, 0, kb + h)),
            pl.BlockSpec((1, seq, M_V_DIM), lambda b, h: (b, 0, vb + h)),
            pl.BlockSpec((1, 1, nc, 8, M_CHUNK), lambda b, h: (b, h, 0, 0, 0)),
        ],
        out_specs=pl.BlockSpec((1, seq, M_V_DIM), lambda b, h: (b, 0, h)),
        out_shape=jax.ShapeDtypeStruct((bsz, seq, M_V_WIDTH), BF16),
        scratch_shapes=[pltpu.VMEM((seq, M_V_DIM), F32), pltpu.VMEM((M_QK_DIM, M_V_DIM), F32),
                        pltpu.VMEM((M_QK_DIM, M_V_DIM), F32)],
        compiler_params=pltpu.CompilerParams(
            dimension_semantics=("arbitrary", "arbitrary"),
            vmem_limit_bytes=_vmem_limit(blocks, scratch_bytes, 8 * 1024 * 1024),
        ),
        name="mlstm",
    )(p3, p3, p3, prep)


def _gate_weights(w_in1, b_gate):
    wg = w_in1[:, M_MAIN_COLS:].reshape(D_MODEL, 4, M_HEADS).transpose(0, 2, 1)
    wg = jnp.pad(wg, ((0, 0), (0, 0), (0, 4))).reshape(D_MODEL, M_HEADS * 8)
    wg = jnp.pad(wg, ((0, 0), (0, GATE_LANES - M_HEADS * 8)))
    bg = jnp.pad(b_gate.astype(F32).reshape(4, M_HEADS).T, ((0, 0), (0, 4))).reshape(1, M_HEADS * 8)
    bg = jnp.pad(bg, ((0, 0), (0, GATE_LANES - M_HEADS * 8)))
    return wg.astype(BF16), bg


def _trunk(x, p):
    bsz, seq, d = x.shape
    t = bsz * seq
    assert seq % (2 * PERM_BLOCK) == 0
    x2d = x.reshape(t, d)
    h0 = _rmsnorm_orders(x2d, p["l0_norm_pre"])
    proj0 = _inproj0(h0, p["l0_w_in"], p["rope"], seq)
    outs, lses = zip(*[_attention_group(proj0, g, bsz, seq) for g in range(N_GROUPS)])
    x1, h1 = _outproj0(outs, lses, proj0, p["l0_w_out"], x2d, p["l0_norm_post"], p["l1_norm_pre"])
    proj1, gates = _inproj1(h1, p["l1_w_main"], p["l1_w_gate"])
    prep = _gate_prep(gates, p["l1_b_gate"], bsz, seq)
    hs = _mlstm(proj1, prep, bsz, seq)
    y = _outproj1(hs.reshape(t, M_V_WIDTH), proj1, p["l1_head_norm"], p["l1_w_out"], x1, p["l1_norm_post"])
    return y.reshape(bsz, seq, d)


def kernel(x_prompt, x_sample, l0_norm_pre, l0_w_in, l0_w_out, l0_norm_post,
           l1_norm_pre, l1_w_in, l1_b_gate, l1_head_norm, l1_w_out, l1_norm_post):
    assert x_prompt.shape[1] == x_sample.shape[1]
    w_gate, b_gate = _gate_weights(l1_w_in, l1_b_gate)
    p = {
        "l0_norm_pre": l0_norm_pre, "l0_w_in": l0_w_in.astype(BF16), "l0_w_out": l0_w_out.astype(BF16),
        "l0_norm_post": l0_norm_post, "l1_norm_pre": l1_norm_pre,
        "l1_w_main": l1_w_in.astype(BF16), "l1_w_gate": w_gate, "l1_b_gate": b_gate,
        "l1_head_norm": l1_head_norm, "l1_w_out": l1_w_out.astype(BF16), "l1_norm_post": l1_norm_post,
        "rope": _rope_tables(x_prompt.shape[1]),
    }
    return (_trunk(x_prompt, p), _trunk(x_sample, p))
```

```python
import functools

import jax
import jax.numpy as jnp
from jax import lax
from jax.experimental import pallas as pl
from jax.experimental.pallas import tpu as pltpu

F32 = jnp.float32
BF16 = jnp.bfloat16

D_MODEL = 2048
ATT_GROUPS = ((128, 1), (512, 4), (2048, 16))
N_GROUPS = len(ATT_GROUPS)
ATT_HEADS = 16
ATT_HEAD_DIM = 128
ATT_WIDTH = ATT_HEADS * ATT_HEAD_DIM
ATT_IN_COLS = 3 * N_GROUPS * ATT_WIDTH + ATT_WIDTH
ATT_HALF = 64
ROPE_THETA = 500000.0
ROPE_DIMS = ATT_HEAD_DIM // 4
ROPE_HALF = ROPE_DIMS // 2
LOG2E = 1.4426950408889634
ATT_Q_SCALE = ATT_HEAD_DIM ** -0.5 * LOG2E

M_HEADS = 8
M_QK_DIM = 256
M_V_DIM = 512
M_QK_WIDTH = M_HEADS * M_QK_DIM
M_V_WIDTH = M_HEADS * M_V_DIM
M_MAIN_COLS = 2 * M_QK_WIDTH + 3 * M_V_WIDTH
M_CHUNK = 256
GATE_LANES = 128

NORM_EPS = 1e-6
NEG_INF = -1e30

LANES = 128
VMEM_CAP_BYTES = 56 * 1024 * 1024


def _vmem_limit(block_bytes, scratch_bytes=0, temp_bytes=0):
    need = 2 * block_bytes + scratch_bytes + temp_bytes
    return int(min(max(need, 16 * 1024 * 1024), VMEM_CAP_BYTES))


def _nbytes(shape, dtype):
    n = 1
    for s in shape:
        n *= s
    return n * jnp.dtype(dtype).itemsize


PERM_BLOCK = 1024


RMS_PIECE = 128


def _rmsnorm_orders_kernel(x_ref, g_ref, o_ref, slab_scr):
    nslab = x_ref.shape[1] // LANES
    pr = RMS_PIECE

    def piece(pi, _):
        r0 = pl.multiple_of(pi * pr, pr)
        x = x_ref[pl.ds(r0, pr), :]
        inv = lax.rsqrt(jnp.mean(x * x, axis=-1, keepdims=True) + NORM_EPS)
        for c in range(nslab):
            ls = slice(c * LANES, (c + 1) * LANES)
            y = x[:, ls] * inv * g_ref[:, ls]
            slab_scr[c] = y
            for g, (_, dil) in enumerate(ATT_GROUPS):
                if dil == 1:
                    o_ref[g, pl.ds(r0, pr), ls] = y.astype(o_ref.dtype)
        for g, (_, dil) in enumerate(ATT_GROUPS):
            if dil == 1:
                continue
            n = PERM_BLOCK // dil
            npc = pr // dil
            for r in range(dil):
                dst = pl.multiple_of(r * n + pi * npc, npc)
                for c in range(nslab):
                    o_ref[g, pl.ds(dst, npc), c * LANES:(c + 1) * LANES] = (
                        slab_scr[c, pl.ds(r, npc, stride=dil), :].astype(o_ref.dtype))
        return 0

    lax.fori_loop(0, x_ref.shape[0] // pr, piece, 0)


def _rmsnorm_orders(x2d, gain):
    t, d = x2d.shape
    tm = PERM_BLOCK
    return pl.pallas_call(
        _rmsnorm_orders_kernel,
        grid=(t // tm,),
        in_specs=[pl.BlockSpec((tm, d), lambda i: (i, 0)), pl.BlockSpec((1, d), lambda i: (0, 0))],
        out_specs=pl.BlockSpec((N_GROUPS, tm, d), lambda i: (0, i, 0)),
        out_shape=jax.ShapeDtypeStruct((N_GROUPS, t, d), BF16),
        scratch_shapes=[pltpu.VMEM((d // LANES, RMS_PIECE, LANES), F32)],
        compiler_params=pltpu.CompilerParams(
            dimension_semantics=("arbitrary",),
            vmem_limit_bytes=_vmem_limit(_nbytes((tm, d), F32) + _nbytes((N_GROUPS, tm, d), BF16),
                                         _nbytes((RMS_PIECE, d), F32), 2 * _nbytes((RMS_PIECE, d), F32)),
        ),
        name="rmsnorm_orders",
    )(x2d, gain.reshape(1, d).astype(F32))


def _order_index(seq, dil):
    n = PERM_BLOCK // dil
    return jnp.arange(seq, dtype=jnp.int32).reshape(seq // PERM_BLOCK, n, dil).transpose(0, 2, 1).reshape(seq)


INPROJ0_TN = 2048
INPROJ0_CHUNK = 256
INPROJ0_TAIL_ROWS = ((0, PERM_BLOCK // 2), (PERM_BLOCK // 2, 3 * PERM_BLOCK // 4), (3 * PERM_BLOCK // 4, PERM_BLOCK))
_GROUP_TILES = 3 * ATT_WIDTH // INPROJ0_TN
_Z_TILES = ATT_WIDTH // INPROJ0_TN


def _inproj0_col_tile(jj):
    return jnp.where(jj < _GROUP_TILES, jj,
                     jnp.where(jj < _GROUP_TILES + _Z_TILES, jj + (N_GROUPS - 1) * _GROUP_TILES, jj - _Z_TILES))


def _inproj0_order(jj):
    return jnp.where(jj < _GROUP_TILES + _Z_TILES, 0, (jj - _Z_TILES) // _GROUP_TILES)


def _inproj0_kernel(h_ref, w_ref, c_ref, s1_ref, s2_ref, o_ref):
    col = _inproj0_col_tile(pl.program_id(1))
    seg = (col * INPROJ0_TN) // ATT_WIDTH
    kind = seg % 3
    is_rope = jnp.logical_and(seg < 3 * N_GROUPS, kind < 2)
    nchunk = INPROJ0_TN // INPROJ0_CHUNK
    tm = o_ref.shape[0]

    def pieces():
        for cc in range(nchunk):
            cols = slice(cc * INPROJ0_CHUNK, (cc + 1) * INPROJ0_CHUNK)
            if cc < nchunk - 1:
                yield slice(0, tm), cols
            else:
                for lo, hi in INPROJ0_TAIL_ROWS:
                    yield slice(lo, hi), cols

    @pl.when(is_rope)
    def _():
        scale = jnp.where(kind == 0, ATT_Q_SCALE, 1.0).astype(F32)
        c_all = c_ref[...] * scale
        s1_all = s1_ref[...] * scale
        s2_all = s2_ref[...] * scale
        for rows, cols in pieces():
            c, s1, s2 = c_all[rows], s1_all[rows], s2_all[rows]
            acc = jnp.dot(h_ref[rows, :], w_ref[:, cols], preferred_element_type=F32)
            for t in range(INPROJ0_CHUNK // LANES):
                a = acc[:, t * LANES:(t + 1) * LANES]
                r = a * c + pltpu.roll(a, LANES - ROPE_HALF, 1) * s1 + pltpu.roll(a, ROPE_HALF, 1) * s2
                lo = cols.start + t * LANES
                o_ref[rows, lo:lo + LANES] = r.astype(o_ref.dtype)

    @pl.when(jnp.logical_not(is_rope))
    def _():
        for rows, cols in pieces():
            o_ref[rows, cols] = jnp.dot(h_ref[rows, :], w_ref[:, cols], preferred_element_type=F32).astype(o_ref.dtype)


def _rope_tables(seq):
    inv = jnp.power(ROPE_THETA, -jnp.arange(ROPE_HALF, dtype=F32) / ROPE_HALF)
    pos = jnp.stack([_order_index(seq, dil) for _, dil in ATT_GROUPS]).astype(F32)
    ang = pos[:, :, None] * inv[None, None, :]
    cos, sin = jnp.cos(ang), jnp.sin(ang)
    zeros = jnp.zeros((N_GROUPS, seq, LANES - ROPE_DIMS), F32)
    zh = jnp.zeros((N_GROUPS, seq, ROPE_HALF), F32)
    c = jnp.concatenate([cos, cos, jnp.ones((N_GROUPS, seq, LANES - ROPE_DIMS), F32)], axis=2)
    s1 = jnp.concatenate([-sin, zh, zeros], axis=2)
    s2 = jnp.concatenate([zh, sin, zeros], axis=2)
    return c, s1, s2


def _inproj0(h_orders, w, tables, seq):
    _, t, k = h_orders.shape
    n = w.shape[1]
    tm, tn = PERM_BLOCK, INPROJ0_TN
    pos_blocks = seq // tm
    tab_spec = pl.BlockSpec((None, tm, LANES), lambda i, j: (_inproj0_order(j), i % pos_blocks, 0))
    blocks = _nbytes((tm, k), BF16) + _nbytes((k, tn), BF16) + _nbytes((tm, tn), BF16) + 3 * _nbytes((tm, LANES), F32)
    return pl.pallas_call(
        _inproj0_kernel,
        grid=(t // tm, n // tn),
        in_specs=[
            pl.BlockSpec((None, tm, k), lambda i, j: (_inproj0_order(j), i, 0)),
            pl.BlockSpec((k, tn), lambda i, j: (0, _inproj0_col_tile(j))),
            tab_spec, tab_spec, tab_spec,
        ],
        out_specs=pl.BlockSpec((tm, tn), lambda i, j: (i, _inproj0_col_tile(j))),
        out_shape=jax.ShapeDtypeStruct((t, n), BF16),
        compiler_params=pltpu.CompilerParams(
            dimension_semantics=("arbitrary", "arbitrary"),
            vmem_limit_bytes=_vmem_limit(blocks, 0, _nbytes((tm, k), BF16) + 3 * _nbytes((tm, LANES), F32)
                                         + 4 * _nbytes((tm, INPROJ0_CHUNK), F32)),
        ),
        name="inproj0",
    )(h_orders, w, *tables)


ATT_TQ = 512


def _attn_kernel(q_ref, kp_ref, kc_ref, kn_ref, vp_ref, vc_ref, vn_ref, o_ref, l_ref, qx, kx, vx, ox, lx, *, tq, ls):
    hb = ATT_HALF
    sub = 2 * hb
    w = qx.shape[2]
    rps = qx.shape[0]
    i = pl.program_id(2)
    bpt, n = q_ref.shape[1], q_ref.shape[3]

    row = lax.broadcasted_iota(jnp.int32, (sub, 2 * sub), 0)
    col = lax.broadcasted_iota(jnp.int32, (sub, 2 * sub), 1)
    band = jnp.abs(col - hb - row) <= hb
    lane = lax.broadcasted_iota(jnp.int32, (sub, LANES), 1)

    for rr in range(rps):
        qx[rr] = q_ref[0, :, rr].reshape(tq, w)
        kx[rr, 0:hb, :] = kp_ref[0, 0, rr]
        kx[rr, hb:hb + tq, :] = kc_ref[0, :, rr].reshape(tq, w)
        kx[rr, hb + tq:, :] = kn_ref[0, 0, rr]
        vx[rr, 0:hb, :] = vp_ref[0, 0, rr]
        vx[rr, hb:hb + tq, :] = vc_ref[0, :, rr].reshape(tq, w)
        vx[rr, hb + tq:, :] = vn_ref[0, 0, rr]

    for rr in range(rps):
        for a in range(tq // sub):
            r0 = a * sub
            kidx = i * tq + r0 + col - hb
            valid = band & (kidx >= 0) & (kidx < ls)
            lse_tile = jnp.zeros((sub, LANES), F32)
            for h in range(ATT_HEADS):
                hs = slice(h * ATT_HEAD_DIM, (h + 1) * ATT_HEAD_DIM)
                qh = qx[rr, r0:r0 + sub, hs]
                kh = kx[rr, r0:r0 + 2 * sub, hs]
                vh = vx[rr, r0:r0 + 2 * sub, hs]
                s = lax.dot_general(qh, kh, (((1,), (1,)), ((), ())), preferred_element_type=F32)
                s = jnp.where(valid, s, NEG_INF)
                m = jnp.max(s, axis=-1, keepdims=True)
                p = jnp.exp2(s - m)
                den = jnp.sum(p, axis=-1, keepdims=True)
                o = jnp.dot(p.astype(BF16), vh, preferred_element_type=F32) * (1.0 / den)
                lse_tile = jnp.where(lane == h, m + jnp.log2(den), lse_tile)
                ox[rr, r0:r0 + sub, hs] = o.astype(ox.dtype)
            lx[rr, r0:r0 + sub, :] = lse_tile
        o_ref[0, :, rr] = ox[rr].reshape(bpt, n, w)
        l_ref[0, :, rr] = lx[rr].reshape(bpt, n, LANES)


def _attention_group(proj, g, bsz, seq):
    _, dil = ATT_GROUPS[g]
    w = ATT_WIDTH
    hb = ATT_HALF
    ls = seq // dil
    if dil == 1:
        n = ATT_TQ // 2
        nb = seq // n
    else:
        nb, n = seq // PERM_BLOCK, PERM_BLOCK // dil
    tq = min(ATT_TQ, ls)
    bpt = tq // n
    rps = min(ATT_TQ // tq, dil)
    assert tq % (2 * hb) == 0 and n % hb == 0 and tq % n == 0 and nb % bpt == 0 and dil % rps == 0
    pv = proj.reshape(bsz, nb, dil, n, ATT_IN_COLS)
    ncb = 3 * g

    grid = (bsz, dil // rps, nb // bpt)
    main = lambda cb: pl.BlockSpec((1, bpt, rps, n, w), lambda b, r, i: (b, i, r, 0, cb))
    prev_h = lambda cb: pl.BlockSpec((1, 1, rps, hb, w),
                                     lambda b, r, i: (b, jnp.maximum(i * bpt - 1, 0), r, n // hb - 1, cb))
    next_h = lambda cb: pl.BlockSpec((1, 1, rps, hb, w),
                                     lambda b, r, i: (b, jnp.minimum((i + 1) * bpt, nb - 1), r, 0, cb))
    out_block = lambda width: pl.BlockSpec((1, bpt, rps, n, width), lambda b, r, i: (b, i, r, 0, 0))

    in_specs = [main(ncb), prev_h(ncb + 1), main(ncb + 1), next_h(ncb + 1),
                prev_h(ncb + 2), main(ncb + 2), next_h(ncb + 2)]
    blocks = rps * (4 * _nbytes((tq, w), BF16) + 4 * _nbytes((hb, w), BF16) + _nbytes((tq, LANES), F32))
    scratch = [pltpu.VMEM((rps, tq, w), BF16), pltpu.VMEM((rps, tq + 2 * hb, w), BF16),
               pltpu.VMEM((rps, tq + 2 * hb, w), BF16), pltpu.VMEM((rps, tq, w), BF16),
               pltpu.VMEM((rps, tq, LANES), F32)]
    scratch_bytes = rps * (4 * _nbytes((tq + 2 * hb, w), BF16) + _nbytes((tq, LANES), F32))
    o, l = pl.pallas_call(
        functools.partial(_attn_kernel, tq=tq, ls=ls),
        grid=grid,
        in_specs=in_specs,
        out_specs=(out_block(w), out_block(LANES)),
        out_shape=(jax.ShapeDtypeStruct((bsz, nb, dil, n, w), BF16),
                   jax.ShapeDtypeStruct((bsz, nb, dil, n, LANES), F32)),
        scratch_shapes=scratch,
        compiler_params=pltpu.CompilerParams(
            dimension_semantics=("arbitrary", "arbitrary", "arbitrary"),
            vmem_limit_bytes=_vmem_limit(blocks, scratch_bytes, 8 * 1024 * 1024),
        ),
        name=f"attn_g{g}",
    )(*([pv] * 7))
    if dil > 1:
        o = o.transpose(0, 1, 3, 2, 4)
        l = l.transpose(0, 1, 3, 2, 4)
    return o.reshape(bsz * seq, w), l.reshape(bsz * seq, LANES)


def _post_norm_residual(out, x_ref, gpost_ref, o_ref, gnext_ref=None, hnext_ref=None):
    ms = jnp.mean(out * out, axis=-1, keepdims=True)
    x1 = x_ref[...] + out * lax.rsqrt(ms + NORM_EPS) * gpost_ref[...]
    o_ref[...] = x1
    if hnext_ref is not None:
        ms1 = jnp.mean(x1 * x1, axis=-1, keepdims=True)
        hnext_ref[...] = (x1 * lax.rsqrt(ms1 + NORM_EPS) * gnext_ref[...]).astype(hnext_ref.dtype)


OUTPROJ_KCHUNK = 512


def _sigmoid(x):
    return 0.5 * jnp.tanh(0.5 * x) + 0.5


def _silu(x):
    xh = 0.5 * x
    return xh * jnp.tanh(xh) + xh


def _outproj0_kernel(o0_ref, o1_ref, o2_ref, l0_ref, l1_ref, l2_ref, z_ref, w_ref, x_ref, gpost_ref, gnext_ref,
                     x1_ref, hnext_ref):
    o_refs = (o0_ref, o1_ref, o2_ref)
    lses = [r[...] for r in (l0_ref, l1_ref, l2_ref)]
    lmax = jnp.maximum(jnp.maximum(lses[0], lses[1]), lses[2])
    es = [jnp.exp2(l - lmax) for l in lses]
    tot = es[0] + es[1] + es[2]
    wts = [e / tot for e in es]
    heads_per_chunk = OUTPROJ_KCHUNK // ATT_HEAD_DIM
    half = heads_per_chunk // 2
    bounds = list(range(0, 2 * heads_per_chunk, half)) + list(range(2 * heads_per_chunk, ATT_HEADS + 1, heads_per_chunk))
    out = None
    for h0, h1 in zip(bounds[:-1], bounds[1:]):
        ys = []
        for h in range(h0, h1):
            hs = slice(h * ATT_HEAD_DIM, (h + 1) * ATT_HEAD_DIM)
            o = wts[0][:, h:h + 1] * o_refs[0][:, hs].astype(F32)
            for g in range(1, N_GROUPS):
                o = o + wts[g][:, h:h + 1] * o_refs[g][:, hs].astype(F32)
            ys.append((o * _silu(z_ref[:, hs].astype(F32))).astype(BF16))
        part = jnp.dot(jnp.concatenate(ys, axis=1), w_ref[h0 * ATT_HEAD_DIM:h1 * ATT_HEAD_DIM, :],
                       preferred_element_type=F32)
        out = part if out is None else out + part
    _post_norm_residual(out, x_ref, gpost_ref, x1_ref, gnext_ref, hnext_ref)


def _outproj0(outs, lses, proj0, w, x2d, g_post, g_next, tm=256):
    t, d = x2d.shape
    k = w.shape[0]
    row = lambda i: (i, 0)
    vec = pl.BlockSpec((1, d), lambda i: (0, 0))
    zb = 3 * N_GROUPS
    in_specs = ([pl.BlockSpec((tm, k), row)] * N_GROUPS + [pl.BlockSpec((tm, LANES), row)] * N_GROUPS
                + [pl.BlockSpec((tm, k), lambda i: (i, zb)),
                   pl.BlockSpec((k, d), lambda i: (0, 0), pipeline_mode=pl.Buffered(1)),
                   pl.BlockSpec((tm, d), row), vec, vec])
    blocks = ((N_GROUPS + 1) * _nbytes((tm, k), BF16) + N_GROUPS * _nbytes((tm, LANES), F32)
              + 2 * _nbytes((tm, d), F32) + _nbytes((tm, d), BF16))
    return pl.pallas_call(
        _outproj0_kernel,
        grid=(t // tm,),
        in_specs=in_specs,
        out_specs=(pl.BlockSpec((tm, d), row), pl.BlockSpec((tm, d), row)),
        out_shape=(jax.ShapeDtypeStruct((t, d), F32), jax.ShapeDtypeStruct((t, d), BF16)),
        compiler_params=pltpu.CompilerParams(
            dimension_semantics=("arbitrary",),
            vmem_limit_bytes=_vmem_limit(blocks, _nbytes((k, d), BF16), 4 * _nbytes((tm, d), F32)),
        ),
        name="outproj0",
    )(*outs, *lses, proj0, w, x2d, g_post.reshape(1, d).astype(F32), g_next.reshape(1, d).astype(F32))


def _outproj1_kernel(hs_ref, og_ref, z_ref, hgain_ref, w_ref, x_ref, gpost_ref, o_ref):
    assert OUTPROJ_KCHUNK == M_V_DIM
    out = None
    for c in range(hs_ref.shape[1] // OUTPROJ_KCHUNK):
        ks = slice(c * OUTPROJ_KCHUNK, (c + 1) * OUTPROJ_KCHUNK)
        hs = hs_ref[:, ks].astype(F32)
        hn = hs * lax.rsqrt(jnp.mean(hs * hs, axis=-1, keepdims=True) + NORM_EPS) * hgain_ref[:, ks]
        y = (og_ref[:, ks].astype(F32) * hn) * z_ref[:, ks].astype(F32)
        part = jnp.dot(y.astype(BF16), w_ref[ks, :], preferred_element_type=F32)
        out = part if out is None else out + part
    _post_norm_residual(out, x_ref, gpost_ref, o_ref)


def _outproj1(hs, proj1, head_gain, w, x2d, g_post, tm=256):
    t, d = x2d.shape
    k = w.shape[0]
    row = lambda i: (i, 0)
    ob = M_OGATE_COL // k
    blocks = 3 * _nbytes((tm, k), BF16) + _nbytes((1, k), F32) + 2 * _nbytes((tm, d), F32)
    return pl.pallas_call(
        _outproj1_kernel,
        grid=(t // tm,),
        in_specs=[pl.BlockSpec((tm, k), row), pl.BlockSpec((tm, k), lambda i: (i, ob)),
                  pl.BlockSpec((tm, k), lambda i: (i, ob + 1)),
                  pl.BlockSpec((1, k), lambda i: (0, 0)),
                  pl.BlockSpec((k, d), lambda i: (0, 0), pipeline_mode=pl.Buffered(1)),
                  pl.BlockSpec((tm, d), row), pl.BlockSpec((1, d), lambda i: (0, 0))],
        out_specs=pl.BlockSpec((tm, d), row),
        out_shape=jax.ShapeDtypeStruct((t, d), F32),
        compiler_params=pltpu.CompilerParams(
            dimension_semantics=("arbitrary",),
            vmem_limit_bytes=_vmem_limit(blocks, _nbytes((k, d), BF16), 4 * _nbytes((tm, k), F32)),
        ),
        name="outproj1",
    )(hs, proj1, proj1, head_gain.reshape(1, k).astype(F32), w, x2d, g_post.reshape(1, d).astype(F32))


INPROJ1_CHUNK = 512


M_OGATE_COL = 2 * M_QK_WIDTH + M_V_WIDTH
M_ZGATE_COL = M_OGATE_COL + M_V_WIDTH


def _inproj1_kernel(h_ref, w_ref, wg_ref, o_ref, g_ref):
    tn = o_ref.shape[1]
    col0 = pl.program_id(1) * tn

    def tile(act):
        for cc in range(tn // INPROJ1_CHUNK):
            cs = slice(cc * INPROJ1_CHUNK, (cc + 1) * INPROJ1_CHUNK)
            acc = jnp.dot(h_ref[...], w_ref[:, cs], preferred_element_type=F32)
            o_ref[:, cs] = (acc if act is None else act(acc)).astype(o_ref.dtype)

    pl.when(col0 < M_OGATE_COL)(lambda: tile(None))
    pl.when(jnp.logical_and(col0 >= M_OGATE_COL, col0 < M_ZGATE_COL))(lambda: tile(_sigmoid))
    pl.when(col0 >= M_ZGATE_COL)(lambda: tile(_silu))

    @pl.when(pl.program_id(1) == 0)
    def _():
        g_ref[...] = jnp.dot(h_ref[...], wg_ref[...], preferred_element_type=F32)


def _inproj1(h, w, wg, tm=1024, tn=2048):
    t, k = h.shape
    n = M_MAIN_COLS
    tm = min(tm, t)
    blocks = (_nbytes((tm, k), BF16) + _nbytes((k, tn), BF16) + _nbytes((k, GATE_LANES), BF16)
              + _nbytes((tm, tn), BF16) + _nbytes((tm, GATE_LANES), F32))
    return pl.pallas_call(
        _inproj1_kernel,
        grid=(t // tm, n // tn),
        in_specs=[
            pl.BlockSpec((tm, k), lambda i, j: (i, 0)),
            pl.BlockSpec((k, tn), lambda i, j: (0, j)),
            pl.BlockSpec((k, GATE_LANES), lambda i, j: (0, 0)),
        ],
        out_specs=(pl.BlockSpec((tm, tn), lambda i, j: (i, j)), pl.BlockSpec((tm, GATE_LANES), lambda i, j: (i, 0))),
        out_shape=(jax.ShapeDtypeStruct((t, n), BF16), jax.ShapeDtypeStruct((t, GATE_LANES), F32)),
        compiler_params=pltpu.CompilerParams(
            dimension_semantics=("arbitrary", "arbitrary"),
            vmem_limit_bytes=_vmem_limit(blocks, 0, 4 * _nbytes((tm, INPROJ1_CHUNK), F32)),
        ),
        name="inproj1",
    )(h, w, wg)


def _scan_rows(x, rows, op, fill, reverse):
    n = x.shape[0]
    k = 1
    while k < n:
        if reverse:
            shifted = jnp.where(rows < n - k, pltpu.roll(x, n - k, 0), fill)
        else:
            shifted = jnp.where(rows >= k, pltpu.roll(x, k, 0), fill)
        x = op(x, shifted)
        k *= 2
    return x


GATE_PREP_CHUNKS = 4


def _gate_prep_kernel(g_ref, b_ref, o_ref):
    for ci in range(GATE_PREP_CHUNKS):
        _gate_prep_chunk(g_ref[0, ci * M_CHUNK:(ci + 1) * M_CHUNK, :] + b_ref[...], o_ref, ci)


def _gate_prep_chunk(g, o_ref, ci):
    rows = lax.broadcasted_iota(jnp.int32, g.shape, 0)
    kind = lax.broadcasted_iota(jnp.int32, g.shape, 1) % 8
    lf = jnp.minimum(g, 0.0) - jnp.log1p(jnp.exp(-jnp.abs(g)))
    csum = _scan_rows(lf, rows, jnp.add, 0.0, reverse=False)
    rsum = _scan_rows(lf, rows, jnp.add, 0.0, reverse=True)
    b_at_i = pltpu.roll(jnp.where(kind == 1, csum, rsum), LANES - 1, 1)
    grow = g - b_at_i
    cm_f = _scan_rows(grow, rows, jnp.maximum, NEG_INF, reverse=False)
    cm_b = _scan_rows(grow, rows, jnp.maximum, NEG_INF, reverse=True)
    y = jnp.where(kind == 0, b_at_i, 0.0)
    y = jnp.where(kind == 1, pltpu.roll(grow, 1, 1), y)
    y = jnp.where(kind == 2, pltpu.roll(cm_f, 2, 1), y)
    y = jnp.where(kind == 3, rsum, y)
    y = jnp.where(kind == 4, pltpu.roll(grow, 2, 1), y)
    y = jnp.where(kind == 5, pltpu.roll(cm_b, 3, 1), y)
    yt = y.T
    o_ref[0, :, ci] = yt[:M_HEADS * 8].reshape(M_HEADS, 8, g.shape[0])


def _gate_prep(gates, bias, bsz, seq):
    nc = seq // M_CHUNK
    per = GATE_PREP_CHUNKS
    return pl.pallas_call(
        _gate_prep_kernel,
        grid=(bsz, nc // per),
        in_specs=[pl.BlockSpec((1, per * M_CHUNK, GATE_LANES), lambda b, c: (b, c, 0)),
                  pl.BlockSpec((1, GATE_LANES), lambda b, c: (0, 0))],
        out_specs=pl.BlockSpec((1, M_HEADS, per, 8, M_CHUNK), lambda b, c: (b, 0, c, 0, 0)),
        out_shape=jax.ShapeDtypeStruct((bsz, M_HEADS, nc, 8, M_CHUNK), F32),
        compiler_params=pltpu.CompilerParams(dimension_semantics=("arbitrary", "arbitrary")),
        name="gate_prep",
    )(gates.reshape(bsz, seq, GATE_LANES), bias)


def _mlstm_kernel(q_ref, k_ref, v_ref, r_ref, o_ref, hs_scr, cf_scr, cb_scr, *, nc):
    L = M_CHUNK
    rows = lax.broadcasted_iota(jnp.int32, (L, L), 0)
    cols = lax.broadcasted_iota(jnp.int32, (L, L), 1)
    eye = rows == cols
    causal = cols <= rows
    anti = cols >= rows

    def to_col(r):
        return jnp.sum(jnp.where(eye, r, 0.0), axis=1, keepdims=True)

    def chunk(c, c_scr, n, m, fwd):
        st = pl.multiple_of(c * L, L)
        qc = q_ref[0, pl.ds(st, L), :] * (M_QK_DIM ** -0.5)
        kc = k_ref[0, pl.ds(st, L), :]
        vc = v_ref[0, pl.ds(st, L), :]
        r = r_ref[0, 0, pl.ds(c, 1)].reshape(8, L)
        base = 0 if fwd else 3
        grow = r[base + 1:base + 2]
        bcol = to_col(r[base:base + 1])
        gcol = to_col(grow)
        mm = jnp.maximum(to_col(r[base + 2:base + 3]), m)
        e = L - 1 if fwd else 0
        mm_l = mm[e:e + 1]
        s = lax.dot_general(qc, kc, (((1,), (1,)), ((), ())), preferred_element_type=F32)
        a = jnp.exp(jnp.where(causal if fwd else anti, grow - mm, NEG_INF)) * s
        gint = jnp.exp(m - mm)
        cmat = c_scr[...]
        num = (jnp.dot(a.astype(BF16), vc, preferred_element_type=F32)
               + gint * jnp.dot(qc, cmat.astype(BF16), preferred_element_type=F32))
        qn = jnp.sum(qc.astype(F32) * n, axis=1, keepdims=True)
        den = jnp.sum(a, axis=1, keepdims=True) + gint * qn
        hc = num * (1.0 / jnp.maximum(jnp.abs(den), jnp.exp(-(bcol + mm))))
        decay = jnp.exp(m - mm_l)
        wk = jnp.exp(gcol - mm_l) * kc.astype(F32)
        c_scr[...] = decay * cmat + lax.dot_general(wk.astype(BF16), vc, (((0,), (0,)), ((), ())),
                                                    preferred_element_type=F32)
        n_new = decay * n + jnp.sum(wk, axis=0, keepdims=True)
        m_new = bcol[e:e + 1] + mm_l
        return hc, n_new, m_new

    def finish(c, hsum):
        st = pl.multiple_of(c * L, L)
        o_ref[0, pl.ds(st, L), :] = hsum.astype(o_ref.dtype)

    def step(j, carry, second_half):
        nf, mf, nb, mb = carry
        cf, cb = j, nc - 1 - j
        hf, nf, mf = chunk(cf, cf_scr, nf, mf, True)
        hb, nb, mb = chunk(cb, cb_scr, nb, mb, False)
        sf = pl.multiple_of(cf * L, L)
        sb = pl.multiple_of(cb * L, L)
        if second_half:
            finish(cf, hs_scr[pl.ds(sf, L), :] + hf)
            finish(cb, hs_scr[pl.ds(sb, L), :] + hb)
        else:
            hs_scr[pl.ds(sf, L), :] = hf
            hs_scr[pl.ds(sb, L), :] = hb
        return nf, mf, nb, mb

    cf_scr[...] = jnp.zeros_like(cf_scr)
    cb_scr[...] = jnp.zeros_like(cb_scr)
    n0 = jnp.zeros((1, M_QK_DIM), F32)
    m0 = jnp.zeros((1, 1), F32)
    carry = lax.fori_loop(0, nc // 2, functools.partial(step, second_half=False), (n0, m0, n0, m0), unroll=4)
    lax.fori_loop(nc // 2, nc, functools.partial(step, second_half=True), carry, unroll=4)


def _mlstm(proj1, prep, bsz, seq):
    nc = seq // M_CHUNK
    assert nc % 2 == 0
    p3 = proj1.reshape(bsz, seq, M_MAIN_COLS)
    kb = M_QK_WIDTH // M_QK_DIM
    vb = 2 * M_QK_WIDTH // M_V_DIM
    blocks = (2 * _nbytes((seq, M_QK_DIM), BF16) + 2 * _nbytes((seq, M_V_DIM), BF16)
              + _nbytes((nc, 8, M_CHUNK), F32))
    scratch_bytes = _nbytes((seq, M_V_DIM), F32) + 2 * _nbytes((M_QK_DIM, M_V_DIM), F32)
    return pl.pallas_call(
        functools.partial(_mlstm_kernel, nc=nc),
        grid=(bsz, M_HEADS),
        in_specs=[
            pl.BlockSpec((1, seq, M_QK_DIM), lambda b, h: (b, 0, h)),
            pl.BlockSpec((1, seq, M_QK_DIM), lambda b, h: (b, 0, kb + h)),
            pl.BlockSpec((1, seq, M_V_DIM), lambda b, h: (b, 0, vb + h)),
            pl.BlockSpec((1, 1, nc, 8, M_CHUNK), lambda b, h: (b, h, 0, 0, 0)),
        ],
        out_specs=pl.BlockSpec((1, seq, M_V_DIM), lambda b, h: (b, 0, h)),
        out_shape=jax.ShapeDtypeStruct((bsz, seq, M_V_WIDTH), BF16),
        scratch_shapes=[pltpu.VMEM((seq, M_V_DIM), F32), pltpu.VMEM((M_QK_DIM, M_V_DIM), F32),
                        pltpu.VMEM((M_QK_DIM, M_V_DIM), F32)],
        compiler_params=pltpu.CompilerParams(
            dimension_semantics=("arbitrary", "arbitrary"),
            vmem_limit_bytes=_vmem_limit(blocks, scratch_bytes, 8 * 1024 * 1024),
        ),
        name="mlstm",
    )(p3, p3, p3, prep)


def _gate_weights(w_in1, b_gate):
    wg = w_in1[:, M_MAIN_COLS:].reshape(D_MODEL, 4, M_HEADS).transpose(0, 2, 1)
    wg = jnp.pad(wg, ((0, 0), (0, 0), (0, 4))).reshape(D_MODEL, M_HEADS * 8)
    wg = jnp.pad(wg, ((0, 0), (0, GATE_LANES - M_HEADS * 8)))
    bg = jnp.pad(b_gate.astype(F32).reshape(4, M_HEADS).T, ((0, 0), (0, 4))).reshape(1, M_HEADS * 8)
    bg = jnp.pad(bg, ((0, 0), (0, GATE_LANES - M_HEADS * 8)))
    return wg.astype(BF16), bg


def _trunk(x, p):
    bsz, seq, d = x.shape
    t = bsz * seq
    assert seq % (2 * PERM_BLOCK) == 0
    x2d = x.reshape(t, d)
    h0 = _rmsnorm_orders(x2d, p["l0_norm_pre"])
    proj0 = _inproj0(h0, p["l0_w_in"], p["rope"], seq)
    outs, lses = zip(*[_attention_group(proj0, g, bsz, seq) for g in range(N_GROUPS)])
    x1, h1 = _outproj0(outs, lses, proj0, p["l0_w_out"], x2d, p["l0_norm_post"], p["l1_norm_pre"])
    proj1, gates = _inproj1(h1, p["l1_w_main"], p["l1_w_gate"])
    prep = _gate_prep(gates, p["l1_b_gate"], bsz, seq)
    hs = _mlstm(proj1, prep, bsz, seq)
    y = _outproj1(hs.reshape(t, M_V_WIDTH), proj1, p["l1_head_norm"], p["l1_w_out"], x1, p["l1_norm_post"])
    return y.reshape(bsz, seq, d)


def kernel(x_prompt, x_sample, l0_norm_pre, l0_w_in, l0_w_out, l0_norm_post,
           l1_norm_pre, l1_w_in, l1_b_gate, l1_head_norm, l1_w_out, l1_norm_post):
    assert x_prompt.shape[1] == x_sample.shape[1]
    w_gate, b_gate = _gate_weights(l1_w_in, l1_b_gate)
    p = {
        "l0_norm_pre": l0_norm_pre, "l0_w_in": l0_w_in.astype(BF16), "l0_w_out": l0_w_out.astype(BF16),
        "l0_norm_post": l0_norm_post, "l1_norm_pre": l1_norm_pre,
        "l1_w_main": l1_w_in.astype(BF16), "l1_w_gate": w_gate, "l1_b_gate": b_gate,
        "l1_head_norm": l1_head_norm, "l1_w_out": l1_w_out.astype(BF16), "l1_norm_post": l1_norm_post,
        "rope": _rope_tables(x_prompt.shape[1]),
    }
    return (_trunk(x_prompt, p), _trunk(x_sample, p))
```

```python
import functools

import jax
import jax.numpy as jnp
from jax import lax
from jax.experimental import pallas as pl
from jax.experimental.pallas import tpu as pltpu

F32 = jnp.float32
BF16 = jnp.bfloat16

D_MODEL = 2048
ATT_GROUPS = ((128, 1), (512, 4), (2048, 16))
N_GROUPS = len(ATT_GROUPS)
ATT_HEADS = 16
ATT_HEAD_DIM = 128
ATT_WIDTH = ATT_HEADS * ATT_HEAD_DIM
ATT_IN_COLS = 3 * N_GROUPS * ATT_WIDTH + ATT_WIDTH
ATT_HALF = 64
ROPE_THETA = 500000.0
ROPE_DIMS = ATT_HEAD_DIM // 4
ROPE_HALF = ROPE_DIMS // 2
LOG2E = 1.4426950408889634
ATT_Q_SCALE = ATT_HEAD_DIM ** -0.5 * LOG2E

M_HEADS = 8
M_QK_DIM = 256
M_V_DIM = 512
M_QK_WIDTH = M_HEADS * M_QK_DIM
M_V_WIDTH = M_HEADS * M_V_DIM
M_MAIN_COLS = 2 * M_QK_WIDTH + 3 * M_V_WIDTH
M_CHUNK = 256
GATE_LANES = 128

NORM_EPS = 1e-6
NEG_INF = -1e30

LANES = 128
VMEM_CAP_BYTES = 56 * 1024 * 1024


def _vmem_limit(block_bytes, scratch_bytes=0, temp_bytes=0):
    need = 2 * block_bytes + scratch_bytes + temp_bytes
    return int(min(max(need, 16 * 1024 * 1024), VMEM_CAP_BYTES))


def _nbytes(shape, dtype):
    n = 1
    for s in shape:
        n *= s
    return n * jnp.dtype(dtype).itemsize


PERM_BLOCK = 1024


RMS_PIECE = 128


def _rmsnorm_orders_kernel(x_ref, g_ref, o_ref, slab_scr):
    nslab = x_ref.shape[1] // LANES
    pr = RMS_PIECE

    def piece(pi, _):
        r0 = pl.multiple_of(pi * pr, pr)
        x = x_ref[pl.ds(r0, pr), :]
        inv = lax.rsqrt(jnp.mean(x * x, axis=-1, keepdims=True) + NORM_EPS)
        for c in range(nslab):
            ls = slice(c * LANES, (c + 1) * LANES)
            y = x[:, ls] * inv * g_ref[:, ls]
            slab_scr[c] = y
            for g, (_, dil) in enumerate(ATT_GROUPS):
                if dil == 1:
                    o_ref[g, pl.ds(r0, pr), ls] = y.astype(o_ref.dtype)
        for g, (_, dil) in enumerate(ATT_GROUPS):
            if dil == 1:
                continue
            n = PERM_BLOCK // dil
            npc = pr // dil
            for r in range(dil):
                dst = pl.multiple_of(r * n + pi * npc, npc)
                for c in range(nslab):
                    o_ref[g, pl.ds(dst, npc), c * LANES:(c + 1) * LANES] = (
                        slab_scr[c, pl.ds(r, npc, stride=dil), :].astype(o_ref.dtype))
        return 0

    lax.fori_loop(0, x_ref.shape[0] // pr, piece, 0)


def _rmsnorm_orders(x2d, gain):
    t, d = x2d.shape
    tm = PERM_BLOCK
    return pl.pallas_call(
        _rmsnorm_orders_kernel,
        grid=(t // tm,),
        in_specs=[pl.BlockSpec((tm, d), lambda i: (i, 0)), pl.BlockSpec((1, d), lambda i: (0, 0))],
        out_specs=pl.BlockSpec((N_GROUPS, tm, d), lambda i: (0, i, 0)),
        out_shape=jax.ShapeDtypeStruct((N_GROUPS, t, d), BF16),
        scratch_shapes=[pltpu.VMEM((d // LANES, RMS_PIECE, LANES), F32)],
        compiler_params=pltpu.CompilerParams(
            dimension_semantics=("arbitrary",),
            vmem_limit_bytes=_vmem_limit(_nbytes((tm, d), F32) + _nbytes((N_GROUPS, tm, d), BF16),
                                         _nbytes((RMS_PIECE, d), F32), 2 * _nbytes((RMS_PIECE, d), F32)),
        ),
        name="rmsnorm_orders",
    )(x2d, gain.reshape(1, d).astype(F32))


def _order_index(seq, dil):
    n = PERM_BLOCK // dil
    return jnp.arange(seq, dtype=jnp.int32).reshape(seq // PERM_BLOCK, n, dil).transpose(0, 2, 1).reshape(seq)


INPROJ0_TN = 2048
INPROJ0_CHUNK = 256
INPROJ0_TAIL_ROWS = ((0, PERM_BLOCK // 2), (PERM_BLOCK // 2, 3 * PERM_BLOCK // 4), (3 * PERM_BLOCK // 4, PERM_BLOCK))
_GROUP_TILES = 3 * ATT_WIDTH // INPROJ0_TN
_Z_TILES = ATT_WIDTH // INPROJ0_TN


def _inproj0_col_tile(jj):
    return jnp.where(jj < _GROUP_TILES, jj,
                     jnp.where(jj < _GROUP_TILES + _Z_TILES, jj + (N_GROUPS - 1) * _GROUP_TILES, jj - _Z_TILES))


def _inproj0_order(jj):
    return jnp.where(jj < _GROUP_TILES + _Z_TILES, 0, (jj - _Z_TILES) // _GROUP_TILES)


def _inproj0_kernel(h_ref, w_ref, c_ref, s1_ref, s2_ref, o_ref):
    col = _inproj0_col_tile(pl.program_id(1))
    seg = (col * INPROJ0_TN) // ATT_WIDTH
    kind = seg % 3
    is_rope = jnp.logical_and(seg < 3 * N_GROUPS, kind < 2)
    nchunk = INPROJ0_TN // INPROJ0_CHUNK
    tm = o_ref.shape[0]

    def pieces():
        for cc in range(nchunk):
            cols = slice(cc * INPROJ0_CHUNK, (cc + 1) * INPROJ0_CHUNK)
            if cc < nchunk - 1:
                yield slice(0, tm), cols
            else:
                for lo, hi in INPROJ0_TAIL_ROWS:
                    yield slice(lo, hi), cols

    @pl.when(is_rope)
    def _():
        scale = jnp.where(kind == 0, ATT_Q_SCALE, 1.0).astype(F32)
        c_all = c_ref[...] * scale
        s1_all = s1_ref[...] * scale
        s2_all = s2_ref[...] * scale
        for rows, cols in pieces():
            c, s1, s2 = c_all[rows], s1_all[rows], s2_all[rows]
            acc = jnp.dot(h_ref[rows, :], w_ref[:, cols], preferred_element_type=F32)
            for t in range(INPROJ0_CHUNK // LANES):
                a = acc[:, t * LANES:(t + 1) * LANES]
                r = a * c + pltpu.roll(a, LANES - ROPE_HALF, 1) * s1 + pltpu.roll(a, ROPE_HALF, 1) * s2
                lo = cols.start + t * LANES
                o_ref[rows, lo:lo + LANES] = r.astype(o_ref.dtype)

    @pl.when(jnp.logical_not(is_rope))
    def _():
        for rows, cols in pieces():
            o_ref[rows, cols] = jnp.dot(h_ref[rows, :], w_ref[:, cols], preferred_element_type=F32).astype(o_ref.dtype)


def _rope_tables(seq):
    inv = jnp.power(ROPE_THETA, -jnp.arange(ROPE_HALF, dtype=F32) / ROPE_HALF)
    pos = jnp.stack([_order_index(seq, dil) for _, dil in ATT_GROUPS]).astype(F32)
    ang = pos[:, :, None] * inv[None, None, :]
    cos, sin = jnp.cos(ang), jnp.sin(ang)
    zeros = jnp.zeros((N_GROUPS, seq, LANES - ROPE_DIMS), F32)
    zh = jnp.zeros((N_GROUPS, seq, ROPE_HALF), F32)
    c = jnp.concatenate([cos, cos, jnp.ones((N_GROUPS, seq, LANES - ROPE_DIMS), F32)], axis=2)
    s1 = jnp.concatenate([-sin, zh, zeros], axis=2)
    s2 = jnp.concatenate([zh, sin, zeros], axis=2)
    return c, s1, s2


def _inproj0(h_orders, w, tables, seq):
    _, t, k = h_orders.shape
    n = w.shape[1]
    tm, tn = PERM_BLOCK, INPROJ0_TN
    pos_blocks = seq // tm
    tab_spec = pl.BlockSpec((None, tm, LANES), lambda i, j: (_inproj0_order(j), i % pos_blocks, 0))
    blocks = _nbytes((tm, k), BF16) + _nbytes((k, tn), BF16) + _nbytes((tm, tn), BF16) + 3 * _nbytes((tm, LANES), F32)
    return pl.pallas_call(
        _inproj0_kernel,
        grid=(t // tm, n // tn),
        in_specs=[
            pl.BlockSpec((None, tm, k), lambda i, j: (_inproj0_order(j), i, 0)),
            pl.BlockSpec((k, tn), lambda i, j: (0, _inproj0_col_tile(j))),
            tab_spec, tab_spec, tab_spec,
        ],
        out_specs=pl.BlockSpec((None, tm, tn), lambda i, j: (_inproj0_col_tile(j), i, 0)),
        out_shape=jax.ShapeDtypeStruct((n // tn, t, tn), BF16),
        compiler_params=pltpu.CompilerParams(
            dimension_semantics=("arbitrary", "arbitrary"),
            vmem_limit_bytes=_vmem_limit(blocks, 0, _nbytes((tm, k), BF16) + 3 * _nbytes((tm, LANES), F32)
                                         + 4 * _nbytes((tm, INPROJ0_CHUNK), F32)),
        ),
        name="inproj0",
    )(h_orders, w, *tables)


ATT_TQ = 512


def _attn_kernel(q_ref, kp_ref, kc_ref, kn_ref, vp_ref, vc_ref, vn_ref, o_ref, l_ref, qx, kx, vx, ox, lx, *, tq, ls):
    hb = ATT_HALF
    sub = 2 * hb
    w = qx.shape[2]
    rps = qx.shape[0]
    i = pl.program_id(2)
    bpt, n = q_ref.shape[1], q_ref.shape[3]

    row = lax.broadcasted_iota(jnp.int32, (sub, 2 * sub), 0)
    col = lax.broadcasted_iota(jnp.int32, (sub, 2 * sub), 1)
    band = jnp.abs(col - hb - row) <= hb
    lane = lax.broadcasted_iota(jnp.int32, (sub, LANES), 1)

    for rr in range(rps):
        qx[rr] = q_ref[0, :, rr].reshape(tq, w)
        kx[rr, 0:hb, :] = kp_ref[0, 0, rr]
        kx[rr, hb:hb + tq, :] = kc_ref[0, :, rr].reshape(tq, w)
        kx[rr, hb + tq:, :] = kn_ref[0, 0, rr]
        vx[rr, 0:hb, :] = vp_ref[0, 0, rr]
        vx[rr, hb:hb + tq, :] = vc_ref[0, :, rr].reshape(tq, w)
        vx[rr, hb + tq:, :] = vn_ref[0, 0, rr]

    for rr in range(rps):
        for a in range(tq // sub):
            r0 = a * sub
            kidx = i * tq + r0 + col - hb
            valid = band & (kidx >= 0) & (kidx < ls)
            lse_tile = jnp.zeros((sub, LANES), F32)
            for h in range(ATT_HEADS):
                hs = slice(h * ATT_HEAD_DIM, (h + 1) * ATT_HEAD_DIM)
                qh = qx[rr, r0:r0 + sub, hs]
                kh = kx[rr, r0:r0 + 2 * sub, hs]
                vh = vx[rr, r0:r0 + 2 * sub, hs]
                s = lax.dot_general(qh, kh, (((1,), (1,)), ((), ())), preferred_element_type=F32)
                s = jnp.where(valid, s, NEG_INF)
                m = jnp.max(s, axis=-1, keepdims=True)
                p = jnp.exp2(s - m)
                den = jnp.sum(p, axis=-1, keepdims=True)
                o = jnp.dot(p.astype(BF16), vh, preferred_element_type=F32) * (1.0 / den)
                lse_tile = jnp.where(lane == h, m + jnp.log2(den), lse_tile)
                ox[rr, r0:r0 + sub, hs] = o.astype(ox.dtype)
            lx[rr, r0:r0 + sub, :] = lse_tile
        o_ref[0, :, rr] = ox[rr].reshape(bpt, n, w)
        l_ref[0, :, rr] = lx[rr].reshape(bpt, n, LANES)


def _attention_group(proj, g, bsz, seq):
    _, dil = ATT_GROUPS[g]
    w = ATT_WIDTH
    hb = ATT_HALF
    ls = seq // dil
    if dil == 1:
        n = ATT_TQ // 2
        nb = seq // n
    else:
        nb, n = seq // PERM_BLOCK, PERM_BLOCK // dil
    tq = min(ATT_TQ, ls)
    bpt = tq // n
    rps = min(ATT_TQ // tq, dil)
    assert tq % (2 * hb) == 0 and n % hb == 0 and tq % n == 0 and nb % bpt == 0 and dil % rps == 0
    pv = proj.reshape(proj.shape[0], bsz, nb, dil, n, w)
    ncb = 3 * g

    grid = (bsz, dil // rps, nb // bpt)
    main = lambda cb: pl.BlockSpec((None, 1, bpt, rps, n, w), lambda b, r, i: (cb, b, i, r, 0, 0))
    prev_h = lambda cb: pl.BlockSpec((None, 1, 1, rps, hb, w),
                                     lambda b, r, i: (cb, b, jnp.maximum(i * bpt - 1, 0), r, n // hb - 1, 0))
    next_h = lambda cb: pl.BlockSpec((None, 1, 1, rps, hb, w),
                                     lambda b, r, i: (cb, b, jnp.minimum((i + 1) * bpt, nb - 1), r, 0, 0))
    out_block = lambda width: pl.BlockSpec((1, bpt, rps, n, width), lambda b, r, i: (b, i, r, 0, 0))

    in_specs = [main(ncb), prev_h(ncb + 1), main(ncb + 1), next_h(ncb + 1),
                prev_h(ncb + 2), main(ncb + 2), next_h(ncb + 2)]
    blocks = rps * (4 * _nbytes((tq, w), BF16) + 4 * _nbytes((hb, w), BF16) + _nbytes((tq, LANES), F32))
    scratch = [pltpu.VMEM((rps, tq, w), BF16), pltpu.VMEM((rps, tq + 2 * hb, w), BF16),
               pltpu.VMEM((rps, tq + 2 * hb, w), BF16), pltpu.VMEM((rps, tq, w), BF16),
               pltpu.VMEM((rps, tq, LANES), F32)]
    scratch_bytes = rps * (4 * _nbytes((tq + 2 * hb, w), BF16) + _nbytes((tq, LANES), F32))
    o, l = pl.pallas_call(
        functools.partial(_attn_kernel, tq=tq, ls=ls),
        grid=grid,
        in_specs=in_specs,
        out_specs=(out_block(w), out_block(LANES)),
        out_shape=(jax.ShapeDtypeStruct((bsz, nb, dil, n, w), BF16),
                   jax.ShapeDtypeStruct((bsz, nb, dil, n, LANES), F32)),
        scratch_shapes=scratch,
        compiler_params=pltpu.CompilerParams(
            dimension_semantics=("arbitrary", "arbitrary", "arbitrary"),
            vmem_limit_bytes=_vmem_limit(blocks, scratch_bytes, 8 * 1024 * 1024),
        ),
        name=f"attn_g{g}",
    )(*([pv] * 7))
    if dil > 1:
        o = o.transpose(0, 1, 3, 2, 4)
        l = l.transpose(0, 1, 3, 2, 4)
    return o.reshape(bsz * seq, w), l.reshape(bsz * seq, LANES)


def _post_norm_residual(out, x_ref, gpost_ref, o_ref, gnext_ref=None, hnext_ref=None):
    ms = jnp.mean(out * out, axis=-1, keepdims=True)
    x1 = x_ref[...] + out * lax.rsqrt(ms + NORM_EPS) * gpost_ref[...]
    o_ref[...] = x1
    if hnext_ref is not None:
        ms1 = jnp.mean(x1 * x1, axis=-1, keepdims=True)
        hnext_ref[...] = (x1 * lax.rsqrt(ms1 + NORM_EPS) * gnext_ref[...]).astype(hnext_ref.dtype)


OUTPROJ_KCHUNK = 512


def _sigmoid(x):
    return 0.5 * jnp.tanh(0.5 * x) + 0.5


def _silu(x):
    xh = 0.5 * x
    return xh * jnp.tanh(xh) + xh


def _outproj0_kernel(o0_ref, o1_ref, o2_ref, l0_ref, l1_ref, l2_ref, z_ref, w_ref, x_ref, gpost_ref, gnext_ref,
                     x1_ref, hnext_ref):
    o_refs = (o0_ref, o1_ref, o2_ref)
    lses = [r[...] for r in (l0_ref, l1_ref, l2_ref)]
    lmax = jnp.maximum(jnp.maximum(lses[0], lses[1]), lses[2])
    es = [jnp.exp2(l - lmax) for l in lses]
    tot = es[0] + es[1] + es[2]
    wts = [e / tot for e in es]
    heads_per_chunk = OUTPROJ_KCHUNK // ATT_HEAD_DIM
    half = heads_per_chunk // 2
    bounds = list(range(0, 2 * heads_per_chunk, half)) + list(range(2 * heads_per_chunk, ATT_HEADS + 1, heads_per_chunk))
    out = None
    for h0, h1 in zip(bounds[:-1], bounds[1:]):
        ys = []
        for h in range(h0, h1):
            hs = slice(h * ATT_HEAD_DIM, (h + 1) * ATT_HEAD_DIM)
            o = wts[0][:, h:h + 1] * o_refs[0][:, hs].astype(F32)
            for g in range(1, N_GROUPS):
                o = o + wts[g][:, h:h + 1] * o_refs[g][:, hs].astype(F32)
            ys.append((o * _silu(z_ref[:, hs].astype(F32))).astype(BF16))
        part = jnp.dot(jnp.concatenate(ys, axis=1), w_ref[h0 * ATT_HEAD_DIM:h1 * ATT_HEAD_DIM, :],
                       preferred_element_type=F32)
        out = part if out is None else out + part
    _post_norm_residual(out, x_ref, gpost_ref, x1_ref, gnext_ref, hnext_ref)


def _outproj0(outs, lses, proj0, w, x2d, g_post, g_next, tm=256):
    t, d = x2d.shape
    k = w.shape[0]
    row = lambda i: (i, 0)
    vec = pl.BlockSpec((1, d), lambda i: (0, 0))
    zb = 3 * N_GROUPS
    in_specs = ([pl.BlockSpec((tm, k), row)] * N_GROUPS + [pl.BlockSpec((tm, LANES), row)] * N_GROUPS
                + [pl.BlockSpec((None, tm, k), lambda i: (zb, i, 0)),
                   pl.BlockSpec((k, d), lambda i: (0, 0), pipeline_mode=pl.Buffered(1)),
                   pl.BlockSpec((tm, d), row), vec, vec])
    blocks = ((N_GROUPS + 1) * _nbytes((tm, k), BF16) + N_GROUPS * _nbytes((tm, LANES), F32)
              + 2 * _nbytes((tm, d), F32) + _nbytes((tm, d), BF16))
    return pl.pallas_call(
        _outproj0_kernel,
        grid=(t // tm,),
        in_specs=in_specs,
        out_specs=(pl.BlockSpec((tm, d), row), pl.BlockSpec((tm, d), row)),
        out_shape=(jax.ShapeDtypeStruct((t, d), F32), jax.ShapeDtypeStruct((t, d), BF16)),
        compiler_params=pltpu.CompilerParams(
            dimension_semantics=("arbitrary",),
            vmem_limit_bytes=_vmem_limit(blocks, _nbytes((k, d), BF16), 4 * _nbytes((tm, d), F32)),
        ),
        name="outproj0",
    )(*outs, *lses, proj0, w, x2d, g_post.reshape(1, d).astype(F32), g_next.reshape(1, d).astype(F32))


def _outproj1_kernel(hs_ref, og_ref, z_ref, hgain_ref, w_ref, x_ref, gpost_ref, o_ref):
    assert OUTPROJ_KCHUNK == M_V_DIM
    out = None
    for c in range(hs_ref.shape[1] // OUTPROJ_KCHUNK):
        ks = slice(c * OUTPROJ_KCHUNK, (c + 1) * OUTPROJ_KCHUNK)
        hs = hs_ref[:, ks].astype(F32)
        hn = hs * lax.rsqrt(jnp.mean(hs * hs, axis=-1, keepdims=True) + NORM_EPS) * hgain_ref[:, ks]
        y = (og_ref[:, ks].astype(F32) * hn) * z_ref[:, ks].astype(F32)
        part = jnp.dot(y.astype(BF16), w_ref[ks, :], preferred_element_type=F32)
        out = part if out is None else out + part
    _post_norm_residual(out, x_ref, gpost_ref, o_ref)


def _outproj1(hs, proj1, head_gain, w, x2d, g_post, tm=256):
    t, d = x2d.shape
    k = w.shape[0]
    row = lambda i: (i, 0)
    ob = M_OGATE_COL // k
    blocks = 3 * _nbytes((tm, k), BF16) + _nbytes((1, k), F32) + 2 * _nbytes((tm, d), F32)
    return pl.pallas_call(
        _outproj1_kernel,
        grid=(t // tm,),
        in_specs=[pl.BlockSpec((tm, k), row), pl.BlockSpec((tm, k), lambda i: (i, ob)),
                  pl.BlockSpec((tm, k), lambda i: (i, ob + 1)),
                  pl.BlockSpec((1, k), lambda i: (0, 0)),
                  pl.BlockSpec((k, d), lambda i: (0, 0), pipeline_mode=pl.Buffered(1)),
                  pl.BlockSpec((tm, d), row), pl.BlockSpec((1, d), lambda i: (0, 0))],
        out_specs=pl.BlockSpec((tm, d), row),
        out_shape=jax.ShapeDtypeStruct((t, d), F32),
        compiler_params=pltpu.CompilerParams(
            dimension_semantics=("arbitrary",),
            vmem_limit_bytes=_vmem_limit(blocks, _nbytes((k, d), BF16), 4 * _nbytes((tm, k), F32)),
        ),
        name="outproj1",
    )(hs, proj1, proj1, head_gain.reshape(1, k).astype(F32), w, x2d, g_post.reshape(1, d).astype(F32))


INPROJ1_CHUNK = 512


M_OGATE_COL = 2 * M_QK_WIDTH + M_V_WIDTH
M_ZGATE_COL = M_OGATE_COL + M_V_WIDTH


def _inproj1_kernel(h_ref, w_ref, wg_ref, o_ref, g_ref):
    tn = o_ref.shape[1]
    col0 = pl.program_id(1) * tn

    def tile(act):
        for cc in range(tn // INPROJ1_CHUNK):
            cs = slice(cc * INPROJ1_CHUNK, (cc + 1) * INPROJ1_CHUNK)
            acc = jnp.dot(h_ref[...], w_ref[:, cs], preferred_element_type=F32)
            o_ref[:, cs] = (acc if act is None else act(acc)).astype(o_ref.dtype)

    pl.when(col0 < M_OGATE_COL)(lambda: tile(None))
    pl.when(jnp.logical_and(col0 >= M_OGATE_COL, col0 < M_ZGATE_COL))(lambda: tile(_sigmoid))
    pl.when(col0 >= M_ZGATE_COL)(lambda: tile(_silu))

    @pl.when(pl.program_id(1) == 0)
    def _():
        g_ref[...] = jnp.dot(h_ref[...], wg_ref[...], preferred_element_type=F32)


def _inproj1(h, w, wg, tm=1024, tn=2048):
    t, k = h.shape
    n = M_MAIN_COLS
    tm = min(tm, t)
    blocks = (_nbytes((tm, k), BF16) + _nbytes((k, tn), BF16) + _nbytes((k, GATE_LANES), BF16)
              + _nbytes((tm, tn), BF16) + _nbytes((tm, GATE_LANES), F32))
    return pl.pallas_call(
        _inproj1_kernel,
        grid=(t // tm, n // tn),
        in_specs=[
            pl.BlockSpec((tm, k), lambda i, j: (i, 0)),
            pl.BlockSpec((k, tn), lambda i, j: (0, j)),
            pl.BlockSpec((k, GATE_LANES), lambda i, j: (0, 0)),
        ],
        out_specs=(pl.BlockSpec((tm, tn), lambda i, j: (i, j)), pl.BlockSpec((tm, GATE_LANES), lambda i, j: (i, 0))),
        out_shape=(jax.ShapeDtypeStruct((t, n), BF16), jax.ShapeDtypeStruct((t, GATE_LANES), F32)),
        compiler_params=pltpu.CompilerParams(
            dimension_semantics=("arbitrary", "arbitrary"),
            vmem_limit_bytes=_vmem_limit(blocks, 0, 4 * _nbytes((tm, INPROJ1_CHUNK), F32)),
        ),
        name="inproj1",
    )(h, w, wg)


def _scan_rows(x, rows, op, fill, reverse):
    n = x.shape[0]
    k = 1
    while k < n:
        if reverse:
            shifted = jnp.where(rows < n - k, pltpu.roll(x, n - k, 0), fill)
        else:
            shifted = jnp.where(rows >= k, pltpu.roll(x, k, 0), fill)
        x = op(x, shifted)
        k *= 2
    return x


GATE_PREP_CHUNKS = 4


def _gate_prep_kernel(g_ref, b_ref, o_ref):
    for ci in range(GATE_PREP_CHUNKS):
        _gate_prep_chunk(g_ref[0, ci * M_CHUNK:(ci + 1) * M_CHUNK, :] + b_ref[...], o_ref, ci)


def _gate_prep_chunk(g, o_ref, ci):
    rows = lax.broadcasted_iota(jnp.int32, g.shape, 0)
    kind = lax.broadcasted_iota(jnp.int32, g.shape, 1) % 8
    lf = jnp.minimum(g, 0.0) - jnp.log1p(jnp.exp(-jnp.abs(g)))
    csum = _scan_rows(lf, rows, jnp.add, 0.0, reverse=False)
    rsum = _scan_rows(lf, rows, jnp.add, 0.0, reverse=True)
    b_at_i = pltpu.roll(jnp.where(kind == 1, csum, rsum), LANES - 1, 1)
    grow = g - b_at_i
    cm_f = _scan_rows(grow, rows, jnp.maximum, NEG_INF, reverse=False)
    cm_b = _scan_rows(grow, rows, jnp.maximum, NEG_INF, reverse=True)
    y = jnp.where(kind == 0, b_at_i, 0.0)
    y = jnp.where(kind == 1, pltpu.roll(grow, 1, 1), y)
    y = jnp.where(kind == 2, pltpu.roll(cm_f, 2, 1), y)
    y = jnp.where(kind == 3, rsum, y)
    y = jnp.where(kind == 4, pltpu.roll(grow, 2, 1), y)
    y = jnp.where(kind == 5, pltpu.roll(cm_b, 3, 1), y)
    yt = y.T
    o_ref[0, :, ci] = yt[:M_HEADS * 8].reshape(M_HEADS, 8, g.shape[0])


def _gate_prep(gates, bias, bsz, seq):
    nc = seq // M_CHUNK
    per = GATE_PREP_CHUNKS
    return pl.pallas_call(
        _gate_prep_kernel,
        grid=(bsz, nc // per),
        in_specs=[pl.BlockSpec((1, per * M_CHUNK, GATE_LANES), lambda b, c: (b, c, 0)),
                  pl.BlockSpec((1, GATE_LANES), lambda b, c: (0, 0))],
        out_specs=pl.BlockSpec((1, M_HEADS, per, 8, M_CHUNK), lambda b, c: (b, 0, c, 0, 0)),
        out_shape=jax.ShapeDtypeStruct((bsz, M_HEADS, nc, 8, M_CHUNK), F32),
        compiler_params=pltpu.CompilerParams(dimension_semantics=("arbitrary", "arbitrary")),
        name="gate_prep",
    )(gates.reshape(bsz, seq, GATE_LANES), bias)


def _mlstm_kernel(q_ref, k_ref, v_ref, r_ref, o_ref, hs_scr, cf_scr, cb_scr, *, nc):
    L = M_CHUNK
    rows = lax.broadcasted_iota(jnp.int32, (L, L), 0)
    cols = lax.broadcasted_iota(jnp.int32, (L, L), 1)
    eye = rows == cols
    causal = cols <= rows
    anti = cols >= rows

    def to_col(r):
        return jnp.sum(jnp.where(eye, r, 0.0), axis=1, keepdims=True)

    def chunk(c, c_scr, n, m, fwd):
        st = pl.multiple_of(c * L, L)
        qc = q_ref[0, pl.ds(st, L), :] * (M_QK_DIM ** -0.5)
        kc = k_ref[0, pl.ds(st, L), :]
        vc = v_ref[0, pl.ds(st, L), :]
        r = r_ref[0, 0, pl.ds(c, 1)].reshape(8, L)
        base = 0 if fwd else 3
        grow = r[base + 1:base + 2]
        bcol = to_col(r[base:base + 1])
        gcol = to_col(grow)
        mm = jnp.maximum(to_col(r[base + 2:base + 3]), m)
        e = L - 1 if fwd else 0
        mm_l = mm[e:e + 1]
        s = lax.dot_general(qc, kc, (((1,), (1,)), ((), ())), preferred_element_type=F32)
        a = jnp.exp(jnp.where(causal if fwd else anti, grow - mm, NEG_INF)) * s
        gint = jnp.exp(m - mm)
        cmat = c_scr[...]
        num = (jnp.dot(a.astype(BF16), vc, preferred_element_type=F32)
               + gint * jnp.dot(qc, cmat.astype(BF16), preferred_element_type=F32))
        qn = jnp.sum(qc.astype(F32) * n, axis=1, keepdims=True)
        den = jnp.sum(a, axis=1, keepdims=True) + gint * qn
        hc = num * (1.0 / jnp.maximum(jnp.abs(den), jnp.exp(-(bcol + mm))))
        decay = jnp.exp(m - mm_l)
        wk = jnp.exp(gcol - mm_l) * kc.astype(F32)
        c_scr[...] = decay * cmat + lax.dot_general(wk.astype(BF16), vc, (((0,), (0,)), ((), ())),
                                                    preferred_element_type=F32)
        n_new = decay * n + jnp.sum(wk, axis=0, keepdims=True)
        m_new = bcol[e:e + 1] + mm_l
        return hc, n_new, m_new

    def finish(c, hsum):
        st = pl.multiple_of(c * L, L)
        o_ref[0, pl.ds(st, L), :] = hsum.astype(o_ref.dtype)

    def step(j, carry, second_half):
        nf, mf, nb, mb = carry
        cf, cb = j, nc - 1 - j
        hf, nf, mf = chunk(cf, cf_scr, nf, mf, True)
        hb, nb, mb = chunk(cb, cb_scr, nb, mb, False)
        sf = pl.multiple_of(cf * L, L)
        sb = pl.multiple_of(cb * L, L)
        if second_half:
            finish(cf, hs_scr[pl.ds(sf, L), :] + hf)
            finish(cb, hs_scr[pl.ds(sb, L), :] + hb)
        else:
            hs_scr[pl.ds(sf, L), :] = hf
            hs_scr[pl.ds(sb, L), :] = hb
        return nf, mf, nb, mb

    cf_scr[...] = jnp.zeros_like(cf_scr)
    cb_scr[...] = jnp.zeros_like(cb_scr)
    n0 = jnp.zeros((1, M_QK_DIM), F32)
    m0 = jnp.zeros((1, 1), F32)
    carry = lax.fori_loop(0, nc // 2, functools.partial(step, second_half=False), (n0, m0, n0, m0), unroll=4)
    lax.fori_loop(nc // 2, nc, functools.partial(step, second_half=True), carry, unroll=4)


def _mlstm(proj1, prep, bsz, seq):
    nc = seq // M_CHUNK
    assert nc % 2 == 0
    p3 = proj1.reshape(bsz, seq, M_MAIN_COLS)
    kb = M_QK_WIDTH // M_QK_DIM
    vb = 2 * M_QK_WIDTH // M_V_DIM
    blocks = (2 * _nbytes((seq, M_QK_DIM), BF16) + 2 * _nbytes((seq, M_V_DIM), BF16)
              + _nbytes((nc, 8, M_CHUNK), F32))
    scratch_bytes = _nbytes((seq, M_V_DIM), F32) + 2 * _nbytes((M_QK_DIM, M_V_DIM), F32)
    return pl.pallas_call(
        functools.partial(_mlstm_kernel, nc=nc),
        grid=(bsz, M_HEADS),
        in_specs=[
            pl.BlockSpec((1, seq, M_QK_DIM), lambda b, h: (b, 0, h)),
            pl.BlockSpec((1, seq, M_QK_DIM), lambda b, h: (b, 0, kb + h)),
            pl.BlockSpec((1, seq, M_V_DIM), lambda b, h: (b, 0, vb + h)),
            pl.BlockSpec((1, 1, nc, 8, M_CHUNK), lambda b, h: (b, h, 0, 0, 0)),
        ],
        out_specs=pl.BlockSpec((1, seq, M_V_DIM), lambda b, h: (b, 0, h)),
        out_shape=jax.ShapeDtypeStruct((bsz, seq, M_V_WIDTH), BF16),
        scratch_shapes=[pltpu.VMEM((seq, M_V_DIM), F32), pltpu.VMEM((M_QK_DIM, M_V_DIM), F32),
                        pltpu.VMEM((M_QK_DIM, M_V_DIM), F32)],
        compiler_params=pltpu.CompilerParams(
            dimension_semantics=("arbitrary", "arbitrary"),
            vmem_limit_bytes=_vmem_limit(blocks, scratch_bytes, 8 * 1024 * 1024),
        ),
        name="mlstm",
    )(p3, p3, p3, prep)


def _gate_weights(w_in1, b_gate):
    wg = w_in1[:, M_MAIN_COLS:].reshape(D_MODEL, 4, M_HEADS).transpose(0, 2, 1)
    wg = jnp.pad(wg, ((0, 0), (0, 0), (0, 4))).reshape(D_MODEL, M_HEADS * 8)
    wg = jnp.pad(wg, ((0, 0), (0, GATE_LANES - M_HEADS * 8)))
    bg = jnp.pad(b_gate.astype(F32).reshape(4, M_HEADS).T, ((0, 0), (0, 4))).reshape(1, M_HEADS * 8)
    bg = jnp.pad(bg, ((0, 0), (0, GATE_LANES - M_HEADS * 8)))
    return wg.astype(BF16), bg


def _trunk(x, p):
    bsz, seq, d = x.shape
    t = bsz * seq
    assert seq % (2 * PERM_BLOCK) == 0
    x2d = x.reshape(t, d)
    h0 = _rmsnorm_orders(x2d, p["l0_norm_pre"])
    proj0 = _inproj0(h0, p["l0_w_in"], p["rope"], seq)
    outs, lses = zip(*[_attention_group(proj0, g, bsz, seq) for g in range(N_GROUPS)])
    x1, h1 = _outproj0(outs, lses, proj0, p["l0_w_out"], x2d, p["l0_norm_post"], p["l1_norm_pre"])
    proj1, gates = _inproj1(h1, p["l1_w_main"], p["l1_w_gate"])
    prep = _gate_prep(gates, p["l1_b_gate"], bsz, seq)
    hs = _mlstm(proj1, prep, bsz, seq)
    y = _outproj1(hs.reshape(t, M_V_WIDTH), proj1, p["l1_head_norm"], p["l1_w_out"], x1, p["l1_norm_post"])
    return y.reshape(bsz, seq, d)


def kernel(x_prompt, x_sample, l0_norm_pre, l0_w_in, l0_w_out, l0_norm_post,
           l1_norm_pre, l1_w_in, l1_b_gate, l1_head_norm, l1_w_out, l1_norm_post):
    assert x_prompt.shape[1] == x_sample.shape[1]
    w_gate, b_gate = _gate_weights(l1_w_in, l1_b_gate)
    p = {
        "l0_norm_pre": l0_norm_pre, "l0_w_in": l0_w_in.astype(BF16), "l0_w_out": l0_w_out.astype(BF16),
        "l0_norm_post": l0_norm_post, "l1_norm_pre": l1_norm_pre,
        "l1_w_main": l1_w_in.astype(BF16), "l1_w_gate": w_gate, "l1_b_gate": b_gate,
        "l1_head_norm": l1_head_norm, "l1_w_out": l1_w_out.astype(BF16), "l1_norm_post": l1_norm_post,
        "rope": _rope_tables(x_prompt.shape[1]),
    }
    return (_trunk(x_prompt, p), _trunk(x_sample, p))
```
